```python
import math
import jax
import jax.numpy as jnp
from jax import lax
import numpy as np

D_MODEL = 2048
BATCH = 4
SEQ = 4096
DEPTH = 1

GRID_W = 64
CTX_LEN = 256
RW_HEADS = 16
HEAD_DIM = 64
RW_DIM = RW_HEADS * HEAD_DIM
DECAY_LORA = 64
ICLR_LORA = 64
GATE_LORA = 160
RW_COLS = 3 * RW_DIM + DECAY_LORA + ICLR_LORA + GATE_LORA
HY_DIM = D_MODEL - RW_DIM
HY_COLS = 3 * HY_DIM
IN_COLS = RW_COLS + HY_COLS
FILTER_WIDTH = 64
N_BANDS = 16
POS_EMB_DIM = 1 + 2 * N_BANDS
DECAY_TARGET = 1e-2
FAST_DECAY_PCT = 0.3
SLOW_DECAY_PCT = 1.5
D_FF = -(-8 * D_MODEL // (3 * 256)) * 256
ALPHA = (2 * DEPTH) ** 0.25
BETA = (8 * DEPTH) ** -0.25
LN_EPS = 1e-5
GN_EPS = 64e-5

kernel_name = 'hybrid_rwkv7_hyena_dit_block'


def layer_norm(h, g, b):
    hf = h.astype(jnp.float32)
    mu = jnp.mean(hf, -1, keepdims=True)
    var = jnp.mean(jnp.square(hf - mu), -1, keepdims=True)
    return ((hf - mu) * lax.rsqrt(var + LN_EPS) * g + b).astype(h.dtype)


def modulate(h, shift, scale):
    return h * (1 + scale) + shift


def conv3(h, w):
    hp = jnp.pad(h, ((0, 0), (1, 1), (0, 0)))
    return hp[:, :-2] * w[0] + hp[:, 1:-1] * w[1] + hp[:, 2:] * w[2]


def grid_pos_embed(rows):
    quarter = D_MODEL // 4
    half = D_MODEL // 2
    omega = 1.0 / (10000.0 ** (jnp.arange(quarter, dtype=jnp.float32) / quarter))
    er = jnp.arange(rows, dtype=jnp.float32)[:, None] * omega
    ec = jnp.arange(GRID_W, dtype=jnp.float32)[:, None] * omega
    er = jnp.concatenate([jnp.sin(er), jnp.cos(er)], -1)
    ec = jnp.concatenate([jnp.sin(ec), jnp.cos(ec)], -1)
    emb = jnp.concatenate([jnp.broadcast_to(er[:, None, :], (rows, GRID_W, half)),
                           jnp.broadcast_to(ec[None, :, :], (rows, GRID_W, half))], -1)
    return emb.reshape(rows * GRID_W, D_MODEL)


def rwkv7_scan(S0, r, w, k, v, a, b, reverse):
    def step(S, inp):
        rt, wt, kt, vt, at, bt = inp
        sa = jnp.einsum('bhij,bhj->bhi', S, at)
        S = S * wt[:, :, None, :] + sa[..., None] * bt[:, :, None, :] + vt[..., None] * kt[:, :, None, :]
        return S, jnp.einsum('bhij,bhj->bhi', S, rt)
    xs = tuple(jnp.swapaxes(t, 0, 1) for t in (r, w, k, v, a, b))
    S, ys = lax.scan(step, S0, xs, reverse=reverse)
    return S, jnp.swapaxes(ys, 0, 1)


def head_group_norm(y, g, b):
    mu = jnp.mean(y, -1, keepdims=True)
    var = jnp.mean(jnp.square(y - mu), -1, keepdims=True)
    return (y - mu) * lax.rsqrt(var + GN_EPS) * g.reshape(RW_HEADS, HEAD_DIM) + b.reshape(RW_HEADS, HEAD_DIM)


def rwkv7_time_mix(rw, S0_f, S0_b, w0_f, w_up_f, a0_f, a_up_f, w0_b, w_up_b, a0_b, a_up_b,
                   k_k, k_a, r_k, g_up, gn_g, gn_b):
    dt = rw.dtype
    rw = rw.astype(jnp.float32)
    B, L, _ = rw.shape
    heads = lambda t: t.reshape(B, L, RW_HEADS, HEAD_DIM)
    cuts = [RW_DIM, 2 * RW_DIM, 3 * RW_DIM, 3 * RW_DIM + DECAY_LORA, 3 * RW_DIM + DECAY_LORA + ICLR_LORA]
    r, k, v, w_lo, a_lo, g_lo = jnp.split(rw, cuts, axis=-1)
    kk = heads(k * k_k)
    kk = kk / jnp.maximum(jnp.linalg.norm(kk, axis=-1, keepdims=True), 1e-12)
    gate = jax.nn.sigmoid(g_lo) @ g_up
    tw = jnp.tanh(w_lo)
    rh, vh = heads(r), heads(v)
    outs, states = [], []
    for w0, w_up, a0, a_up, S0, rev in ((w0_f, w_up_f, a0_f, a_up_f, S0_f, False),
                                       (w0_b, w_up_b, a0_b, a_up_b, S0_b, True)):
        log_w = -jax.nn.softplus(-(w0 + tw @ w_up)) - 0.5
        decay = jnp.exp(-jnp.exp(log_w))
        a = jax.nn.sigmoid(a0 + a_lo @ a_up)
        kh = heads(k * (1 + (a - 1) * k_a))
        S, y = rwkv7_scan(S0, rh, heads(decay), kh, vh, -kk, kk * heads(a), rev)
        bonus = jnp.sum(rh * kh * r_k.reshape(RW_HEADS, HEAD_DIM), -1, keepdims=True) * vh
        outs.append(head_group_norm(y, gn_g, gn_b) + bonus)
        states.append(S)
    o = (outs[0] + outs[1]).reshape(B, L, RW_DIM) * gate
    return o.astype(dt), states[0], states[1]


def two_sided_filter(L, filt_w1, filt_b1, filt_w2, filt_b2, filt_w3, sin_freq):
    pos = jnp.arange(L, dtype=jnp.float32)[:, None]
    t = jnp.linspace(0.0, 1.0, L, dtype=jnp.float32)[:, None]
    bands = jnp.linspace(1e-4, N_BANDS - 1, N_BANDS, dtype=jnp.float32)[None, :]
    ang = 2.0 * math.pi * bands * pos / L
    z = jnp.concatenate([t, jnp.cos(ang), -jnp.sin(ang)], -1)
    h = jnp.sin(sin_freq * (z @ filt_w1 + filt_b1))
    h = jnp.sin(sin_freq * (h @ filt_w2 + filt_b2))
    h = h @ filt_w3
    max_decay = math.log(DECAY_TARGET) / FAST_DECAY_PCT
    min_decay = math.log(DECAY_TARGET) / SLOW_DECAY_PCT
    deltas = jnp.abs(jnp.linspace(min_decay, max_decay, HY_DIM, dtype=jnp.float32))
    window = jnp.exp(-t * deltas)
    h_f = h[:, :HY_DIM] * window
    h_b = h[:, HY_DIM:] * window
    return jnp.concatenate([h_f, jnp.zeros((1, HY_DIM), h_f.dtype), h_b[:0:-1]], 0)


def fft_long_conv(u, kern):
    L = u.shape[1]
    n = 2 * L
    uf = jnp.fft.rfft(u, n=n, axis=1)
    kf = jnp.fft.rfft(kern, n=n, axis=0)
    return jnp.fft.irfft(uf * kf[None], n=n, axis=1)[:, :L]


def hyena_mix(hy, conv_w, conv_b, filt_w1, filt_b1, filt_w2, filt_b2, filt_w3, sin_freq, bias):
    dt = hy.dtype
    hy = (conv3(hy, conv_w) + conv_b).astype(jnp.float32)
    x0, x1, v = jnp.split(hy, 3, axis=-1)
    kern = two_sided_filter(hy.shape[1], filt_w1, filt_b1, filt_w2, filt_b2, filt_w3, sin_freq)
    z = v * x1
    z = fft_long_conv(z, kern) + z * bias
    return (z * x0).astype(dt)


def swiglu(u, w1, w3, w2):
    return (jax.nn.silu(u @ w1) * (u @ w3)) @ w2


def setup_inputs(seed: int = 0) -> dict:
    key = jax.random.key(seed)
    ks = iter(jax.random.split(key, 48))

    def nrm(shape, scale):
        return jax.random.normal(next(ks), shape, jnp.float32) * scale

    def gain(shape):
        return 1.0 + nrm(shape, 0.02)

    Dm = D_MODEL
    centre = jnp.array([0.0, 1.0, 0.0], jnp.float32)[None, :, None]
    return {
        'x': nrm((BATCH, SEQ, Dm), 1.0),
        'c': nrm((BATCH, Dm), 1.0),
        'ctx': nrm((BATCH, CTX_LEN, Dm), 1.0),
        'c_ctx': nrm((Dm,), 1.0),
        'w_ada': nrm((DEPTH, Dm, 6 * Dm), Dm ** -0.5),
        'b_ada': nrm((DEPTH, 6 * Dm), 0.02),
        'w_in': nrm((DEPTH, Dm, IN_COLS), Dm ** -0.5),
        'conv_rw': centre + nrm((DEPTH, 3, RW_COLS), 0.3),
        'conv_hy': nrm((DEPTH, 3, HY_COLS), 3 ** -0.5),
        'conv_hy_b': nrm((DEPTH, HY_COLS), 0.02),
        'w0_f': nrm((DEPTH, RW_DIM), 0.5),
        'w_up_f': nrm((DEPTH, DECAY_LORA, RW_DIM), 0.5 * DECAY_LORA ** -0.5),
        'a0_f': nrm((DEPTH, RW_DIM), 0.5),
        'a_up_f': nrm((DEPTH, ICLR_LORA, RW_DIM), 0.5 * ICLR_LORA ** -0.5),
        'w0_b': nrm((DEPTH, RW_DIM), 0.5),
        'w_up_b': nrm((DEPTH, DECAY_LORA, RW_DIM), 0.5 * DECAY_LORA ** -0.5),
        'a0_b': nrm((DEPTH, RW_DIM), 0.5),
        'a_up_b': nrm((DEPTH, ICLR_LORA, RW_DIM), 0.5 * ICLR_LORA ** -0.5),
        'k_k': 0.85 + nrm((DEPTH, RW_DIM), 0.02),
        'k_a': gain((DEPTH, RW_DIM)),
        'r_k': nrm((DEPTH, RW_DIM), 0.1),
        'g_up': nrm((DEPTH, GATE_LORA, RW_DIM), GATE_LORA ** -0.5),
        'gn_g': gain((DEPTH, RW_DIM)),
        'gn_b': nrm((DEPTH, RW_DIM), 0.02),
        'filt_w1': nrm((DEPTH, POS_EMB_DIM, FILTER_WIDTH), POS_EMB_DIM ** -0.5),
        'filt_b1': nrm((DEPTH, FILTER_WIDTH), 0.02),
        'filt_w2': nrm((DEPTH, FILTER_WIDTH, FILTER_WIDTH), FILTER_WIDTH ** -0.5),
        'filt_b2': nrm((DEPTH, FILTER_WIDTH), 0.02),
        'filt_w3': nrm((DEPTH, FILTER_WIDTH, 2 * HY_DIM), 0.02 * FILTER_WIDTH ** -0.5),
        'sin_freq': gain((DEPTH, FILTER_WIDTH)),
        'hy_bias': nrm((DEPTH, HY_DIM), 0.5),
        'w_out': nrm((DEPTH, Dm, Dm), BETA * Dm ** -0.5),
        'ln1_g': gain((DEPTH, Dm)),
        'ln1_b': nrm((DEPTH, Dm), 0.02),
        'ffn_w1': nrm((DEPTH, Dm, D_FF), BETA * Dm ** -0.5),
        'ffn_w3': nrm((DEPTH, Dm, D_FF), BETA * Dm ** -0.5),
        'ffn_w2': nrm((DEPTH, D_FF, Dm), BETA * D_FF ** -0.5),
        'ln2_g': gain((DEPTH, Dm)),
        'ln2_b': nrm((DEPTH, Dm), 0.02),
    }


def reference(x, c, ctx, c_ctx, w_ada, b_ada, w_in, conv_rw, conv_hy, conv_hy_b,
              w0_f, w_up_f, a0_f, a_up_f, w0_b, w_up_b, a0_b, a_up_b,
              k_k, k_a, r_k, g_up, gn_g, gn_b,
              filt_w1, filt_b1, filt_w2, filt_b2, filt_w3, sin_freq, hy_bias,
              w_out, ln1_g, ln1_b, ffn_w1, ffn_w3, ffn_w2, ln2_g, ln2_b):
    B, L, _ = x.shape
    ROWS = L // GRID_W
    x = x + grid_pos_embed(ROWS).astype(x.dtype)[None]
    h_ctx = ctx
    zero_state = jnp.zeros((B, RW_HEADS, HEAD_DIM, HEAD_DIM), jnp.float32)
    for l in range(DEPTH):
        last = l == DEPTH - 1
        mod = jax.nn.silu(c) @ w_ada[l] + b_ada[l]
        mod_c = jax.nn.silu(c_ctx) @ w_ada[l] + b_ada[l]
        sh1, sc1, g1, sh2, sc2, g2 = jnp.split(mod[:, None, :], 6, axis=-1)
        csh1, csc1, cg1, csh2, csc2, cg2 = jnp.split(mod_c, 6)
        rw_p = dict(w0_f=w0_f[l], w_up_f=w_up_f[l], a0_f=a0_f[l], a_up_f=a_up_f[l],
                    w0_b=w0_b[l], w_up_b=w_up_b[l], a0_b=a0_b[l], a_up_b=a_up_b[l],
                    k_k=k_k[l], k_a=k_a[l], r_k=r_k[l], g_up=g_up[l], gn_g=gn_g[l], gn_b=gn_b[l])
        hy_p = dict(conv_w=conv_hy[l], conv_b=conv_hy_b[l], filt_w1=filt_w1[l], filt_b1=filt_b1[l],
                    filt_w2=filt_w2[l], filt_b2=filt_b2[l], filt_w3=filt_w3[l], sin_freq=sin_freq[l],
                    bias=hy_bias[l])
        proj = modulate(x, sh1, sc1) @ w_in[l]
        proj_c = modulate(h_ctx, csh1, csc1) @ w_in[l]
        o_rw_c, S_f, S_b = rwkv7_time_mix(conv3(proj_c[..., :RW_COLS], conv_rw[l]), zero_state, zero_state, **rw_p)
        o_rw, _, _ = rwkv7_time_mix(conv3(proj[..., :RW_COLS], conv_rw[l]), S_f, S_b, **rw_p)
        o_hy = hyena_mix(proj[..., RW_COLS:], **hy_p)
        mix = jnp.concatenate([o_rw, o_hy], -1) @ w_out[l]
        x = layer_norm(ALPHA * x + g1 * mix, ln1_g[l], ln1_b[l])
        ff = swiglu(modulate(x, sh2, sc2), ffn_w1[l], ffn_w3[l], ffn_w2[l])
        x = layer_norm(ALPHA * x + g2 * ff, ln2_g[l], ln2_b[l])
        if not last:
            o_hy_c = hyena_mix(proj_c[..., RW_COLS:], **hy_p)
            mix_c = jnp.concatenate([o_rw_c, o_hy_c], -1) @ w_out[l]
            h_ctx = layer_norm(ALPHA * h_ctx + cg1 * mix_c, ln1_g[l], ln1_b[l])
            ff_c = swiglu(modulate(h_ctx, csh2, csc2), ffn_w1[l], ffn_w3[l], ffn_w2[l])
            h_ctx = layer_norm(ALPHA * h_ctx + cg2 * ff_c, ln2_g[l], ln2_b[l])
    return x
```

```python
import functools
import math

import numpy as np
import jax
import jax.numpy as jnp
from jax import lax
from jax.experimental import pallas as pl
from jax.experimental.pallas import tpu as pltpu

F32 = jnp.float32
BF16 = jnp.bfloat16

GRID_W = 64
HEAD_DIM = 64
N_BANDS = 16
DECAY_TARGET = 1e-2
FAST_DECAY_PCT = 0.3
SLOW_DECAY_PCT = 1.5
LN_EPS = 1e-5
GN_EPS = 64e-5
ALPHA = 2.0 ** 0.25

LANE = 128
SUB = 8
VMEM_LIMIT = 56 * 1024 * 1024

SCAN_CHUNK = 64
SCAN_GROUP = 4
FFT_N2 = 64


def _cparams(sem):
    return pltpu.CompilerParams(dimension_semantics=sem, vmem_limit_bytes=VMEM_LIMIT)


def _dot(a, b, dims=(((1,), (0,)), ((), ()))):
    return lax.dot_general(a.astype(BF16), b.astype(BF16), dims, preferred_element_type=F32)


def _dot_nt(a, b):
    return _dot(a, b, (((1,), (1,)), ((), ())))


def _dot_tn(a, b):
    return _dot(a, b, (((0,), (0,)), ((), ())))


def _split2(a):
    hi = a.astype(BF16)
    lo = (a - hi.astype(F32)).astype(BF16)
    return hi, lo


def _dot3(a, b):
    ah, al = _split2(a)
    bh, bl = _split2(b)
    return _dot(ah, bh) + (_dot(al, bh) + _dot(ah, bl))


def _dot_exact_rhs(a, b_exact):
    a1 = a.astype(BF16)
    r1 = a - a1.astype(F32)
    a2 = r1.astype(BF16)
    a3 = (r1 - a2.astype(F32)).astype(BF16)
    return _dot(a1, b_exact) + (_dot(a2, b_exact) + _dot(a3, b_exact))


def _dot_exact_lhs(a_exact, b):
    b1 = b.astype(BF16)
    r1 = b - b1.astype(F32)
    b2 = r1.astype(BF16)
    b3 = (r1 - b2.astype(F32)).astype(BF16)
    return _dot(a_exact, b1) + (_dot(a_exact, b2) + _dot(a_exact, b3))


def _sigmoid(x):
    return 1.0 / (1.0 + jnp.exp(-x))


def _layer_norm(h, g, b):
    mu = jnp.mean(h, -1, keepdims=True)
    hc = h - mu
    var = jnp.mean(hc * hc, -1, keepdims=True)
    return hc * lax.rsqrt(var + LN_EPS) * g + b


def _conv3_rows(cur, prev_row, next_row, w):
    n = cur.shape[0]
    rows = lax.broadcasted_iota(jnp.int32, cur.shape, 0)
    up = jnp.where(rows == 0, prev_row, pltpu.roll(cur, 1, 0))
    dn = jnp.where(rows == n - 1, next_row, pltpu.roll(cur, n - 1, 0))
    return up * w[0:1] + cur * w[1:2] + dn * w[2:3]


def _adaln_kernel(c_ref, w_ref, b_ref, o_ref):
    cv = c_ref[...]
    a = cv * _sigmoid(cv)
    o_ref[...] = _dot3(a, w_ref[...]) + b_ref[...]


def _adaln(cc, w, b, tn=1024):
    m, d = cc.shape
    n = w.shape[1]
    return pl.pallas_call(
        _adaln_kernel,
        grid=(n // tn,),
        in_specs=[pl.BlockSpec((m, d), lambda j: (0, 0)),
                  pl.BlockSpec((d, tn), lambda j: (0, j)),
                  pl.BlockSpec((1, tn), lambda j: (0, j))],
        out_specs=pl.BlockSpec((m, tn), lambda j: (0, j)),
        out_shape=jax.ShapeDtypeStruct((m, n), F32),
        compiler_params=_cparams(("arbitrary",)),
        name="adaln",
    )(cc, w, b.reshape(1, n))


def _add_pos_rows(xv, er_ref, ec_ref, q):
    half = ec_ref.shape[1]
    return jnp.concatenate([xv[:, :half] + er_ref[q:q + 1, :], xv[:, half:] + ec_ref[...]], axis=1)


def _inproj_kernel(x_ref, er_ref, ec_ref, sh_ref, sc_ref, w_ref, o_ref, a_scr, *, add_pos):
    @pl.when(pl.program_id(2) == 0)
    def _():
        scale = 1.0 + sc_ref[0]
        shift = sh_ref[0]
        tm = x_ref.shape[1]
        if add_pos:
            for q in range(tm // GRID_W):
                rows = slice(q * GRID_W, (q + 1) * GRID_W)
                xv = _add_pos_rows(x_ref[0, rows, :], er_ref, ec_ref, q)
                a_scr[rows, :] = (xv * scale + shift).astype(BF16)
        else:
            a_scr[...] = (x_ref[0] * scale + shift).astype(BF16)

    o_ref[0] = jnp.dot(a_scr[...], w_ref[...], preferred_element_type=F32)


def _inproj(x, er, ec, mod3, w, *, tm, tn, add_pos, ctx_row):
    bsz, lx, d = x.shape
    n = w.shape[1]
    gr = tm // GRID_W
    if add_pos:
        er_spec = pl.BlockSpec((gr, d // 2), lambda b, i, j: (i, 0))
    else:
        er_spec = pl.BlockSpec((SUB, d // 2), lambda b, i, j: (0, 0))
    row = (lambda b: ctx_row) if ctx_row is not None else (lambda b: b)
    return pl.pallas_call(
        functools.partial(_inproj_kernel, add_pos=add_pos),
        grid=(bsz, lx // tm, n // tn),
        in_specs=[pl.BlockSpec((1, tm, d), lambda b, i, j: (b, i, 0)),
                  er_spec,
                  pl.BlockSpec((GRID_W, d // 2), lambda b, i, j: (0, 0)),
                  pl.BlockSpec((1, 1, d), lambda b, i, j: (row(b), 0, 0)),
                  pl.BlockSpec((1, 1, d), lambda b, i, j: (row(b), 0, 1)),
                  pl.BlockSpec((d, tn), lambda b, i, j: (0, j))],
        out_specs=pl.BlockSpec((1, tm, tn), lambda b, i, j: (b, i, j)),
        out_shape=jax.ShapeDtypeStruct((bsz, lx, n), F32),
        scratch_shapes=[pltpu.VMEM((tm, d), BF16)],
        compiler_params=_cparams(("arbitrary", "arbitrary", "arbitrary")),
        name="inproj_pos" if add_pos else "inproj_ctx",
    )(x, er, ec, mod3, mod3, w)


def _head_sum(x, ones_bd):
    outs = []
    for j in range(x.shape[1] // LANE):
        outs.append(_dot_exact_rhs(x[:, j * LANE:(j + 1) * LANE], ones_bd))
    return jnp.concatenate(outs, axis=1)


def _prep_kernel(p_rkv, p_rkv_prev, p_rkv_next, p_lo, p_lo_prev, p_lo_next, pc_rkv, pc_lo,
                 cw_rkv, cw_lo, wup, aup, gup, w0, a0, kk_w, ka_w, ones_ref,
                 r_out, v_out, kk_out, gate_out, lw_out, kh_out, b_out, *, n_lat):
    i = pl.program_id(1)
    is_ctx = i == n_lat
    no_prev = jnp.logical_or(is_ctx, i == 0)
    no_next = jnp.logical_or(is_ctx, i == n_lat - 1)
    rw = r_out.shape[2]

    def conv(main_ref, ctx_ref, prev_ref, next_ref, cw_ref, cols):
        cur = jnp.where(is_ctx, ctx_ref[0, :, cols], main_ref[0, :, cols])
        prev_row = jnp.where(no_prev, 0.0, prev_ref[0, SUB - 1:SUB, cols])
        next_row = jnp.where(no_next, 0.0, next_ref[0, 0:1, cols])
        return _conv3_rows(cur, prev_row, next_row, cw_ref[:, cols])

    def rkv(s):
        return conv(p_rkv, pc_rkv, p_rkv_prev, p_rkv_next, cw_rkv, slice(s * rw, (s + 1) * rw))

    ones_bd = ones_ref[...]
    k = rkv(1)
    kkr = k * kk_w[...]
    nrm = jnp.sqrt(_head_sum(kkr * kkr, ones_bd))
    kk = kkr / jnp.maximum(nrm, 1e-12)
    kk_out[0] = kk
    r_out[0] = rkv(0)
    v_out[0] = rkv(2)

    lo = conv(p_lo, pc_lo, p_lo_prev, p_lo_next, cw_lo, slice(0, p_lo.shape[2]))
    tw = jnp.tanh(lo[:, 0:LANE])
    al = lo[:, LANE:2 * LANE]
    gl = _sigmoid(lo[:, 2 * LANE:])
    gate_out[0] = _dot(gl, gup[...])
    wx = _dot3(tw, wup[...]) + w0[...]
    ax = _dot3(al, aup[...]) + a0[...]
    for d in range(2):
        cols = slice(d * rw, (d + 1) * rw)
        lw_out[d, 0] = -math.exp(-0.5) * _sigmoid(wx[:, cols])
        ia = _sigmoid(ax[:, cols])
        kh_out[d, 0] = k * (1.0 + (ia - 1.0) * ka_w[...])
        b_out[d, 0] = kk * ia


def _rwkv_prep(p, pc, cw_rkv, cw_lo, wup, aup, gup, w0, a0, k_k, k_a, *, rw, rkv_cols, lo_cols):
    bsz, lq, _ = p.shape
    ctx = pc.shape[1]
    tr = ctx
    n_lat = lq // tr
    t_all = lq + ctx
    hb = tr // SUB
    lo_w = lo_cols[1] - lo_cols[0]
    lo_blk = lo_cols[0] // lo_w
    lo_blk_c = rkv_cols // lo_w
    lat = lambda i: jnp.minimum(i, n_lat - 1)
    prev_blk = lambda i: jnp.clip(i * hb - 1, 0, lq // SUB - 1)
    next_blk = lambda i: jnp.clip((i + 1) * hb, 0, lq // SUB - 1)
    ones_bd = (np.arange(LANE)[:, None] // HEAD_DIM == np.arange(LANE)[None, :] // HEAD_DIM)
    ones_bd = jnp.asarray(ones_bd, BF16)
    full = lambda a: pl.BlockSpec(a.shape, lambda b, i: (0,) * a.ndim)
    row = lambda a: a.reshape(1, -1)
    weights = [cw_rkv, cw_lo, wup, aup, gup, row(w0), row(a0), row(k_k), row(k_a), ones_bd]
    shared = jax.ShapeDtypeStruct((bsz, t_all, rw), F32)
    perdir = jax.ShapeDtypeStruct((2, bsz, t_all, rw), F32)
    o_shared = pl.BlockSpec((1, tr, rw), lambda b, i: (b, i, 0))
    o_dir = pl.BlockSpec((2, 1, tr, rw), lambda b, i: (0, b, i, 0))
    return pl.pallas_call(
        functools.partial(_prep_kernel, n_lat=n_lat),
        grid=(bsz, n_lat + 1),
        in_specs=[pl.BlockSpec((1, tr, rkv_cols), lambda b, i: (b, lat(i), 0)),
                  pl.BlockSpec((1, SUB, rkv_cols), lambda b, i: (b, prev_blk(i), 0)),
                  pl.BlockSpec((1, SUB, rkv_cols), lambda b, i: (b, next_blk(i), 0)),
                  pl.BlockSpec((1, tr, lo_w), lambda b, i: (b, lat(i), lo_blk)),
                  pl.BlockSpec((1, SUB, lo_w), lambda b, i: (b, prev_blk(i), lo_blk)),
                  pl.BlockSpec((1, SUB, lo_w), lambda b, i: (b, next_blk(i), lo_blk)),
                  pl.BlockSpec((1, tr, rkv_cols), lambda b, i: (b, 0, 0)),
                  pl.BlockSpec((1, tr, lo_w), lambda b, i: (b, 0, lo_blk_c))]
                 + [full(a) for a in weights],
        out_specs=[o_shared, o_shared, o_shared, o_shared, o_dir, o_dir, o_dir],
        out_shape=[shared, shared, shared, shared, perdir, perdir, perdir],
        compiler_params=_cparams(("arbitrary", "arbitrary")),
        name="rwkv_prep",
    )(p, p, p, p, p, p, pc, pc, *weights)


def _scan_kernel(r_ref, v_ref, kk_ref, lw_ref, kh_ref, b_ref, rk_ref, gng_ref, gnb_ref, ones_ref,
                 o_ref, s_scr):
    d = pl.program_id(1)
    c = pl.program_id(2)
    C = r_ref.shape[1]
    G = SCAN_GROUP
    W = G * HEAD_DIM
    n = G * C
    n_groups = r_ref.shape[2] // W
    fwd = d == 0
    sgn = 1 - 2 * d

    @pl.when(c == 0)
    def _():
        s_scr[...] = jnp.zeros_like(s_scr)

    r = r_ref[0]
    v = v_ref[0]
    kk = kk_ref[0]
    lw = lw_ref[0, 0]
    kh = kh_ref[0, 0]
    bv = b_ref[0, 0]

    ti = lax.broadcasted_iota(jnp.int32, (C, C), 0)
    tj = lax.broadcasted_iota(jnp.int32, (C, C), 1)
    tri = jnp.where((ti - tj) * sgn >= 0, 1.0, 0.0).astype(BF16)
    cum = _dot_exact_lhs(tri, lw)
    tot = jnp.where(fwd, cum[C - 1:C, :], cum[0:1, :])
    e_in = jnp.exp(cum)
    e_ex = jnp.exp(cum - lw)
    e_neg = jnp.exp(-cum)
    e_end = jnp.exp(tot - cum)
    rt = r * e_in
    at = -kk * e_ex
    bt = bv * e_neg
    kt = kh * e_neg
    bp = bv * e_end
    kp = kh * e_end
    e_tot = jnp.exp(tot)

    bi = lax.broadcasted_iota(jnp.int32, (n, n), 0)
    bj = lax.broadcasted_iota(jnp.int32, (n, n), 1)
    same = (bi // C) == (bj // C)
    li = bi % C
    lj = bj % C
    before = (li - lj) * sgn > 0
    strict = jnp.logical_and(same, before)
    incl = jnp.logical_and(same, jnp.logical_or(before, li == lj))
    hrow = lax.broadcasted_iota(jnp.int32, (n, W), 0) // C
    hlane = lax.broadcasted_iota(jnp.int32, (n, W), 1) // HEAD_DIM
    hmask = hrow == hlane

    def rep(x):
        return jnp.where(hmask, jnp.concatenate([x] * G, axis=0), 0.0).astype(BF16)

    ys = []
    for g in range(n_groups):
        cols = slice(g * W, (g + 1) * W)
        a2, r2, b2, k2, v2 = rep(at[:, cols]), rep(rt[:, cols]), rep(bt[:, cols]), rep(kt[:, cols]), rep(v[:, cols])
        bp2, kp2 = rep(bp[:, cols]), rep(kp[:, cols])
        s0 = s_scr[g]
        ar = jnp.concatenate([a2, r2], axis=0)
        tt = _dot_nt(ar, jnp.concatenate([b2, k2], axis=0))
        a_ab = jnp.where(strict, tt[:n, :n], 0.0)
        a_ak = jnp.where(strict, tt[:n, n:], 0.0)
        a_rb = jnp.where(incl, tt[n:, :n], 0.0)
        a_rk = jnp.where(incl, tt[n:, n:], 0.0)
        xs = _dot_nt(ar, s0)
        u = xs[:n] + _dot(a_ak, v2)
        p = a_ab
        u = u + _dot(p, u)
        step = 2
        while step < C:
            p = _dot(p, p)
            u = u + _dot(p, u)
            step *= 2
        uv = jnp.concatenate([u.astype(BF16), v2], axis=0)
        y2 = xs[n:] + _dot(jnp.concatenate([a_rb, a_rk], axis=1), uv)
        y = y2[0:C]
        for h in range(1, G):
            y = y + y2[h * C:(h + 1) * C]
        ys.append(y)
        s_scr[g] = s0 * e_tot[:, cols] + _dot_tn(uv, jnp.concatenate([bp2, kp2], axis=0))

    y = jnp.concatenate(ys, axis=1)
    ones_bd = ones_ref[...]
    inv = 1.0 / HEAD_DIM
    mu = _head_sum(y, ones_bd) * inv
    yc = y - mu
    var = _head_sum(yc * yc, ones_bd) * inv
    gn = yc * lax.rsqrt(var + GN_EPS) * gng_ref[...] + gnb_ref[...]
    bonus = _head_sum(r * kh * rk_ref[...], ones_bd) * v
    o_ref[0, 0] = gn + bonus


def _rwkv_scan(r, v, kk, lw, kh, bvec, r_k, gn_g, gn_b, *, n_lat_tokens):
    bsz, t_all, rw = r.shape
    C = SCAN_CHUNK
    n_chunks = t_all // C
    n_ctx = (t_all - n_lat_tokens) // C
    n_lat = n_lat_tokens // C
    W = SCAN_GROUP * HEAD_DIM

    def tblk(d, c):
        fwd_blk = jnp.where(c < n_ctx, n_lat + c, c - n_ctx)
        return jnp.where(d == 0, fwd_blk, n_chunks - 1 - c)

    ones_bd = (np.arange(LANE)[:, None] // HEAD_DIM == np.arange(LANE)[None, :] // HEAD_DIM)
    ones_bd = jnp.asarray(ones_bd, BF16)
    shared = pl.BlockSpec((1, C, rw), lambda b, d, c: (b, tblk(d, c), 0))
    perdir = pl.BlockSpec((1, 1, C, rw), lambda b, d, c: (d, b, tblk(d, c), 0))
    vec = pl.BlockSpec((1, rw), lambda b, d, c: (0, 0))
    return pl.pallas_call(
        _scan_kernel,
        grid=(bsz, 2, n_chunks),
        in_specs=[shared, shared, shared, perdir, perdir, perdir, vec, vec, vec,
                  pl.BlockSpec((LANE, LANE), lambda b, d, c: (0, 0))],
        out_specs=pl.BlockSpec((1, 1, C, rw), lambda b, d, c: (d, b, tblk(d, c), 0)),
        out_shape=jax.ShapeDtypeStruct((2, bsz, t_all, rw), F32),
        scratch_shapes=[pltpu.VMEM((rw // W, W, W), F32)],
        compiler_params=_cparams(("arbitrary", "arbitrary", "arbitrary")),
        name="rwkv_scan",
    )(r, v, kk, lw, kh, bvec, r_k.reshape(1, rw), gn_g.reshape(1, rw), gn_b.reshape(1, rw), ones_bd)


def _fft_tables(L):
    n2 = FFT_N2
    n_all = 2 * L
    n1 = n_all // n2
    nh = n1 // 2
    two_pi = 2.0 * np.pi
    k1 = np.arange(n1)
    th1 = two_pi * np.outer(k1, np.arange(nh)) / n1
    f1 = np.concatenate([np.cos(th1), -np.sin(th1)], 0)
    th2 = two_pi * np.outer(np.arange(n2), np.arange(n2)) / n2
    fr, fi = np.cos(th2), -np.sin(th2)
    a3 = np.block([[fr, -fi], [fi, fr]])
    b3 = np.block([[-fi, -fr], [fr, -fi]])
    tht = two_pi * np.outer(k1, np.arange(n2)) / n_all
    twr = np.tile(np.cos(tht), (1, 2))
    twi = np.tile(-np.sin(tht), (1, 2))
    gr, gi = np.cos(th2), np.sin(th2)
    m3i = np.block([[gr, -gi], [gi, gr]]) / n_all
    th4 = two_pi * np.outer(np.arange(nh), k1) / n1
    cos4, sin4 = np.cos(th4), np.sin(th4)
    th5 = two_pi * np.outer(np.arange(n2), k1) / n_all
    t2r, t2i = np.cos(th5), np.sin(th5)
    f = lambda a: jnp.asarray(a, F32)
    return dict(f1=jnp.asarray(f1, BF16), a3=f(a3), b3=f(b3), twr=f(twr), twi=f(twi),
                m3i=jnp.asarray(m3i, BF16), cos4=f(cos4), sin4=f(sin4), t2r=f(t2r), t2i=f(t2i))


def _fft_stage1(src_ref, f1_ref, s_ref):
    two_n1, nh = f1_ref.shape
    n2 = FFT_N2

    def body(nlo, carry):
        xin = src_ref[pl.ds(nlo, nh, stride=n2), :]
        s_ref[pl.ds(pl.multiple_of(nlo * two_n1, two_n1), two_n1), :] = _dot(f1_ref[...], xin)
        return carry

    lax.fori_loop(0, n2, body, 0)


def _fft_mid_matrix(k1, a3_ref, b3_ref, twr_ref, twi_ref):
    tr = twr_ref[pl.ds(k1, 1), :]
    ti = twi_ref[pl.ds(k1, 1), :]
    return (a3_ref[...] * tr + b3_ref[...] * ti).astype(BF16)


def _fft_load_k1(s_ref, k1, two_n1):
    n2 = FFT_N2
    n1 = two_n1 // 2
    re = s_ref[pl.ds(k1, n2, stride=two_n1), :]
    im = s_ref[pl.ds(n1 + k1, n2, stride=two_n1), :]
    return jnp.concatenate([re, im], axis=0)


def _hyena_kernel(x0_ref, x1_ref, v_ref, cw0, cw1, cw2, cb0, cb1, cb2, bias_ref, kf_ref,
                  f1_ref, a3_ref, b3_ref, twr_ref, twi_ref, m3i_ref, cos4_ref, sin4_ref, t2r_ref, t2i_ref,
                  o_ref, z_scr, y_scr, s_scr):
    n2 = FFT_N2
    two_n1 = f1_ref.shape[0]
    n1 = two_n1 // 2
    nh = n1 // 2
    zero = jnp.zeros((1, x0_ref.shape[2]), F32)

    def sconv(ref, cw, cb):
        return _conv3_rows(ref[0], zero, zero, cw[...]) + cb[...]

    z_scr[...] = sconv(v_ref, cw2, cb2) * sconv(x1_ref, cw1, cb1)
    _fft_stage1(z_scr, f1_ref, s_scr)

    def mid(k1, carry):
        m = _fft_mid_matrix(k1, a3_ref, b3_ref, twr_ref, twi_ref)
        cc = _dot(m, _fft_load_k1(s_scr, k1, two_n1))
        kf = kf_ref[0, k1]
        cr, ci = cc[:n2], cc[n2:]
        kr, ki = kf[:n2], kf[n2:]
        pp = jnp.concatenate([cr * kr - ci * ki, cr * ki + ci * kr], axis=0)
        q = _dot(m3i_ref[...], pp)
        s_scr[pl.ds(k1, n2, stride=two_n1), :] = q[:n2]
        s_scr[pl.ds(n1 + k1, n2, stride=two_n1), :] = q[n2:]
        return carry

    lax.fori_loop(0, n1, mid, 0)

    def last(m2, carry):
        tr = t2r_ref[pl.ds(m2, 1), :]
        ti = t2i_ref[pl.ds(m2, 1), :]
        er = cos4_ref[...] * tr - sin4_ref[...] * ti
        ei = cos4_ref[...] * ti + sin4_ref[...] * tr
        f4 = jnp.concatenate([er, -ei], axis=1)
        qm = s_scr[pl.ds(pl.multiple_of(m2 * two_n1, two_n1), two_n1), :]
        y_scr[pl.ds(m2, nh, stride=n2), :] = _dot(f4, qm)
        return carry

    lax.fori_loop(0, n2, last, 0)
    z = z_scr[...]
    o_ref[0] = (y_scr[...] + z * bias_ref[...]) * sconv(x0_ref, cw0, cb0)


def _hyena_conv(p, conv_w, conv_b, hy_bias, kf, tabs, *, hy_col0, hy_dim, ct):
    bsz, L, _ = p.shape
    n_tiles = hy_dim // ct
    blk0 = hy_col0 // ct
    sec = hy_dim // ct
    xspec = lambda s: pl.BlockSpec((1, L, ct), lambda t, b: (b, 0, blk0 + s * sec + t))
    wspec = lambda s: pl.BlockSpec((3, ct), lambda t, b: (0, s * sec + t))
    bspec = lambda s: pl.BlockSpec((1, ct), lambda t, b: (0, s * sec + t))
    tab_list = [tabs[k] for k in ("f1", "a3", "b3", "twr", "twi", "m3i", "cos4", "sin4", "t2r", "t2i")]
    full = lambda a: pl.BlockSpec(a.shape, lambda t, b: (0,) * a.ndim)
    two_n1 = tabs["f1"].shape[0]
    return pl.pallas_call(
        _hyena_kernel,
        grid=(n_tiles, bsz),
        in_specs=[xspec(0), xspec(1), xspec(2), wspec(0), wspec(1), wspec(2), bspec(0), bspec(1), bspec(2),
                  pl.BlockSpec((1, ct), lambda t, b: (0, t)),
                  pl.BlockSpec((1,) + kf.shape[1:], lambda t, b: (t, 0, 0, 0))]
                 + [full(a) for a in tab_list],
        out_specs=pl.BlockSpec((1, L, ct), lambda t, b: (b, 0, t)),
        out_shape=jax.ShapeDtypeStruct((bsz, L, hy_dim), F32),
        scratch_shapes=[pltpu.VMEM((L, ct), F32), pltpu.VMEM((L, ct), F32),
                        pltpu.VMEM((FFT_N2 * two_n1, ct), F32)],
        compiler_params=_cparams(("arbitrary", "arbitrary")),
        name="hyena_conv",
    )(p, p, p, conv_w, conv_w, conv_w, conv_b.reshape(1, -1), conv_b.reshape(1, -1), conv_b.reshape(1, -1),
      hy_bias.reshape(1, -1), kf, *tab_list)


def _filter_kernel(z_ref, t_ref, w1_ref, b1_ref, w2_ref, b2_ref, w3f_ref, w3b_ref, freq_ref, delta_ref,
                   f1_ref, a3_ref, b3_ref, twr_ref, twi_ref, kf_ref, hf_scr, hb_scr, sf_scr, sb_scr):
    n2 = FFT_N2
    two_n1 = f1_ref.shape[0]
    n1 = two_n1 // 2
    freq = freq_ref[...]
    h = jnp.sin(freq * (_dot3(z_ref[...], w1_ref[...]) + b1_ref[...]))
    h = jnp.sin(freq * (_dot3(h, w2_ref[...]) + b2_ref[...]))
    window = jnp.exp(-t_ref[...] * delta_ref[...])
    hf_scr[...] = _dot3(h, w3f_ref[...]) * window
    hb = _dot3(h, w3b_ref[...]) * window
    rows = lax.broadcasted_iota(jnp.int32, hb.shape, 0)
    hb_scr[...] = jnp.where(rows == 0, 0.0, hb)
    _fft_stage1(hf_scr, f1_ref, sf_scr)
    _fft_stage1(hb_scr, f1_ref, sb_scr)

    def mid(k1, carry):
        m = _fft_mid_matrix(k1, a3_ref, b3_ref, twr_ref, twi_ref)
        cf = _dot(m, _fft_load_k1(sf_scr, k1, two_n1))
        cb = _dot(m, _fft_load_k1(sb_scr, k1, two_n1))
        kf_ref[0, k1] = jnp.concatenate([cf[:n2] + cb[:n2], cf[n2:] - cb[n2:]], axis=0)
        return carry

    lax.fori_loop(0, n1, mid, 0)


def _filter_spectrum(zpos, tcol, w1, b1, w2, b2, w3, freq, deltas, tabs, *, ct):
    L = zpos.shape[0]
    hy_dim = deltas.shape[1]
    n_tiles = hy_dim // ct
    two_n1 = tabs["f1"].shape[0]
    n1 = two_n1 // 2
    tab_list = [tabs[k] for k in ("f1", "a3", "b3", "twr", "twi")]
    full = lambda a: pl.BlockSpec(a.shape, lambda t: (0,) * a.ndim)
    row = lambda a: a.reshape(1, -1)
    small = [zpos, tcol, w1, row(b1), w2, row(b2)]
    return pl.pallas_call(
        _filter_kernel,
        grid=(n_tiles,),
        in_specs=[full(a) for a in small]
                 + [pl.BlockSpec((w3.shape[0], ct), lambda t: (0, t)),
                    pl.BlockSpec((w3.shape[0], ct), lambda t: (0, n_tiles + t)),
                    full(row(freq)),
                    pl.BlockSpec((1, ct), lambda t: (0, t))]
                 + [full(a) for a in tab_list],
        out_specs=pl.BlockSpec((1, n1, 2 * FFT_N2, ct), lambda t: (t, 0, 0, 0)),
        out_shape=jax.ShapeDtypeStruct((n_tiles, n1, 2 * FFT_N2, ct), F32),
        scratch_shapes=[pltpu.VMEM((L, ct), F32), pltpu.VMEM((L, ct), F32),
                        pltpu.VMEM((FFT_N2 * two_n1, ct), F32), pltpu.VMEM((FFT_N2 * two_n1, ct), F32)],
        compiler_params=_cparams(("arbitrary",)),
        name="hyena_filter",
    )(*small, w3, w3, row(freq), deltas, *tab_list)


def _outproj_kernel(of_ref, ob_ref, gate_ref, ohy_ref, x_ref, er_ref, ec_ref, g1_ref, w_ref, lng_ref, lnb_ref,
                    o_ref):
    rw = of_ref.shape[3]
    a_rw = ((of_ref[0, 0] + ob_ref[0, 0]) * gate_ref[0]).astype(BF16)
    mix = jnp.dot(a_rw, w_ref[0:rw, :], preferred_element_type=F32)
    mix = mix + jnp.dot(ohy_ref[0].astype(BF16), w_ref[rw:, :], preferred_element_type=F32)
    gmix = g1_ref[0] * mix
    tm = x_ref.shape[1]
    for q in range(tm // GRID_W):
        rows = slice(q * GRID_W, (q + 1) * GRID_W)
        xv = _add_pos_rows(x_ref[0, rows, :], er_ref, ec_ref, q)
        o_ref[0, rows, :] = _layer_norm(ALPHA * xv + gmix[rows], lng_ref[...], lnb_ref[...])


def _outproj(o_rw, gate, o_hy, x, er, ec, mod3, w_out, ln_g, ln_b, *, tm):
    bsz, L, d = x.shape
    rw = gate.shape[2]
    hy = o_hy.shape[2]
    gr = tm // GRID_W
    return pl.pallas_call(
        _outproj_kernel,
        grid=(bsz, L // tm),
        in_specs=[pl.BlockSpec((1, 1, tm, rw), lambda b, i: (0, b, i, 0)),
                  pl.BlockSpec((1, 1, tm, rw), lambda b, i: (1, b, i, 0)),
                  pl.BlockSpec((1, tm, rw), lambda b, i: (b, i, 0)),
                  pl.BlockSpec((1, tm, hy), lambda b, i: (b, i, 0)),
                  pl.BlockSpec((1, tm, d), lambda b, i: (b, i, 0)),
                  pl.BlockSpec((gr, d // 2), lambda b, i: (i, 0)),
                  pl.BlockSpec((GRID_W, d // 2), lambda b, i: (0, 0)),
                  pl.BlockSpec((1, 1, d), lambda b, i: (b, 0, 2)),
                  pl.BlockSpec((d, d), lambda b, i: (0, 0)),
                  pl.BlockSpec((1, d), lambda b, i: (0, 0)),
                  pl.BlockSpec((1, d), lambda b, i: (0, 0))],
        out_specs=pl.BlockSpec((1, tm, d), lambda b, i: (b, i, 0)),
        out_shape=jax.ShapeDtypeStruct((bsz, L, d), F32),
        compiler_params=_cparams(("arbitrary", "arbitrary")),
        name="outproj_ln",
    )(o_rw, o_rw, gate, o_hy, x, er, ec, mod3, w_out, ln_g.reshape(1, d), ln_b.reshape(1, d))


def _ffn_kernel(x_ref, sh_ref, sc_ref, g2_ref, w1_ref, w3_ref, w2_ref, lng_ref, lnb_ref, o_ref, a_scr, acc_scr):
    f = pl.program_id(2)

    @pl.when(f == 0)
    def _():
        a_scr[...] = (x_ref[0] * (1.0 + sc_ref[0]) + sh_ref[0]).astype(BF16)
        acc_scr[...] = jnp.zeros_like(acc_scr)

    a = a_scr[...]
    h1 = jnp.dot(a, w1_ref[...], preferred_element_type=F32)
    h3 = jnp.dot(a, w3_ref[...], preferred_element_type=F32)
    h = (h1 * _sigmoid(h1) * h3).astype(BF16)
    acc_scr[...] += jnp.dot(h, w2_ref[...], preferred_element_type=F32)

    @pl.when(f == pl.num_programs(2) - 1)
    def _():
        o_ref[0] = _layer_norm(ALPHA * x_ref[0] + g2_ref[0] * acc_scr[...], lng_ref[...], lnb_ref[...])


def _ffn(x1, mod3, w1, w3, w2, ln_g, ln_b, *, tm, tf):
    bsz, L, d = x1.shape
    dff = w1.shape[1]
    return pl.pallas_call(
        _ffn_kernel,
        grid=(bsz, L // tm, dff // tf),
        in_specs=[pl.BlockSpec((1, tm, d), lambda b, i, f: (b, i, 0)),
                  pl.BlockSpec((1, 1, d), lambda b, i, f: (b, 0, 3)),
                  pl.BlockSpec((1, 1, d), lambda b, i, f: (b, 0, 4)),
                  pl.BlockSpec((1, 1, d), lambda b, i, f: (b, 0, 5)),
                  pl.BlockSpec((d, tf), lambda b, i, f: (0, f)),
                  pl.BlockSpec((d, tf), lambda b, i, f: (0, f)),
                  pl.BlockSpec((tf, d), lambda b, i, f: (f, 0)),
                  pl.BlockSpec((1, d), lambda b, i, f: (0, 0)),
                  pl.BlockSpec((1, d), lambda b, i, f: (0, 0))],
        out_specs=pl.BlockSpec((1, tm, d), lambda b, i, f: (b, i, 0)),
        out_shape=jax.ShapeDtypeStruct((bsz, L, d), F32),
        scratch_shapes=[pltpu.VMEM((tm, d), BF16), pltpu.VMEM((tm, d), F32)],
        compiler_params=_cparams(("arbitrary", "arbitrary", "arbitrary")),
        name="ffn_ln",
    )(x1, mod3, mod3, mod3, w1, w3, w2, ln_g.reshape(1, d), ln_b.reshape(1, d))


def _pos_tables(rows, d):
    quarter = d // 4
    omega = 1.0 / (10000.0 ** (jnp.arange(quarter, dtype=F32) / quarter))
    er = jnp.arange(rows, dtype=F32)[:, None] * omega
    ec = jnp.arange(GRID_W, dtype=F32)[:, None] * omega
    er = jnp.concatenate([jnp.sin(er), jnp.cos(er)], -1)
    ec = jnp.concatenate([jnp.sin(ec), jnp.cos(ec)], -1)
    return er, ec


def _filter_positions(L):
    pos = jnp.arange(L, dtype=F32)[:, None]
    t = jnp.linspace(0.0, 1.0, L, dtype=F32)[:, None]
    bands = jnp.linspace(1e-4, N_BANDS - 1, N_BANDS, dtype=F32)[None, :]
    ang = 2.0 * math.pi * bands * pos / L
    z = jnp.concatenate([t, jnp.cos(ang), -jnp.sin(ang)], -1)
    return z, t


def _pad_cols(a, width):
    return jnp.pad(a, ((0, 0), (0, width - a.shape[1])))


def _pad_rows(a, height):
    return jnp.pad(a, ((0, height - a.shape[0]), (0, 0)))


def kernel(x, c, ctx, c_ctx, w_ada, b_ada, w_in, conv_rw, conv_hy, conv_hy_b, w0_f, w_up_f, a0_f, a_up_f, w0_b, w_up_b, a0_b, a_up_b, k_k, k_a, r_k, g_up, gn_g, gn_b, filt_w1, filt_b1, filt_w2, filt_b2, filt_w3, sin_freq, hy_bias, w_out, ln1_g, ln1_b, ffn_w1, ffn_w3, ffn_w2, ln2_g, ln2_b):
    bsz, L, d = x.shape
    rw = k_k.shape[1]
    hy = hy_bias.shape[1]
    dl, il, gl = w_up_f.shape[1], a_up_f.shape[1], g_up.shape[1]
    lo_w = 4 * LANE
    rkv_cols = 3 * rw

    wi = w_in[0]
    lo0 = rkv_cols

    def lora_layout(a):
        return jnp.concatenate([_pad_cols(a[:, lo0:lo0 + dl], LANE),
                                _pad_cols(a[:, lo0 + dl:lo0 + dl + il], LANE),
                                _pad_cols(a[:, lo0 + dl + il:lo0 + dl + il + gl], 2 * LANE)], axis=1)

    hy0 = lo0 + dl + il + gl
    w_lat = jnp.concatenate([wi[:, :rkv_cols], wi[:, hy0:], lora_layout(wi)], axis=1).astype(BF16)
    w_ctx = jnp.concatenate([wi[:, :rkv_cols], lora_layout(wi)], axis=1).astype(BF16)
    cw_rkv = conv_rw[0][:, :rkv_cols]
    cw_lo = lora_layout(conv_rw[0])
    wup = _pad_rows(jnp.concatenate([w_up_f[0], w_up_b[0]], axis=1), LANE)
    aup = _pad_rows(jnp.concatenate([a_up_f[0], a_up_b[0]], axis=1), LANE)
    gup = _pad_rows(g_up[0], 2 * LANE)
    w0 = jnp.concatenate([w0_f[0], w0_b[0]])
    a0 = jnp.concatenate([a0_f[0], a0_b[0]])

    cc = _pad_rows(jnp.concatenate([c, c_ctx[None, :]], axis=0), SUB)
    mod = _adaln(cc, w_ada[0], b_ada[0])
    mod3 = mod.reshape(SUB, 1, 6 * d)

    er, ec = _pos_tables(L // GRID_W, d)

    p_lat = _inproj(x, er, ec, mod3, w_lat, tm=512, tn=512, add_pos=True, ctx_row=None)
    p_ctx = _inproj(ctx, er, ec, mod3, w_ctx, tm=ctx.shape[1], tn=512, add_pos=False, ctx_row=bsz)

    lo_cols = (rkv_cols + 3 * hy, rkv_cols + 3 * hy + lo_w)
    r, v, kk, gate, lw, kh, bvec = _rwkv_prep(p_lat, p_ctx, cw_rkv, cw_lo, wup, aup, gup, w0, a0,
                                              k_k[0], k_a[0], rw=rw, rkv_cols=rkv_cols, lo_cols=lo_cols)
    o_rw = _rwkv_scan(r, v, kk, lw, kh, bvec, r_k[0], gn_g[0], gn_b[0], n_lat_tokens=L)

    tabs = _fft_tables(L)
    zpos, tcol = _filter_positions(L)
    max_decay = math.log(DECAY_TARGET) / FAST_DECAY_PCT
    min_decay = math.log(DECAY_TARGET) / SLOW_DECAY_PCT
    deltas = jnp.abs(jnp.linspace(min_decay, max_decay, hy, dtype=F32))[None, :]
    kf = _filter_spectrum(_pad_cols(zpos, LANE), tcol, _pad_rows(filt_w1[0], LANE), filt_b1[0], filt_w2[0],
                          filt_b2[0], filt_w3[0], sin_freq[0], deltas, tabs, ct=LANE)
    o_hy = _hyena_conv(p_lat, conv_hy[0], conv_hy_b[0], hy_bias[0], kf, tabs,
                       hy_col0=rkv_cols, hy_dim=hy, ct=LANE)

    x1 = _outproj(o_rw, gate, o_hy, x, er, ec, mod3, w_out[0].astype(BF16), ln1_g[0], ln1_b[0], tm=512)
    return _ffn(x1, mod3, ffn_w1[0].astype(BF16), ffn_w3[0].astype(BF16), ffn_w2[0].astype(BF16),
                ln2_g[0], ln2_b[0], tm=512, tf=512)
```

```python
import functools
import math

import numpy as np
import jax
import jax.numpy as jnp
from jax import lax
from jax.experimental import pallas as pl
from jax.experimental.pallas import tpu as pltpu

F32 = jnp.float32
BF16 = jnp.bfloat16

GRID_W = 64
HEAD_DIM = 64
N_BANDS = 16
DECAY_TARGET = 1e-2
FAST_DECAY_PCT = 0.3
SLOW_DECAY_PCT = 1.5
LN_EPS = 1e-5
GN_EPS = 64e-5
ALPHA = 2.0 ** 0.25

LANE = 128
SUB = 8
VMEM_LIMIT = 56 * 1024 * 1024

SCAN_CHUNK = 64
SCAN_GROUP = 4
FFT_N2 = 64
FFT_UNROLL = 4


def _cparams(sem):
    return pltpu.CompilerParams(dimension_semantics=sem, vmem_limit_bytes=VMEM_LIMIT)


def _dot(a, b, dims=(((1,), (0,)), ((), ()))):
    return lax.dot_general(a.astype(BF16), b.astype(BF16), dims, preferred_element_type=F32)


def _dot_nt(a, b):
    return _dot(a, b, (((1,), (1,)), ((), ())))


def _dot_tn(a, b):
    return _dot(a, b, (((0,), (0,)), ((), ())))


def _split2(a):
    hi = a.astype(BF16)
    lo = (a - hi.astype(F32)).astype(BF16)
    return hi, lo


def _dot3(a, b):
    ah, al = _split2(a)
    bh, bl = _split2(b)
    return _dot(ah, bh) + (_dot(al, bh) + _dot(ah, bl))


def _dot_exact_rhs(a, b_exact):
    a1 = a.astype(BF16)
    r1 = a - a1.astype(F32)
    a2 = r1.astype(BF16)
    a3 = (r1 - a2.astype(F32)).astype(BF16)
    return _dot(a1, b_exact) + (_dot(a2, b_exact) + _dot(a3, b_exact))


def _dot_exact_lhs(a_exact, b):
    b1 = b.astype(BF16)
    r1 = b - b1.astype(F32)
    b2 = r1.astype(BF16)
    b3 = (r1 - b2.astype(F32)).astype(BF16)
    return _dot(a_exact, b1) + (_dot(a_exact, b2) + _dot(a_exact, b3))


def _sigmoid(x):
    return 1.0 / (1.0 + jnp.exp(-x))


def _layer_norm(h, g, b):
    mu = jnp.mean(h, -1, keepdims=True)
    hc = h - mu
    var = jnp.mean(hc * hc, -1, keepdims=True)
    return hc * lax.rsqrt(var + LN_EPS) * g + b


def _conv3_rows(cur, prev_row, next_row, w):
    n = cur.shape[0]
    rows = lax.broadcasted_iota(jnp.int32, cur.shape, 0)
    up = jnp.where(rows == 0, prev_row, pltpu.roll(cur, 1, 0))
    dn = jnp.where(rows == n - 1, next_row, pltpu.roll(cur, n - 1, 0))
    return up * w[0:1] + cur * w[1:2] + dn * w[2:3]


def _adaln_kernel(c_ref, w_ref, b_ref, o_ref):
    cv = c_ref[...]
    a = cv * _sigmoid(cv)
    o_ref[...] = _dot3(a, w_ref[...]) + b_ref[...]


def _adaln(cc, w, b, tn=1024):
    m, d = cc.shape
    n = w.shape[1]
    return pl.pallas_call(
        _adaln_kernel,
        grid=(n // tn,),
        in_specs=[pl.BlockSpec((m, d), lambda j: (0, 0)),
                  pl.BlockSpec((d, tn), lambda j: (0, j)),
                  pl.BlockSpec((1, tn), lambda j: (0, j))],
        out_specs=pl.BlockSpec((m, tn), lambda j: (0, j)),
        out_shape=jax.ShapeDtypeStruct((m, n), F32),
        compiler_params=_cparams(("arbitrary",)),
        name="adaln",
    )(cc, w, b.reshape(1, n))


def _add_pos_rows(xv, er_ref, ec_ref, q):
    half = ec_ref.shape[1]
    return jnp.concatenate([xv[:, :half] + er_ref[q:q + 1, :], xv[:, half:] + ec_ref[...]], axis=1)


def _inproj_kernel(x_ref, er_ref, ec_ref, sh_ref, sc_ref, w_ref, o_ref, a_scr, *, add_pos):
    @pl.when(pl.program_id(2) == 0)
    def _():
        scale = 1.0 + sc_ref[0]
        shift = sh_ref[0]
        tm = x_ref.shape[1]
        if add_pos:
            for q in range(tm // GRID_W):
                rows = slice(q * GRID_W, (q + 1) * GRID_W)
                xv = _add_pos_rows(x_ref[0, rows, :], er_ref, ec_ref, q)
                a_scr[rows, :] = (xv * scale + shift).astype(BF16)
        else:
            a_scr[...] = (x_ref[0] * scale + shift).astype(BF16)

    o_ref[0] = jnp.dot(a_scr[...], w_ref[...], preferred_element_type=F32)


def _inproj(x, er, ec, mod3, w, *, tm, tn, add_pos, ctx_row):
    bsz, lx, d = x.shape
    n = w.shape[1]
    gr = tm // GRID_W
    if add_pos:
        er_spec = pl.BlockSpec((gr, d // 2), lambda b, i, j: (i, 0))
    else:
        er_spec = pl.BlockSpec((SUB, d // 2), lambda b, i, j: (0, 0))
    row = (lambda b: ctx_row) if ctx_row is not None else (lambda b: b)
    return pl.pallas_call(
        functools.partial(_inproj_kernel, add_pos=add_pos),
        grid=(bsz, lx // tm, n // tn),
        in_specs=[pl.BlockSpec((1, tm, d), lambda b, i, j: (b, i, 0)),
                  er_spec,
                  pl.BlockSpec((GRID_W, d // 2), lambda b, i, j: (0, 0)),
                  pl.BlockSpec((1, 1, d), lambda b, i, j: (row(b), 0, 0)),
                  pl.BlockSpec((1, 1, d), lambda b, i, j: (row(b), 0, 1)),
                  pl.BlockSpec((d, tn), lambda b, i, j: (0, j))],
        out_specs=pl.BlockSpec((1, tm, tn), lambda b, i, j: (b, i, j)),
        out_shape=jax.ShapeDtypeStruct((bsz, lx, n), F32),
        scratch_shapes=[pltpu.VMEM((tm, d), BF16)],
        compiler_params=_cparams(("arbitrary", "arbitrary", "arbitrary")),
        name="inproj_pos" if add_pos else "inproj_ctx",
    )(x, er, ec, mod3, mod3, w)


def _head_sum(x, ones_bd):
    outs = []
    for j in range(x.shape[1] // LANE):
        outs.append(_dot_exact_rhs(x[:, j * LANE:(j + 1) * LANE], ones_bd))
    return jnp.concatenate(outs, axis=1)


def _prep_kernel(p_rkv, p_rkv_prev, p_rkv_next, p_lo, p_lo_prev, p_lo_next, pc_rkv, pc_lo,
                 cw_rkv, cw_lo, wup, aup, gup, w0, a0, kk_w, ka_w, ones_ref,
                 r_out, v_out, kk_out, gate_out, lw_out, kh_out, b_out, *, n_lat):
    i = pl.program_id(1)
    is_ctx = i == n_lat
    no_prev = jnp.logical_or(is_ctx, i == 0)
    no_next = jnp.logical_or(is_ctx, i == n_lat - 1)
    rw = r_out.shape[2]

    def conv(main_ref, ctx_ref, prev_ref, next_ref, cw_ref, cols):
        cur = jnp.where(is_ctx, ctx_ref[0, :, cols], main_ref[0, :, cols])
        prev_row = jnp.where(no_prev, 0.0, prev_ref[0, SUB - 1:SUB, cols])
        next_row = jnp.where(no_next, 0.0, next_ref[0, 0:1, cols])
        return _conv3_rows(cur, prev_row, next_row, cw_ref[:, cols])

    def rkv(s):
        return conv(p_rkv, pc_rkv, p_rkv_prev, p_rkv_next, cw_rkv, slice(s * rw, (s + 1) * rw))

    ones_bd = ones_ref[...]
    k = rkv(1)
    kkr = k * kk_w[...]
    nrm = jnp.sqrt(_head_sum(kkr * kkr, ones_bd))
    kk = kkr / jnp.maximum(nrm, 1e-12)
    kk_out[0] = kk
    r_out[0] = rkv(0)
    v_out[0] = rkv(2)

    lo = conv(p_lo, pc_lo, p_lo_prev, p_lo_next, cw_lo, slice(0, p_lo.shape[2]))
    tw = jnp.tanh(lo[:, 0:LANE])
    al = lo[:, LANE:2 * LANE]
    gl = _sigmoid(lo[:, 2 * LANE:])
    gate_out[0] = _dot(gl, gup[...])
    wx = _dot3(tw, wup[...]) + w0[...]
    ax = _dot3(al, aup[...]) + a0[...]
    for d in range(2):
        cols = slice(d * rw, (d + 1) * rw)
        lw_out[d, 0] = -math.exp(-0.5) * _sigmoid(wx[:, cols])
        ia = _sigmoid(ax[:, cols])
        kh_out[d, 0] = k * (1.0 + (ia - 1.0) * ka_w[...])
        b_out[d, 0] = kk * ia


def _rwkv_prep(p, pc, cw_rkv, cw_lo, wup, aup, gup, w0, a0, k_k, k_a, *, rw, rkv_cols, lo_cols):
    bsz, lq, _ = p.shape
    ctx = pc.shape[1]
    tr = ctx
    n_lat = lq // tr
    t_all = lq + ctx
    hb = tr // SUB
    lo_w = lo_cols[1] - lo_cols[0]
    lo_blk = lo_cols[0] // lo_w
    lo_blk_c = rkv_cols // lo_w
    lat = lambda i: jnp.minimum(i, n_lat - 1)
    prev_blk = lambda i: jnp.clip(i * hb - 1, 0, lq // SUB - 1)
    next_blk = lambda i: jnp.clip((i + 1) * hb, 0, lq // SUB - 1)
    ones_bd = (np.arange(LANE)[:, None] // HEAD_DIM == np.arange(LANE)[None, :] // HEAD_DIM)
    ones_bd = jnp.asarray(ones_bd, BF16)
    full = lambda a: pl.BlockSpec(a.shape, lambda b, i: (0,) * a.ndim)
    row = lambda a: a.reshape(1, -1)
    weights = [cw_rkv, cw_lo, wup, aup, gup, row(w0), row(a0), row(k_k), row(k_a), ones_bd]
    shared = jax.ShapeDtypeStruct((bsz, t_all, rw), F32)
    perdir = jax.ShapeDtypeStruct((2, bsz, t_all, rw), F32)
    o_shared = pl.BlockSpec((1, tr, rw), lambda b, i: (b, i, 0))
    o_dir = pl.BlockSpec((2, 1, tr, rw), lambda b, i: (0, b, i, 0))
    return pl.pallas_call(
        functools.partial(_prep_kernel, n_lat=n_lat),
        grid=(bsz, n_lat + 1),
        in_specs=[pl.BlockSpec((1, tr, rkv_cols), lambda b, i: (b, lat(i), 0)),
                  pl.BlockSpec((1, SUB, rkv_cols), lambda b, i: (b, prev_blk(i), 0)),
                  pl.BlockSpec((1, SUB, rkv_cols), lambda b, i: (b, next_blk(i), 0)),
                  pl.BlockSpec((1, tr, lo_w), lambda b, i: (b, lat(i), lo_blk)),
                  pl.BlockSpec((1, SUB, lo_w), lambda b, i: (b, prev_blk(i), lo_blk)),
                  pl.BlockSpec((1, SUB, lo_w), lambda b, i: (b, next_blk(i), lo_blk)),
                  pl.BlockSpec((1, tr, rkv_cols), lambda b, i: (b, 0, 0)),
                  pl.BlockSpec((1, tr, lo_w), lambda b, i: (b, 0, lo_blk_c))]
                 + [full(a) for a in weights],
        out_specs=[o_shared, o_shared, o_shared, o_shared, o_dir, o_dir, o_dir],
        out_shape=[shared, shared, shared, shared, perdir, perdir, perdir],
        compiler_params=_cparams(("arbitrary", "arbitrary")),
        name="rwkv_prep",
    )(p, p, p, p, p, p, pc, pc, *weights)


def _scan_kernel(r_ref, v_ref, kk_ref, lw_ref, kh_ref, b_ref, rk_ref, gng_ref, gnb_ref, ones_ref,
                 o_ref, s_scr):
    d = pl.program_id(1)
    c = pl.program_id(2)
    C = r_ref.shape[1]
    G = SCAN_GROUP
    W = G * HEAD_DIM
    n = G * C
    n_groups = r_ref.shape[2] // W
    fwd = d == 0
    sgn = 1 - 2 * d

    @pl.when(c == 0)
    def _():
        s_scr[...] = jnp.zeros_like(s_scr)

    r = r_ref[0]
    v = v_ref[0]
    kk = kk_ref[0]
    lw = lw_ref[0, 0]
    kh = kh_ref[0, 0]
    bv = b_ref[0, 0]

    ti = lax.broadcasted_iota(jnp.int32, (C, C), 0)
    tj = lax.broadcasted_iota(jnp.int32, (C, C), 1)
    tri = jnp.where((ti - tj) * sgn >= 0, 1.0, 0.0).astype(BF16)
    cum = _dot_exact_lhs(tri, lw)
    tot = jnp.where(fwd, cum[C - 1:C, :], cum[0:1, :])
    e_in = jnp.exp(cum)
    e_ex = jnp.exp(cum - lw)
    e_neg = jnp.exp(-cum)
    e_end = jnp.exp(tot - cum)
    rt = r * e_in
    at = -kk * e_ex
    bt = bv * e_neg
    kt = kh * e_neg
    bp = bv * e_end
    kp = kh * e_end
    e_tot = jnp.exp(tot)

    bi = lax.broadcasted_iota(jnp.int32, (n, n), 0)
    bj = lax.broadcasted_iota(jnp.int32, (n, n), 1)
    same = (bi // C) == (bj // C)
    li = bi % C
    lj = bj % C
    before = (li - lj) * sgn > 0
    strict = jnp.logical_and(same, before)
    incl = jnp.logical_and(same, jnp.logical_or(before, li == lj))
    hrow = lax.broadcasted_iota(jnp.int32, (n, W), 0) // C
    hlane = lax.broadcasted_iota(jnp.int32, (n, W), 1) // HEAD_DIM
    hmask = hrow == hlane

    def rep(x):
        return jnp.where(hmask, jnp.concatenate([x] * G, axis=0), 0.0).astype(BF16)

    groups = range(n_groups)
    col = [slice(g * W, (g + 1) * W) for g in groups]
    v2 = [rep(v[:, col[g]]) for g in groups]
    ar = [jnp.concatenate([rep(at[:, col[g]]), rep(rt[:, col[g]])], axis=0) for g in groups]
    bk = [jnp.concatenate([rep(bt[:, col[g]]), rep(kt[:, col[g]])], axis=0) for g in groups]
    s0 = [s_scr[g] for g in groups]
    tt = [_dot_nt(ar[g], bk[g]) for g in groups]
    xs = [_dot_nt(ar[g], s0[g]) for g in groups]
    p = [jnp.where(strict, tt[g][:n, :n], 0.0).astype(BF16) for g in groups]
    u = [xs[g][:n] + _dot(jnp.where(strict, tt[g][:n, n:], 0.0), v2[g]) for g in groups]
    u = [u[g] + _dot(p[g], u[g]) for g in groups]
    step = 2
    while step < C:
        p = [_dot(p[g], p[g]).astype(BF16) for g in groups]
        u = [u[g] + _dot(p[g], u[g]) for g in groups]
        step *= 2
    uv = [jnp.concatenate([u[g].astype(BF16), v2[g]], axis=0) for g in groups]
    a_r = [jnp.concatenate([jnp.where(incl, tt[g][n:, :n], 0.0), jnp.where(incl, tt[g][n:, n:], 0.0)],
                           axis=1) for g in groups]
    y2 = [xs[g][n:] + _dot(a_r[g], uv[g]) for g in groups]
    for g in groups:
        bkp = jnp.concatenate([rep(bp[:, col[g]]), rep(kp[:, col[g]])], axis=0)
        s_scr[g] = s0[g] * e_tot[:, col[g]] + _dot_tn(uv[g], bkp)
    ys = []
    for g in groups:
        y = y2[g][0:C]
        for h in range(1, G):
            y = y + y2[g][h * C:(h + 1) * C]
        ys.append(y)

    y = jnp.concatenate(ys, axis=1)
    ones_bd = ones_ref[...]
    inv = 1.0 / HEAD_DIM
    mu = _head_sum(y, ones_bd) * inv
    yc = y - mu
    var = _head_sum(yc * yc, ones_bd) * inv
    gn = yc * lax.rsqrt(var + GN_EPS) * gng_ref[...] + gnb_ref[...]
    bonus = _head_sum(r * kh * rk_ref[...], ones_bd) * v
    o_ref[0, 0] = gn + bonus


def _rwkv_scan(r, v, kk, lw, kh, bvec, r_k, gn_g, gn_b, *, n_lat_tokens):
    bsz, t_all, rw = r.shape
    C = SCAN_CHUNK
    n_chunks = t_all // C
    n_ctx = (t_all - n_lat_tokens) // C
    n_lat = n_lat_tokens // C
    W = SCAN_GROUP * HEAD_DIM

    def tblk(d, c):
        fwd_blk = jnp.where(c < n_ctx, n_lat + c, c - n_ctx)
        return jnp.where(d == 0, fwd_blk, n_chunks - 1 - c)

    ones_bd = (np.arange(LANE)[:, None] // HEAD_DIM == np.arange(LANE)[None, :] // HEAD_DIM)
    ones_bd = jnp.asarray(ones_bd, BF16)
    shared = pl.BlockSpec((1, C, rw), lambda b, d, c: (b, tblk(d, c), 0))
    perdir = pl.BlockSpec((1, 1, C, rw), lambda b, d, c: (d, b, tblk(d, c), 0))
    vec = pl.BlockSpec((1, rw), lambda b, d, c: (0, 0))
    return pl.pallas_call(
        _scan_kernel,
        grid=(bsz, 2, n_chunks),
        in_specs=[shared, shared, shared, perdir, perdir, perdir, vec, vec, vec,
                  pl.BlockSpec((LANE, LANE), lambda b, d, c: (0, 0))],
        out_specs=pl.BlockSpec((1, 1, C, rw), lambda b, d, c: (d, b, tblk(d, c), 0)),
        out_shape=jax.ShapeDtypeStruct((2, bsz, t_all, rw), F32),
        scratch_shapes=[pltpu.VMEM((rw // W, W, W), F32)],
        compiler_params=_cparams(("arbitrary", "arbitrary", "arbitrary")),
        name="rwkv_scan",
    )(r, v, kk, lw, kh, bvec, r_k.reshape(1, rw), gn_g.reshape(1, rw), gn_b.reshape(1, rw), ones_bd)


def _fft_tables(L):
    n2 = FFT_N2
    n_all = 2 * L
    n1 = n_all // n2
    nh = n1 // 2
    two_pi = 2.0 * np.pi
    k1 = np.arange(n1)
    th1 = two_pi * np.outer(k1, np.arange(nh)) / n1
    f1 = np.concatenate([np.cos(th1), -np.sin(th1)], 0)
    th2 = two_pi * np.outer(np.arange(n2), np.arange(n2)) / n2
    fr, fi = np.cos(th2), -np.sin(th2)
    a3 = np.block([[fr, -fi], [fi, fr]])
    b3 = np.block([[-fi, -fr], [fr, -fi]])
    tht = two_pi * np.outer(k1, np.arange(n2)) / n_all
    twr = np.tile(np.cos(tht), (1, 2))
    twi = np.tile(-np.sin(tht), (1, 2))
    gr, gi = np.cos(th2), np.sin(th2)
    m3i = np.block([[gr, -gi], [gi, gr]]) / n_all
    th4 = two_pi * np.outer(np.arange(nh), k1) / n1
    cos4, sin4 = np.cos(th4), np.sin(th4)
    th5 = two_pi * np.outer(np.arange(n2), k1) / n_all
    t2r, t2i = np.cos(th5), np.sin(th5)
    f = lambda a: jnp.asarray(a, F32)
    return dict(f1=f(f1), a3=f(a3), b3=f(b3), twr=f(twr), twi=f(twi),
                m3i=f(m3i), cos4=f(cos4), sin4=f(sin4), t2r=f(t2r), t2i=f(t2i))


def _fft_stage1(src_ref, f1_ref, s_ref):
    two_n1, nh = f1_ref.shape
    n2 = FFT_N2
    f1 = f1_ref[...].astype(BF16)

    def body(nlo, carry):
        xin = src_ref[pl.ds(nlo, nh, stride=n2), :]
        s_ref[pl.ds(pl.multiple_of(nlo * two_n1, two_n1), two_n1), :] = _dot(f1, xin)
        return carry

    lax.fori_loop(0, n2, body, 0, unroll=FFT_UNROLL)


def _fft_mid_matrix(k1, a3_ref, b3_ref, twr_ref, twi_ref):
    tr = twr_ref[pl.ds(k1, 1), :]
    ti = twi_ref[pl.ds(k1, 1), :]
    return (a3_ref[...] * tr + b3_ref[...] * ti).astype(BF16)


def _fft_load_k1(s_ref, k1, two_n1):
    n2 = FFT_N2
    n1 = two_n1 // 2
    re = s_ref[pl.ds(k1, n2, stride=two_n1), :]
    im = s_ref[pl.ds(n1 + k1, n2, stride=two_n1), :]
    return jnp.concatenate([re, im], axis=0)


def _hyena_kernel(x0_ref, x1_ref, v_ref, cw0, cw1, cw2, cb0, cb1, cb2, bias_ref, kf_ref,
                  f1_ref, a3_ref, b3_ref, twr_ref, twi_ref, m3i_ref, cos4_ref, sin4_ref, t2r_ref, t2i_ref,
                  o_ref, z_scr, y_scr, s_scr, q_scr):
    n2 = FFT_N2
    two_n1 = f1_ref.shape[0]
    n1 = two_n1 // 2
    nh = n1 // 2
    zero = jnp.zeros((1, x0_ref.shape[2]), F32)

    def sconv(ref, cw, cb):
        return _conv3_rows(ref[0], zero, zero, cw[...]) + cb[...]

    z_scr[...] = sconv(v_ref, cw2, cb2) * sconv(x1_ref, cw1, cb1)
    _fft_stage1(z_scr, f1_ref, s_scr)
    m3i = m3i_ref[...].astype(BF16)

    def mid(k1, carry):
        m = _fft_mid_matrix(k1, a3_ref, b3_ref, twr_ref, twi_ref)
        cc = _dot(m, _fft_load_k1(s_scr, k1, two_n1))
        kf = kf_ref[0, k1]
        cr, ci = cc[:n2], cc[n2:]
        kr, ki = kf[:n2], kf[n2:]
        pp = jnp.concatenate([cr * kr - ci * ki, cr * ki + ci * kr], axis=0)
        q = _dot(m3i, pp)
        q_scr[pl.ds(k1, n2, stride=two_n1), :] = q[:n2]
        q_scr[pl.ds(n1 + k1, n2, stride=two_n1), :] = q[n2:]
        return carry

    lax.fori_loop(0, n1, mid, 0, unroll=FFT_UNROLL)

    def last(m2, carry):
        tr = t2r_ref[pl.ds(m2, 1), :]
        ti = t2i_ref[pl.ds(m2, 1), :]
        er = cos4_ref[...] * tr - sin4_ref[...] * ti
        ei = cos4_ref[...] * ti + sin4_ref[...] * tr
        f4 = jnp.concatenate([er, -ei], axis=1)
        qm = q_scr[pl.ds(pl.multiple_of(m2 * two_n1, two_n1), two_n1), :]
        y_scr[pl.ds(m2, nh, stride=n2), :] = _dot(f4, qm)
        return carry

    lax.fori_loop(0, n2, last, 0, unroll=FFT_UNROLL)
    z = z_scr[...]
    o_ref[0] = (y_scr[...] + z * bias_ref[...]) * sconv(x0_ref, cw0, cb0)


def _hyena_conv(p, conv_w, conv_b, hy_bias, kf, tabs, *, hy_col0, hy_dim, ct):
    bsz, L, _ = p.shape
    n_tiles = hy_dim // ct
    blk0 = hy_col0 // ct
    sec = hy_dim // ct
    xspec = lambda s: pl.BlockSpec((1, L, ct), lambda t, b: (b, 0, blk0 + s * sec + t))
    wspec = lambda s: pl.BlockSpec((3, ct), lambda t, b: (0, s * sec + t))
    bspec = lambda s: pl.BlockSpec((1, ct), lambda t, b: (0, s * sec + t))
    tab_list = [tabs[k] for k in ("f1", "a3", "b3", "twr", "twi", "m3i", "cos4", "sin4", "t2r", "t2i")]
    full = lambda a: pl.BlockSpec(a.shape, lambda t, b: (0,) * a.ndim)
    two_n1 = tabs["f1"].shape[0]
    return pl.pallas_call(
        _hyena_kernel,
        grid=(n_tiles, bsz),
        in_specs=[xspec(0), xspec(1), xspec(2), wspec(0), wspec(1), wspec(2), bspec(0), bspec(1), bspec(2),
                  pl.BlockSpec((1, ct), lambda t, b: (0, t)),
                  pl.BlockSpec((1,) + kf.shape[1:], lambda t, b: (t, 0, 0, 0), pipeline_mode=pl.Buffered(1))]
                 + [full(a) for a in tab_list],
        out_specs=pl.BlockSpec((1, L, ct), lambda t, b: (b, 0, t)),
        out_shape=jax.ShapeDtypeStruct((bsz, L, hy_dim), F32),
        scratch_shapes=[pltpu.VMEM((L, ct), F32), pltpu.VMEM((L, ct), F32),
                        pltpu.VMEM((FFT_N2 * two_n1, ct), F32), pltpu.VMEM((FFT_N2 * two_n1, ct), F32)],
        compiler_params=_cparams(("arbitrary", "arbitrary")),
        name="hyena_conv",
    )(p, p, p, conv_w, conv_w, conv_w, conv_b.reshape(1, -1), conv_b.reshape(1, -1), conv_b.reshape(1, -1),
      hy_bias.reshape(1, -1), kf, *tab_list)


def _filter_kernel(z_ref, t_ref, w1_ref, b1_ref, w2_ref, b2_ref, w3f_ref, w3b_ref, freq_ref, delta_ref,
                   f1_ref, a3_ref, b3_ref, twr_ref, twi_ref, kf_ref, h_scr, hf_scr, hb_scr, sf_scr, sb_scr):
    n2 = FFT_N2
    two_n1 = f1_ref.shape[0]
    n1 = two_n1 // 2

    @pl.when(pl.program_id(0) == 0)
    def _():
        freq = freq_ref[...]
        h1 = jnp.sin(freq * (_dot3(z_ref[...], w1_ref[...]) + b1_ref[...]))
        h_scr[...] = jnp.sin(freq * (_dot3(h1, w2_ref[...]) + b2_ref[...]))

    h = h_scr[...]
    window = jnp.exp(-t_ref[...] * delta_ref[...])
    hf_scr[...] = _dot3(h, w3f_ref[...]) * window
    hb = _dot3(h, w3b_ref[...]) * window
    rows = lax.broadcasted_iota(jnp.int32, hb.shape, 0)
    hb_scr[...] = jnp.where(rows == 0, 0.0, hb)
    _fft_stage1(hf_scr, f1_ref, sf_scr)
    _fft_stage1(hb_scr, f1_ref, sb_scr)

    def mid(k1, carry):
        m = _fft_mid_matrix(k1, a3_ref, b3_ref, twr_ref, twi_ref)
        cf = _dot(m, _fft_load_k1(sf_scr, k1, two_n1))
        cb = _dot(m, _fft_load_k1(sb_scr, k1, two_n1))
        kf_ref[0, k1] = jnp.concatenate([cf[:n2] + cb[:n2], cf[n2:] - cb[n2:]], axis=0)
        return carry

    lax.fori_loop(0, n1, mid, 0, unroll=FFT_UNROLL)


def _filter_spectrum(zpos, tcol, w1, b1, w2, b2, w3, freq, deltas, tabs, *, ct):
    L = zpos.shape[0]
    hy_dim = deltas.shape[1]
    n_tiles = hy_dim // ct
    two_n1 = tabs["f1"].shape[0]
    n1 = two_n1 // 2
    tab_list = [tabs[k] for k in ("f1", "a3", "b3", "twr", "twi")]
    full = lambda a: pl.BlockSpec(a.shape, lambda t: (0,) * a.ndim)
    row = lambda a: a.reshape(1, -1)
    small = [zpos, tcol, w1, row(b1), w2, row(b2)]
    return pl.pallas_call(
        _filter_kernel,
        grid=(n_tiles,),
        in_specs=[full(a) for a in small]
                 + [pl.BlockSpec((w3.shape[0], ct), lambda t: (0, t)),
                    pl.BlockSpec((w3.shape[0], ct), lambda t: (0, n_tiles + t)),
                    full(row(freq)),
                    pl.BlockSpec((1, ct), lambda t: (0, t))]
                 + [full(a) for a in tab_list],
        out_specs=pl.BlockSpec((1, n1, 2 * FFT_N2, ct), lambda t: (t, 0, 0, 0)),
        out_shape=jax.ShapeDtypeStruct((n_tiles, n1, 2 * FFT_N2, ct), F32),
        scratch_shapes=[pltpu.VMEM((L, w2.shape[1]), F32), pltpu.VMEM((L, ct), F32), pltpu.VMEM((L, ct), F32),
                        pltpu.VMEM((FFT_N2 * two_n1, ct), F32), pltpu.VMEM((FFT_N2 * two_n1, ct), F32)],
        compiler_params=_cparams(("arbitrary",)),
        name="hyena_filter",
    )(*small, w3, w3, row(freq), deltas, *tab_list)


def _outproj_kernel(of_ref, ob_ref, gate_ref, ohy_ref, x_ref, er_ref, ec_ref, g1_ref, w_ref, lng_ref, lnb_ref,
                    o_ref):
    rw = of_ref.shape[3]
    a_rw = ((of_ref[0, 0] + ob_ref[0, 0]) * gate_ref[0]).astype(BF16)
    mix = jnp.dot(a_rw, w_ref[0:rw, :], preferred_element_type=F32)
    mix = mix + jnp.dot(ohy_ref[0].astype(BF16), w_ref[rw:, :], preferred_element_type=F32)
    gmix = g1_ref[0] * mix
    tm = x_ref.shape[1]
    for q in range(tm // GRID_W):
        rows = slice(q * GRID_W, (q + 1) * GRID_W)
        xv = _add_pos_rows(x_ref[0, rows, :], er_ref, ec_ref, q)
        o_ref[0, rows, :] = _layer_norm(ALPHA * xv + gmix[rows], lng_ref[...], lnb_ref[...])


def _outproj(o_rw, gate, o_hy, x, er, ec, mod3, w_out, ln_g, ln_b, *, tm):
    bsz, L, d = x.shape
    rw = gate.shape[2]
    hy = o_hy.shape[2]
    gr = tm // GRID_W
    return pl.pallas_call(
        _outproj_kernel,
        grid=(bsz, L // tm),
        in_specs=[pl.BlockSpec((1, 1, tm, rw), lambda b, i: (0, b, i, 0)),
                  pl.BlockSpec((1, 1, tm, rw), lambda b, i: (1, b, i, 0)),
                  pl.BlockSpec((1, tm, rw), lambda b, i: (b, i, 0)),
                  pl.BlockSpec((1, tm, hy), lambda b, i: (b, i, 0)),
                  pl.BlockSpec((1, tm, d), lambda b, i: (b, i, 0)),
                  pl.BlockSpec((gr, d // 2), lambda b, i: (i, 0)),
                  pl.BlockSpec((GRID_W, d // 2), lambda b, i: (0, 0)),
                  pl.BlockSpec((1, 1, d), lambda b, i: (b, 0, 2)),
                  pl.BlockSpec((d, d), lambda b, i: (0, 0)),
                  pl.BlockSpec((1, d), lambda b, i: (0, 0)),
                  pl.BlockSpec((1, d), lambda b, i: (0, 0))],
        out_specs=pl.BlockSpec((1, tm, d), lambda b, i: (b, i, 0)),
        out_shape=jax.ShapeDtypeStruct((bsz, L, d), F32),
        compiler_params=_cparams(("arbitrary", "arbitrary")),
        name="outproj_ln",
    )(o_rw, o_rw, gate, o_hy, x, er, ec, mod3, w_out, ln_g.reshape(1, d), ln_b.reshape(1, d))


def _ffn_kernel(x_ref, sh_ref, sc_ref, g2_ref, w1_ref, w3_ref, w2_ref, lng_ref, lnb_ref, o_ref, a_scr, acc_scr):
    f = pl.program_id(2)

    @pl.when(f == 0)
    def _():
        a_scr[...] = (x_ref[0] * (1.0 + sc_ref[0]) + sh_ref[0]).astype(BF16)
        acc_scr[...] = jnp.zeros_like(acc_scr)

    a = a_scr[...]
    h1 = jnp.dot(a, w1_ref[...], preferred_element_type=F32)
    h3 = jnp.dot(a, w3_ref[...], preferred_element_type=F32)
    h = (h1 * _sigmoid(h1) * h3).astype(BF16)
    acc_scr[...] += jnp.dot(h, w2_ref[...], preferred_element_type=F32)

    @pl.when(f == pl.num_programs(2) - 1)
    def _():
        o_ref[0] = _layer_norm(ALPHA * x_ref[0] + g2_ref[0] * acc_scr[...], lng_ref[...], lnb_ref[...])


def _ffn(x1, mod3, w1, w3, w2, ln_g, ln_b, *, tm, tf):
    bsz, L, d = x1.shape
    dff = w1.shape[1]
    return pl.pallas_call(
        _ffn_kernel,
        grid=(bsz, L // tm, dff // tf),
        in_specs=[pl.BlockSpec((1, tm, d), lambda b, i, f: (b, i, 0)),
                  pl.BlockSpec((1, 1, d), lambda b, i, f: (b, 0, 3)),
                  pl.BlockSpec((1, 1, d), lambda b, i, f: (b, 0, 4)),
                  pl.BlockSpec((1, 1, d), lambda b, i, f: (b, 0, 5)),
                  pl.BlockSpec((d, tf), lambda b, i, f: (0, f)),
                  pl.BlockSpec((d, tf), lambda b, i, f: (0, f)),
                  pl.BlockSpec((tf, d), lambda b, i, f: (f, 0)),
                  pl.BlockSpec((1, d), lambda b, i, f: (0, 0)),
                  pl.BlockSpec((1, d), lambda b, i, f: (0, 0))],
        out_specs=pl.BlockSpec((1, tm, d), lambda b, i, f: (b, i, 0)),
        out_shape=jax.ShapeDtypeStruct((bsz, L, d), F32),
        scratch_shapes=[pltpu.VMEM((tm, d), BF16), pltpu.VMEM((tm, d), F32)],
        compiler_params=_cparams(("arbitrary", "arbitrary", "arbitrary")),
        name="ffn_ln",
    )(x1, mod3, mod3, mod3, w1, w3, w2, ln_g.reshape(1, d), ln_b.reshape(1, d))


def _pos_tables(rows, d):
    quarter = d // 4
    omega = 1.0 / (10000.0 ** (jnp.arange(quarter, dtype=F32) / quarter))
    er = jnp.arange(rows, dtype=F32)[:, None] * omega
    ec = jnp.arange(GRID_W, dtype=F32)[:, None] * omega
    er = jnp.concatenate([jnp.sin(er), jnp.cos(er)], -1)
    ec = jnp.concatenate([jnp.sin(ec), jnp.cos(ec)], -1)
    return er, ec


def _filter_positions(L):
    pos = jnp.arange(L, dtype=F32)[:, None]
    t = jnp.linspace(0.0, 1.0, L, dtype=F32)[:, None]
    bands = jnp.linspace(1e-4, N_BANDS - 1, N_BANDS, dtype=F32)[None, :]
    ang = 2.0 * math.pi * bands * pos / L
    z = jnp.concatenate([t, jnp.cos(ang), -jnp.sin(ang)], -1)
    return z, t


def _pad_cols(a, width):
    return jnp.pad(a, ((0, 0), (0, width - a.shape[1])))


def _pad_rows(a, height):
    return jnp.pad(a, ((0, height - a.shape[0]), (0, 0)))


def kernel(x, c, ctx, c_ctx, w_ada, b_ada, w_in, conv_rw, conv_hy, conv_hy_b, w0_f, w_up_f, a0_f, a_up_f, w0_b, w_up_b, a0_b, a_up_b, k_k, k_a, r_k, g_up, gn_g, gn_b, filt_w1, filt_b1, filt_w2, filt_b2, filt_w3, sin_freq, hy_bias, w_out, ln1_g, ln1_b, ffn_w1, ffn_w3, ffn_w2, ln2_g, ln2_b):
    bsz, L, d = x.shape
    rw = k_k.shape[1]
    hy = hy_bias.shape[1]
    dl, il, gl = w_up_f.shape[1], a_up_f.shape[1], g_up.shape[1]
    lo_w = 4 * LANE
    rkv_cols = 3 * rw

    wi = w_in[0]
    lo0 = rkv_cols

    def lora_layout(a):
        return jnp.concatenate([_pad_cols(a[:, lo0:lo0 + dl], LANE),
                                _pad_cols(a[:, lo0 + dl:lo0 + dl + il], LANE),
                                _pad_cols(a[:, lo0 + dl + il:lo0 + dl + il + gl], 2 * LANE)], axis=1)

    hy0 = lo0 + dl + il + gl
    w_lat = jnp.concatenate([wi[:, :rkv_cols], wi[:, hy0:], lora_layout(wi)], axis=1).astype(BF16)
    w_ctx = jnp.concatenate([wi[:, :rkv_cols], lora_layout(wi)], axis=1).astype(BF16)
    cw_rkv = conv_rw[0][:, :rkv_cols]
    cw_lo = lora_layout(conv_rw[0])
    wup = _pad_rows(jnp.concatenate([w_up_f[0], w_up_b[0]], axis=1), LANE)
    aup = _pad_rows(jnp.concatenate([a_up_f[0], a_up_b[0]], axis=1), LANE)
    gup = _pad_rows(g_up[0], 2 * LANE)
    w0 = jnp.concatenate([w0_f[0], w0_b[0]])
    a0 = jnp.concatenate([a0_f[0], a0_b[0]])

    cc = _pad_rows(jnp.concatenate([c, c_ctx[None, :]], axis=0), SUB)
    mod = _adaln(cc, w_ada[0], b_ada[0])
    mod3 = mod.reshape(SUB, 1, 6 * d)

    er, ec = _pos_tables(L // GRID_W, d)

    p_lat = _inproj(x, er, ec, mod3, w_lat, tm=1024, tn=512, add_pos=True, ctx_row=None)
    p_ctx = _inproj(ctx, er, ec, mod3, w_ctx, tm=ctx.shape[1], tn=512, add_pos=False, ctx_row=bsz)

    lo_cols = (rkv_cols + 3 * hy, rkv_cols + 3 * hy + lo_w)
    r, v, kk, gate, lw, kh, bvec = _rwkv_prep(p_lat, p_ctx, cw_rkv, cw_lo, wup, aup, gup, w0, a0,
                                              k_k[0], k_a[0], rw=rw, rkv_cols=rkv_cols, lo_cols=lo_cols)
    o_rw = _rwkv_scan(r, v, kk, lw, kh, bvec, r_k[0], gn_g[0], gn_b[0], n_lat_tokens=L)

    tabs = _fft_tables(L)
    zpos, tcol = _filter_positions(L)
    max_decay = math.log(DECAY_TARGET) / FAST_DECAY_PCT
    min_decay = math.log(DECAY_TARGET) / SLOW_DECAY_PCT
    deltas = jnp.abs(jnp.linspace(min_decay, max_decay, hy, dtype=F32))[None, :]
    kf = _filter_spectrum(_pad_cols(zpos, LANE), tcol, _pad_rows(filt_w1[0], LANE), filt_b1[0], filt_w2[0],
                          filt_b2[0], filt_w3[0], sin_freq[0], deltas, tabs, ct=LANE)
    o_hy = _hyena_conv(p_lat, conv_hy[0], conv_hy_b[0], hy_bias[0], kf, tabs,
                       hy_col0=rkv_cols, hy_dim=hy, ct=LANE)

    x1 = _outproj(o_rw, gate, o_hy, x, er, ec, mod3, w_out[0].astype(BF16), ln1_g[0], ln1_b[0], tm=512)
    return _ffn(x1, mod3, ffn_w1[0].astype(BF16), ffn_w3[0].astype(BF16), ffn_w2[0].astype(BF16),
                ln2_g[0], ln2_b[0], tm=512, tf=512)
```

```python
import functools
import math

import numpy as np
import jax
import jax.numpy as jnp
from jax import lax
from jax.experimental import pallas as pl
from jax.experimental.pallas import tpu as pltpu

F32 = jnp.float32
BF16 = jnp.bfloat16

GRID_W = 64
HEAD_DIM = 64
N_BANDS = 16
DECAY_TARGET = 1e-2
FAST_DECAY_PCT = 0.3
SLOW_DECAY_PCT = 1.5
LN_EPS = 1e-5
GN_EPS = 64e-5
ALPHA = 2.0 ** 0.25

LANE = 128
SUB = 8
VMEM_LIMIT = 56 * 1024 * 1024

SCAN_CHUNK = 64
SCAN_GROUP = 4
FFT_N2 = 64
FFT_UNROLL = 4


def _cparams(sem):
    return pltpu.CompilerParams(dimension_semantics=sem, vmem_limit_bytes=VMEM_LIMIT)


def _dot(a, b, dims=(((1,), (0,)), ((), ()))):
    return lax.dot_general(a.astype(BF16), b.astype(BF16), dims, preferred_element_type=F32)


def _dot_nt(a, b):
    return _dot(a, b, (((1,), (1,)), ((), ())))


def _dot_tn(a, b):
    return _dot(a, b, (((0,), (0,)), ((), ())))


def _split2(a):
    hi = a.astype(BF16)
    lo = (a - hi.astype(F32)).astype(BF16)
    return hi, lo


def _dot3(a, b):
    ah, al = _split2(a)
    bh, bl = _split2(b)
    return _dot(ah, bh) + (_dot(al, bh) + _dot(ah, bl))


def _dot_exact_rhs(a, b_exact):
    a1 = a.astype(BF16)
    r1 = a - a1.astype(F32)
    a2 = r1.astype(BF16)
    a3 = (r1 - a2.astype(F32)).astype(BF16)
    return _dot(a1, b_exact) + (_dot(a2, b_exact) + _dot(a3, b_exact))


def _dot_exact_lhs(a_exact, b):
    b1 = b.astype(BF16)
    r1 = b - b1.astype(F32)
    b2 = r1.astype(BF16)
    b3 = (r1 - b2.astype(F32)).astype(BF16)
    return _dot(a_exact, b1) + (_dot(a_exact, b2) + _dot(a_exact, b3))


def _sigmoid(x):
    return 1.0 / (1.0 + jnp.exp(-x))


def _layer_norm(h, g, b):
    mu = jnp.mean(h, -1, keepdims=True)
    hc = h - mu
    var = jnp.mean(hc * hc, -1, keepdims=True)
    return hc * lax.rsqrt(var + LN_EPS) * g + b


def _conv3_rows(cur, prev_row, next_row, w):
    n = cur.shape[0]
    rows = lax.broadcasted_iota(jnp.int32, cur.shape, 0)
    up = jnp.where(rows == 0, prev_row, pltpu.roll(cur, 1, 0))
    dn = jnp.where(rows == n - 1, next_row, pltpu.roll(cur, n - 1, 0))
    return up * w[0:1] + cur * w[1:2] + dn * w[2:3]


def _adaln_kernel(c_ref, w_ref, b_ref, o_ref):
    cv = c_ref[...]
    a = cv * _sigmoid(cv)
    o_ref[...] = _dot3(a, w_ref[...]) + b_ref[...]


def _adaln(cc, w, b, tn=1024):
    m, d = cc.shape
    n = w.shape[1]
    return pl.pallas_call(
        _adaln_kernel,
        grid=(n // tn,),
        in_specs=[pl.BlockSpec((m, d), lambda j: (0, 0)),
                  pl.BlockSpec((d, tn), lambda j: (0, j)),
                  pl.BlockSpec((1, tn), lambda j: (0, j))],
        out_specs=pl.BlockSpec((m, tn), lambda j: (0, j)),
        out_shape=jax.ShapeDtypeStruct((m, n), F32),
        compiler_params=_cparams(("arbitrary",)),
        name="adaln",
    )(cc, w, b.reshape(1, n))


def _add_pos_rows(xv, er_ref, ec_ref, q):
    half = ec_ref.shape[1]
    return jnp.concatenate([xv[:, :half] + er_ref[q:q + 1, :], xv[:, half:] + ec_ref[...]], axis=1)


def _inproj_kernel(x_ref, er_ref, ec_ref, sh_ref, sc_ref, w_ref, o_ref, a_scr, *, add_pos):
    @pl.when(pl.program_id(2) == 0)
    def _():
        scale = 1.0 + sc_ref[0]
        shift = sh_ref[0]
        tm = x_ref.shape[1]
        if add_pos:
            for q in range(tm // GRID_W):
                rows = slice(q * GRID_W, (q + 1) * GRID_W)
                xv = _add_pos_rows(x_ref[0, rows, :], er_ref, ec_ref, q)
                a_scr[rows, :] = (xv * scale + shift).astype(BF16)
        else:
            a_scr[...] = (x_ref[0] * scale + shift).astype(BF16)

    o_ref[0] = jnp.dot(a_scr[...], w_ref[...], preferred_element_type=F32)


def _inproj(x, er, ec, mod3, w, *, tm, tn, add_pos, ctx_row):
    bsz, lx, d = x.shape
    n = w.shape[1]
    gr = tm // GRID_W
    if add_pos:
        er_spec = pl.BlockSpec((gr, d // 2), lambda b, i, j: (i, 0))
    else:
        er_spec = pl.BlockSpec((SUB, d // 2), lambda b, i, j: (0, 0))
    row = (lambda b: ctx_row) if ctx_row is not None else (lambda b: b)
    return pl.pallas_call(
        functools.partial(_inproj_kernel, add_pos=add_pos),
        grid=(bsz, lx // tm, n // tn),
        in_specs=[pl.BlockSpec((1, tm, d), lambda b, i, j: (b, i, 0)),
                  er_spec,
                  pl.BlockSpec((GRID_W, d // 2), lambda b, i, j: (0, 0)),
                  pl.BlockSpec((1, 1, d), lambda b, i, j: (row(b), 0, 0)),
                  pl.BlockSpec((1, 1, d), lambda b, i, j: (row(b), 0, 1)),
                  pl.BlockSpec((d, tn), lambda b, i, j: (0, j))],
        out_specs=pl.BlockSpec((1, tm, tn), lambda b, i, j: (b, i, j)),
        out_shape=jax.ShapeDtypeStruct((bsz, lx, n), F32),
        scratch_shapes=[pltpu.VMEM((tm, d), BF16)],
        compiler_params=_cparams(("arbitrary", "arbitrary", "arbitrary")),
        name="inproj_pos" if add_pos else "inproj_ctx",
    )(x, er, ec, mod3, mod3, w)


def _head_sum(x, ones_bd):
    outs = []
    for j in range(x.shape[1] // LANE):
        outs.append(_dot_exact_rhs(x[:, j * LANE:(j + 1) * LANE], ones_bd))
    return jnp.concatenate(outs, axis=1)


def _prep_kernel(p_rkv, p_rkv_prev, p_rkv_next, p_lo, p_lo_prev, p_lo_next, pc_rkv, pc_lo,
                 cw_rkv, cw_lo, wup, aup, gup, w0, a0, kk_w, ka_w, ones_ref,
                 r_out, v_out, kk_out, gate_out, lw_out, kh_out, b_out, *, n_lat):
    i = pl.program_id(1)
    is_ctx = i == n_lat
    no_prev = jnp.logical_or(is_ctx, i == 0)
    no_next = jnp.logical_or(is_ctx, i == n_lat - 1)
    rw = r_out.shape[2]

    def conv(main_ref, ctx_ref, prev_ref, next_ref, cw_ref, cols):
        cur = jnp.where(is_ctx, ctx_ref[0, :, cols], main_ref[0, :, cols])
        prev_row = jnp.where(no_prev, 0.0, prev_ref[0, SUB - 1:SUB, cols])
        next_row = jnp.where(no_next, 0.0, next_ref[0, 0:1, cols])
        return _conv3_rows(cur, prev_row, next_row, cw_ref[:, cols])

    def rkv(s):
        return conv(p_rkv, pc_rkv, p_rkv_prev, p_rkv_next, cw_rkv, slice(s * rw, (s + 1) * rw))

    ones_bd = ones_ref[...]
    k = rkv(1)
    kkr = k * kk_w[...]
    nrm = jnp.sqrt(_head_sum(kkr * kkr, ones_bd))
    kk = kkr / jnp.maximum(nrm, 1e-12)
    kk_out[0] = kk.astype(kk_out.dtype)
    r_out[0] = rkv(0).astype(r_out.dtype)
    v_out[0] = rkv(2).astype(v_out.dtype)

    lo = conv(p_lo, pc_lo, p_lo_prev, p_lo_next, cw_lo, slice(0, p_lo.shape[2]))
    tw = jnp.tanh(lo[:, 0:LANE])
    al = lo[:, LANE:2 * LANE]
    gl = _sigmoid(lo[:, 2 * LANE:])
    gate_out[0] = _dot(gl, gup[...]).astype(gate_out.dtype)
    wx = _dot3(tw, wup[...]) + w0[...]
    ax = _dot3(al, aup[...]) + a0[...]
    for d in range(2):
        cols = slice(d * rw, (d + 1) * rw)
        lw_out[d, 0] = -math.exp(-0.5) * _sigmoid(wx[:, cols])
        ia = _sigmoid(ax[:, cols])
        kh_out[d, 0] = (k * (1.0 + (ia - 1.0) * ka_w[...])).astype(kh_out.dtype)
        b_out[d, 0] = (kk * ia).astype(b_out.dtype)


def _rwkv_prep(p, pc, cw_rkv, cw_lo, wup, aup, gup, w0, a0, k_k, k_a, *, rw, rkv_cols, lo_cols):
    bsz, lq, _ = p.shape
    ctx = pc.shape[1]
    tr = ctx
    n_lat = lq // tr
    t_all = lq + ctx
    hb = tr // SUB
    lo_w = lo_cols[1] - lo_cols[0]
    lo_blk = lo_cols[0] // lo_w
    lo_blk_c = rkv_cols // lo_w
    lat = lambda i: jnp.minimum(i, n_lat - 1)
    prev_blk = lambda i: jnp.clip(i * hb - 1, 0, lq // SUB - 1)
    next_blk = lambda i: jnp.clip((i + 1) * hb, 0, lq // SUB - 1)
    ones_bd = (np.arange(LANE)[:, None] // HEAD_DIM == np.arange(LANE)[None, :] // HEAD_DIM)
    ones_bd = jnp.asarray(ones_bd, BF16)
    full = lambda a: pl.BlockSpec(a.shape, lambda b, i: (0,) * a.ndim)
    row = lambda a: a.reshape(1, -1)
    weights = [cw_rkv, cw_lo, wup, aup, gup, row(w0), row(a0), row(k_k), row(k_a), ones_bd]
    shared = jax.ShapeDtypeStruct((bsz, t_all, rw), BF16)
    perdir = jax.ShapeDtypeStruct((2, bsz, t_all, rw), BF16)
    perdir_f32 = jax.ShapeDtypeStruct((2, bsz, t_all, rw), F32)
    o_shared = pl.BlockSpec((1, tr, rw), lambda b, i: (b, i, 0))
    o_dir = pl.BlockSpec((2, 1, tr, rw), lambda b, i: (0, b, i, 0))
    return pl.pallas_call(
        functools.partial(_prep_kernel, n_lat=n_lat),
        grid=(bsz, n_lat + 1),
        in_specs=[pl.BlockSpec((1, tr, rkv_cols), lambda b, i: (b, lat(i), 0)),
                  pl.BlockSpec((1, SUB, rkv_cols), lambda b, i: (b, prev_blk(i), 0)),
                  pl.BlockSpec((1, SUB, rkv_cols), lambda b, i: (b, next_blk(i), 0)),
                  pl.BlockSpec((1, tr, lo_w), lambda b, i: (b, lat(i), lo_blk)),
                  pl.BlockSpec((1, SUB, lo_w), lambda b, i: (b, prev_blk(i), lo_blk)),
                  pl.BlockSpec((1, SUB, lo_w), lambda b, i: (b, next_blk(i), lo_blk)),
                  pl.BlockSpec((1, tr, rkv_cols), lambda b, i: (b, 0, 0)),
                  pl.BlockSpec((1, tr, lo_w), lambda b, i: (b, 0, lo_blk_c))]
                 + [full(a) for a in weights],
        out_specs=[o_shared, o_shared, o_shared, o_shared, o_dir, o_dir, o_dir],
        out_shape=[shared, shared, shared, shared, perdir_f32, perdir, perdir],
        compiler_params=_cparams(("arbitrary", "arbitrary")),
        name="rwkv_prep",
    )(p, p, p, p, p, p, pc, pc, *weights)


def _scan_kernel(rf_ref, vf_ref, kkf_ref, lwf_ref, khf_ref, bf_ref, rb_ref, vb_ref, kkb_ref, lwb_ref, khb_ref,
                 bb_ref, rk_ref, gng_ref, gnb_ref, ones_ref, of_ref, ob_ref, s_scr):
    C = rf_ref.shape[1]
    G = SCAN_GROUP
    W = G * HEAD_DIM
    n = G * C
    n_groups = rf_ref.shape[2] // W

    @pl.when(pl.program_id(1) == 0)
    def _():
        s_scr[...] = jnp.zeros_like(s_scr)

    bi = lax.broadcasted_iota(jnp.int32, (n, n), 0)
    bj = lax.broadcasted_iota(jnp.int32, (n, n), 1)
    same = (bi // C) == (bj // C)
    li = bi % C
    lj = bj % C
    eye = jnp.where(bi == bj, 1.0, 0.0)
    hrow = lax.broadcasted_iota(jnp.int32, (n, W), 0) // C
    hlane = lax.broadcasted_iota(jnp.int32, (n, W), 1) // HEAD_DIM
    hmask = hrow == hlane
    ti = lax.broadcasted_iota(jnp.int32, (C, C), 0)
    tj = lax.broadcasted_iota(jnp.int32, (C, C), 1)

    def rep(x):
        return jnp.where(hmask, jnp.concatenate([x] * G, axis=0), 0.0).astype(BF16)

    chains = []
    dirs = ((rf_ref, vf_ref, kkf_ref, lwf_ref, khf_ref, bf_ref, False),
            (rb_ref, vb_ref, kkb_ref, lwb_ref, khb_ref, bb_ref, True))
    per_dir = []
    for di, (r_ref, v_ref, kk_ref, lw_ref, kh_ref, b_ref, rev) in enumerate(dirs):
        r = r_ref[0].astype(F32)
        v = v_ref[0].astype(F32)
        kk = kk_ref[0].astype(F32)
        lw = lw_ref[0, 0]
        kh = kh_ref[0, 0].astype(F32)
        bv = b_ref[0, 0].astype(F32)
        before = (lj > li) if rev else (lj < li)
        strict = jnp.logical_and(same, before)
        incl = jnp.logical_and(same, jnp.logical_or(before, li == lj))
        tri = jnp.where((tj >= ti) if rev else (tj <= ti), 1.0, 0.0).astype(BF16)
        cum = _dot_exact_lhs(tri, lw)
        tot = cum[0:1, :] if rev else cum[C - 1:C, :]
        e_neg = jnp.exp(-cum)
        e_end = jnp.exp(tot - cum)
        rt = r * jnp.exp(cum)
        at = -kk * jnp.exp(cum - lw)
        bt = bv * e_neg
        kt = kh * e_neg
        bp = bv * e_end
        kp = kh * e_end
        per_dir.append((r, v, kh, jnp.exp(tot)))
        for g in range(n_groups):
            cols = slice(g * W, (g + 1) * W)
            chains.append(dict(
                di=di, g=g, cols=cols, strict=strict, incl=incl,
                v2=rep(v[:, cols]),
                ar=jnp.concatenate([rep(at[:, cols]), rep(rt[:, cols])], axis=0),
                bk=jnp.concatenate([rep(bt[:, cols]), rep(kt[:, cols])], axis=0),
                bkp=jnp.concatenate([rep(bp[:, cols]), rep(kp[:, cols])], axis=0)))

    for ch in chains:
        ch["tt"] = _dot_nt(ch["ar"], ch["bk"])
    for ch in chains:
        ch["p"] = jnp.where(ch["strict"], ch["tt"][:n, :n], 0.0).astype(BF16)
        ch["tinv"] = eye + ch["p"].astype(F32)
    step = 2
    while step < C:
        for ch in chains:
            ch["p"] = _dot(ch["p"], ch["p"]).astype(BF16)
        for ch in chains:
            ch["tinv"] = ch["tinv"] + _dot(ch["p"], ch["tinv"])
        step *= 2
    for ch in chains:
        ch["s0"] = s_scr[ch["di"], ch["g"]]
        ch["xs"] = _dot_nt(ch["ar"], ch["s0"])
    for ch in chains:
        a_ak = jnp.where(ch["strict"], ch["tt"][:n, n:], 0.0)
        ch["rhs"] = ch["xs"][:n] + _dot(a_ak, ch["v2"])
    for ch in chains:
        u = _dot(ch["tinv"], ch["rhs"])
        ch["uv"] = jnp.concatenate([u.astype(BF16), ch["v2"]], axis=0)
    for ch in chains:
        a_r = jnp.concatenate([jnp.where(ch["incl"], ch["tt"][n:, :n], 0.0),
                               jnp.where(ch["incl"], ch["tt"][n:, n:], 0.0)], axis=1)
        ch["y2"] = ch["xs"][n:] + _dot(a_r, ch["uv"])
    for ch in chains:
        e_tot = per_dir[ch["di"]][3]
        s_scr[ch["di"], ch["g"]] = ch["s0"] * e_tot[:, ch["cols"]] + _dot_tn(ch["uv"], ch["bkp"])

    ones_bd = ones_ref[...]
    inv = 1.0 / HEAD_DIM
    for di, o_ref in enumerate((of_ref, ob_ref)):
        r, v, kh, _ = per_dir[di]
        ys = []
        for ch in chains:
            if ch["di"] == di:
                y = ch["y2"][0:C]
                for h in range(1, G):
                    y = y + ch["y2"][h * C:(h + 1) * C]
                ys.append(y)
        y = jnp.concatenate(ys, axis=1)
        mu = _head_sum(y, ones_bd) * inv
        yc = y - mu
        var = _head_sum(yc * yc, ones_bd) * inv
        gn = yc * lax.rsqrt(var + GN_EPS) * gng_ref[...] + gnb_ref[...]
        bonus = _head_sum(r * kh * rk_ref[...], ones_bd) * v
        o_ref[0] = gn + bonus


def _rwkv_scan(r, v, kk, lw, kh, bvec, r_k, gn_g, gn_b, *, n_lat_tokens):
    bsz, t_all, rw = r.shape
    C = SCAN_CHUNK
    n_chunks = t_all // C
    n_ctx = (t_all - n_lat_tokens) // C
    n_lat = n_lat_tokens // C
    W = SCAN_GROUP * HEAD_DIM

    blk_f = lambda c: jnp.where(c < n_ctx, n_lat + c, c - n_ctx)
    blk_b = lambda c: n_chunks - 1 - c
    ones_bd = (np.arange(LANE)[:, None] // HEAD_DIM == np.arange(LANE)[None, :] // HEAD_DIM)
    ones_bd = jnp.asarray(ones_bd, BF16)
    shared = lambda blk: pl.BlockSpec((1, C, rw), lambda b, c: (b, blk(c), 0))
    perdir = lambda d, blk: pl.BlockSpec((1, 1, C, rw), lambda b, c: (d, b, blk(c), 0))
    vec = pl.BlockSpec((1, rw), lambda b, c: (0, 0))
    out = jax.ShapeDtypeStruct((bsz, t_all, rw), F32)
    return pl.pallas_call(
        _scan_kernel,
        grid=(bsz, n_chunks),
        in_specs=[shared(blk_f), shared(blk_f), shared(blk_f), perdir(0, blk_f), perdir(0, blk_f), perdir(0, blk_f),
                  shared(blk_b), shared(blk_b), shared(blk_b), perdir(1, blk_b), perdir(1, blk_b), perdir(1, blk_b),
                  vec, vec, vec, pl.BlockSpec((LANE, LANE), lambda b, c: (0, 0))],
        out_specs=[shared(blk_f), shared(blk_b)],
        out_shape=[out, out],
        scratch_shapes=[pltpu.VMEM((2, rw // W, W, W), F32)],
        compiler_params=_cparams(("arbitrary", "arbitrary")),
        name="rwkv_scan",
    )(r, v, kk, lw, kh, bvec, r, v, kk, lw, kh, bvec,
      r_k.reshape(1, rw), gn_g.reshape(1, rw), gn_b.reshape(1, rw), ones_bd)


def _fft_tables(L):
    n2 = FFT_N2
    n_all = 2 * L
    n1 = n_all // n2
    nh = n1 // 2
    two_pi = 2.0 * np.pi
    kh = -(-(nh + 1) // 4) * 4
    k1 = np.arange(kh)
    pair_w = np.where(k1 > nh, 0.0, np.where((k1 == 0) | (k1 == nh), 1.0, 2.0))
    th1 = two_pi * np.outer(k1, np.arange(nh)) / n1
    f1 = np.concatenate([np.cos(th1), -np.sin(th1)], 0)
    th2 = two_pi * np.outer(np.arange(n2), np.arange(n2)) / n2
    fr, fi = np.cos(th2), -np.sin(th2)
    a3 = np.block([[fr, -fi], [fi, fr]])
    b3 = np.block([[-fi, -fr], [fr, -fi]])
    tht = two_pi * np.outer(k1, np.arange(n2)) / n_all
    twr = np.tile(np.cos(tht), (1, 2))
    twi = np.tile(-np.sin(tht), (1, 2))
    gr, gi = np.cos(th2), np.sin(th2)
    m3i = np.block([[gr, -gi], [gi, gr]]) / n_all
    th4 = two_pi * np.outer(np.arange(nh), k1) / n1
    cos4, sin4 = np.cos(th4) * pair_w, np.sin(th4) * pair_w
    ca = np.concatenate([cos4, -sin4], 1)
    cb = np.concatenate([-sin4, -cos4], 1)
    th5 = two_pi * np.outer(np.arange(n2), k1) / n_all
    t2r, t2i = np.tile(np.cos(th5), (1, 2)), np.tile(np.sin(th5), (1, 2))
    f = lambda a: jnp.asarray(a, F32)
    return dict(f1=f(f1), a3=f(a3), b3=f(b3), twr=f(twr), twi=f(twi),
                m3i=f(m3i), ca=f(ca), cb=f(cb), t2r=f(t2r), t2i=f(t2i))


def _fft_stage1(src_ref, f1_ref, s_ref):
    kh2, nh = f1_ref.shape
    n2 = FFT_N2
    f1 = f1_ref[...].astype(BF16)

    def body(nlo, carry):
        xin = src_ref[pl.ds(nlo, nh, stride=n2), :]
        s_ref[pl.ds(pl.multiple_of(nlo * kh2, SUB), kh2), :] = _dot(f1, xin)
        return carry

    lax.fori_loop(0, n2, body, 0, unroll=FFT_UNROLL)


def _fft_mid_matrix(k1, a3_ref, b3_ref, twr_ref, twi_ref):
    tr = twr_ref[pl.ds(k1, 1), :]
    ti = twi_ref[pl.ds(k1, 1), :]
    return (a3_ref[...] * tr + b3_ref[...] * ti).astype(BF16)


def _fft_load_k1(s_ref, k1, kh2):
    n2 = FFT_N2
    re = s_ref[pl.ds(k1, n2, stride=kh2), :]
    im = s_ref[pl.ds(kh2 // 2 + k1, n2, stride=kh2), :]
    return jnp.concatenate([re, im], axis=0)


def _hyena_kernel(x0_ref, x1_ref, v_ref, cw0, cw1, cw2, cb0, cb1, cb2, bias_ref, kf_ref,
                  f1_ref, a3_ref, b3_ref, twr_ref, twi_ref, m3i_ref, ca_ref, cb_ref, t2r_ref, t2i_ref,
                  o_ref, z_scr, y_scr, s_scr, q_scr):
    n2 = FFT_N2
    kh2, nh = f1_ref.shape
    kh = kh2 // 2
    zero = jnp.zeros((1, x0_ref.shape[2]), F32)

    def sconv(ref, cw, cb):
        return _conv3_rows(ref[0], zero, zero, cw[...]) + cb[...]

    z_scr[...] = sconv(v_ref, cw2, cb2) * sconv(x1_ref, cw1, cb1)
    _fft_stage1(z_scr, f1_ref, s_scr)
    m3i = m3i_ref[...].astype(BF16)

    def mid(k1, carry):
        m = _fft_mid_matrix(k1, a3_ref, b3_ref, twr_ref, twi_ref)
        cc = _dot(m, _fft_load_k1(s_scr, k1, kh2))
        kf = kf_ref[0, k1]
        cr, ci = cc[:n2], cc[n2:]
        kr, ki = kf[:n2], kf[n2:]
        pp = jnp.concatenate([cr * kr - ci * ki, cr * ki + ci * kr], axis=0)
        q = _dot(m3i, pp)
        q_scr[pl.ds(k1, n2, stride=kh2), :] = q[:n2]
        q_scr[pl.ds(kh + k1, n2, stride=kh2), :] = q[n2:]
        return carry

    lax.fori_loop(0, kh, mid, 0, unroll=FFT_UNROLL)

    def last(m2, carry):
        f4 = ca_ref[...] * t2r_ref[pl.ds(m2, 1), :] + cb_ref[...] * t2i_ref[pl.ds(m2, 1), :]
        qm = q_scr[pl.ds(pl.multiple_of(m2 * kh2, SUB), kh2), :]
        y_scr[pl.ds(m2, nh, stride=n2), :] = _dot(f4, qm)
        return carry

    lax.fori_loop(0, n2, last, 0, unroll=FFT_UNROLL)
    z = z_scr[...]
    o_ref[0] = (y_scr[...] + z * bias_ref[...]) * sconv(x0_ref, cw0, cb0)


def _hyena_conv(p, conv_w, conv_b, hy_bias, kf, tabs, *, hy_col0, hy_dim, ct):
    bsz, L, _ = p.shape
    n_tiles = hy_dim // ct
    blk0 = hy_col0 // ct
    sec = hy_dim // ct
    xspec = lambda s: pl.BlockSpec((1, L, ct), lambda t, b: (b, 0, blk0 + s * sec + t))
    wspec = lambda s: pl.BlockSpec((3, ct), lambda t, b: (0, s * sec + t))
    bspec = lambda s: pl.BlockSpec((1, ct), lambda t, b: (0, s * sec + t))
    tab_list = [tabs[k] for k in ("f1", "a3", "b3", "twr", "twi", "m3i", "ca", "cb", "t2r", "t2i")]
    full = lambda a: pl.BlockSpec(a.shape, lambda t, b: (0,) * a.ndim)
    kh2 = tabs["f1"].shape[0]
    return pl.pallas_call(
        _hyena_kernel,
        grid=(n_tiles, bsz),
        in_specs=[xspec(0), xspec(1), xspec(2), wspec(0), wspec(1), wspec(2), bspec(0), bspec(1), bspec(2),
                  pl.BlockSpec((1, ct), lambda t, b: (0, t)),
                  pl.BlockSpec((1,) + kf.shape[1:], lambda t, b: (t, 0, 0, 0), pipeline_mode=pl.Buffered(1))]
                 + [full(a) for a in tab_list],
        out_specs=pl.BlockSpec((1, L, ct), lambda t, b: (b, 0, t)),
        out_shape=jax.ShapeDtypeStruct((bsz, L, hy_dim), F32),
        scratch_shapes=[pltpu.VMEM((L, ct), F32), pltpu.VMEM((L, ct), F32),
                        pltpu.VMEM((FFT_N2 * kh2, ct), F32), pltpu.VMEM((FFT_N2 * kh2, ct), F32)],
        compiler_params=_cparams(("arbitrary", "arbitrary")),
        name="hyena_conv",
    )(p, p, p, conv_w, conv_w, conv_w, conv_b.reshape(1, -1), conv_b.reshape(1, -1), conv_b.reshape(1, -1),
      hy_bias.reshape(1, -1), kf, *tab_list)


def _filter_kernel(z_ref, t_ref, w1_ref, b1_ref, w2_ref, b2_ref, w3f_ref, w3b_ref, freq_ref, delta_ref,
                   f1_ref, a3_ref, b3_ref, twr_ref, twi_ref, kf_ref, h_scr, hf_scr, hb_scr, sf_scr, sb_scr):
    n2 = FFT_N2
    kh2 = f1_ref.shape[0]

    @pl.when(pl.program_id(0) == 0)
    def _():
        freq = freq_ref[...]
        h1 = jnp.sin(freq * (_dot3(z_ref[...], w1_ref[...]) + b1_ref[...]))
        h_scr[...] = jnp.sin(freq * (_dot3(h1, w2_ref[...]) + b2_ref[...]))

    h = h_scr[...]
    window = jnp.exp(-t_ref[...] * delta_ref[...])
    hf_scr[...] = _dot3(h, w3f_ref[...]) * window
    hb = _dot3(h, w3b_ref[...]) * window
    rows = lax.broadcasted_iota(jnp.int32, hb.shape, 0)
    hb_scr[...] = jnp.where(rows == 0, 0.0, hb)
    _fft_stage1(hf_scr, f1_ref, sf_scr)
    _fft_stage1(hb_scr, f1_ref, sb_scr)

    def mid(k1, carry):
        m = _fft_mid_matrix(k1, a3_ref, b3_ref, twr_ref, twi_ref)
        cf = _dot(m, _fft_load_k1(sf_scr, k1, kh2))
        cb = _dot(m, _fft_load_k1(sb_scr, k1, kh2))
        kf_ref[0, k1] = jnp.concatenate([cf[:n2] + cb[:n2], cf[n2:] - cb[n2:]], axis=0)
        return carry

    lax.fori_loop(0, kh2 // 2, mid, 0, unroll=FFT_UNROLL)


def _filter_spectrum(zpos, tcol, w1, b1, w2, b2, w3, freq, deltas, tabs, *, ct):
    L = zpos.shape[0]
    hy_dim = deltas.shape[1]
    n_tiles = hy_dim // ct
    kh2 = tabs["f1"].shape[0]
    kh = kh2 // 2
    tab_list = [tabs[k] for k in ("f1", "a3", "b3", "twr", "twi")]
    full = lambda a: pl.BlockSpec(a.shape, lambda t: (0,) * a.ndim)
    row = lambda a: a.reshape(1, -1)
    small = [zpos, tcol, w1, row(b1), w2, row(b2)]
    return pl.pallas_call(
        _filter_kernel,
        grid=(n_tiles,),
        in_specs=[full(a) for a in small]
                 + [pl.BlockSpec((w3.shape[0], ct), lambda t: (0, t)),
                    pl.BlockSpec((w3.shape[0], ct), lambda t: (0, n_tiles + t)),
                    full(row(freq)),
                    pl.BlockSpec((1, ct), lambda t: (0, t))]
                 + [full(a) for a in tab_list],
        out_specs=pl.BlockSpec((1, kh, 2 * FFT_N2, ct), lambda t: (t, 0, 0, 0)),
        out_shape=jax.ShapeDtypeStruct((n_tiles, kh, 2 * FFT_N2, ct), F32),
        scratch_shapes=[pltpu.VMEM((L, w2.shape[1]), F32), pltpu.VMEM((L, ct), F32), pltpu.VMEM((L, ct), F32),
                        pltpu.VMEM((FFT_N2 * kh2, ct), F32), pltpu.VMEM((FFT_N2 * kh2, ct), F32)],
        compiler_params=_cparams(("arbitrary",)),
        name="hyena_filter",
    )(*small, w3, w3, row(freq), deltas, *tab_list)


def _outproj_kernel(of_ref, ob_ref, gate_ref, ohy_ref, x_ref, er_ref, ec_ref, g1_ref, w_ref, lng_ref, lnb_ref,
                    o_ref):
    rw = of_ref.shape[2]
    a_rw = ((of_ref[0] + ob_ref[0]) * gate_ref[0].astype(F32)).astype(BF16)
    mix = jnp.dot(a_rw, w_ref[0:rw, :], preferred_element_type=F32)
    mix = mix + jnp.dot(ohy_ref[0].astype(BF16), w_ref[rw:, :], preferred_element_type=F32)
    gmix = g1_ref[0] * mix
    tm = x_ref.shape[1]
    for q in range(tm // GRID_W):
        rows = slice(q * GRID_W, (q + 1) * GRID_W)
        xv = _add_pos_rows(x_ref[0, rows, :], er_ref, ec_ref, q)
        o_ref[0, rows, :] = _layer_norm(ALPHA * xv + gmix[rows], lng_ref[...], lnb_ref[...])


def _outproj(o_f, o_b, gate, o_hy, x, er, ec, mod3, w_out, ln_g, ln_b, *, tm):
    bsz, L, d = x.shape
    rw = gate.shape[2]
    hy = o_hy.shape[2]
    gr = tm // GRID_W
    return pl.pallas_call(
        _outproj_kernel,
        grid=(bsz, L // tm),
        in_specs=[pl.BlockSpec((1, tm, rw), lambda b, i: (b, i, 0)),
                  pl.BlockSpec((1, tm, rw), lambda b, i: (b, i, 0)),
                  pl.BlockSpec((1, tm, rw), lambda b, i: (b, i, 0)),
                  pl.BlockSpec((1, tm, hy), lambda b, i: (b, i, 0)),
                  pl.BlockSpec((1, tm, d), lambda b, i: (b, i, 0)),
                  pl.BlockSpec((gr, d // 2), lambda b, i: (i, 0)),
                  pl.BlockSpec((GRID_W, d // 2), lambda b, i: (0, 0)),
                  pl.BlockSpec((1, 1, d), lambda b, i: (b, 0, 2)),
                  pl.BlockSpec((d, d), lambda b, i: (0, 0)),
                  pl.BlockSpec((1, d), lambda b, i: (0, 0)),
                  pl.BlockSpec((1, d), lambda b, i: (0, 0))],
        out_specs=pl.BlockSpec((1, tm, d), lambda b, i: (b, i, 0)),
        out_shape=jax.ShapeDtypeStruct((bsz, L, d), F32),
        compiler_params=_cparams(("arbitrary", "arbitrary")),
        name="outproj_ln",
    )(o_f, o_b, gate, o_hy, x, er, ec, mod3, w_out, ln_g.reshape(1, d), ln_b.reshape(1, d))


def _ffn_kernel(x_ref, sh_ref, sc_ref, g2_ref, w1_ref, w3_ref, w2_ref, lng_ref, lnb_ref, o_ref, a_scr, acc_scr):
    f = pl.program_id(2)

    @pl.when(f == 0)
    def _():
        a_scr[...] = (x_ref[0] * (1.0 + sc_ref[0]) + sh_ref[0]).astype(BF16)
        acc_scr[...] = jnp.zeros_like(acc_scr)

    a = a_scr[...]
    h1 = jnp.dot(a, w1_ref[...], preferred_element_type=F32)
    h3 = jnp.dot(a, w3_ref[...], preferred_element_type=F32)
    h = (h1 * _sigmoid(h1) * h3).astype(BF16)
    acc_scr[...] += jnp.dot(h, w2_ref[...], preferred_element_type=F32)

    @pl.when(f == pl.num_programs(2) - 1)
    def _():
        o_ref[0] = _layer_norm(ALPHA * x_ref[0] + g2_ref[0] * acc_scr[...], lng_ref[...], lnb_ref[...])


def _ffn(x1, mod3, w1, w3, w2, ln_g, ln_b, *, tm, tf):
    bsz, L, d = x1.shape
    dff = w1.shape[1]
    return pl.pallas_call(
        _ffn_kernel,
        grid=(bsz, L // tm, dff // tf),
        in_specs=[pl.BlockSpec((1, tm, d), lambda b, i, f: (b, i, 0)),
                  pl.BlockSpec((1, 1, d), lambda b, i, f: (b, 0, 3)),
                  pl.BlockSpec((1, 1, d), lambda b, i, f: (b, 0, 4)),
                  pl.BlockSpec((1, 1, d), lambda b, i, f: (b, 0, 5)),
                  pl.BlockSpec((d, tf), lambda b, i, f: (0, f)),
                  pl.BlockSpec((d, tf), lambda b, i, f: (0, f)),
                  pl.BlockSpec((tf, d), lambda b, i, f: (f, 0)),
                  pl.BlockSpec((1, d), lambda b, i, f: (0, 0)),
                  pl.BlockSpec((1, d), lambda b, i, f: (0, 0))],
        out_specs=pl.BlockSpec((1, tm, d), lambda b, i, f: (b, i, 0)),
        out_shape=jax.ShapeDtypeStruct((bsz, L, d), F32),
        scratch_shapes=[pltpu.VMEM((tm, d), BF16), pltpu.VMEM((tm, d), F32)],
        compiler_params=_cparams(("arbitrary", "arbitrary", "arbitrary")),
        name="ffn_ln",
    )(x1, mod3, mod3, mod3, w1, w3, w2, ln_g.reshape(1, d), ln_b.reshape(1, d))


def _pos_tables(rows, d):
    quarter = d // 4
    omega = 1.0 / (10000.0 ** (jnp.arange(quarter, dtype=F32) / quarter))
    er = jnp.arange(rows, dtype=F32)[:, None] * omega
    ec = jnp.arange(GRID_W, dtype=F32)[:, None] * omega
    er = jnp.concatenate([jnp.sin(er), jnp.cos(er)], -1)
    ec = jnp.concatenate([jnp.sin(ec), jnp.cos(ec)], -1)
    return er, ec


def _filter_positions(L):
    pos = jnp.arange(L, dtype=F32)[:, None]
    t = jnp.linspace(0.0, 1.0, L, dtype=F32)[:, None]
    bands = jnp.linspace(1e-4, N_BANDS - 1, N_BANDS, dtype=F32)[None, :]
    ang = 2.0 * math.pi * bands * pos / L
    z = jnp.concatenate([t, jnp.cos(ang), -jnp.sin(ang)], -1)
    return z, t


def _pad_cols(a, width):
    return jnp.pad(a, ((0, 0), (0, width - a.shape[1])))


def _pad_rows(a, height):
    return jnp.pad(a, ((0, height - a.shape[0]), (0, 0)))


def kernel(x, c, ctx, c_ctx, w_ada, b_ada, w_in, conv_rw, conv_hy, conv_hy_b, w0_f, w_up_f, a0_f, a_up_f, w0_b, w_up_b, a0_b, a_up_b, k_k, k_a, r_k, g_up, gn_g, gn_b, filt_w1, filt_b1, filt_w2, filt_b2, filt_w3, sin_freq, hy_bias, w_out, ln1_g, ln1_b, ffn_w1, ffn_w3, ffn_w2, ln2_g, ln2_b):
    bsz, L, d = x.shape
    rw = k_k.shape[1]
    hy = hy_bias.shape[1]
    dl, il, gl = w_up_f.shape[1], a_up_f.shape[1], g_up.shape[1]
    lo_w = 4 * LANE
    rkv_cols = 3 * rw

    wi = w_in[0]
    lo0 = rkv_cols

    def lora_layout(a):
        return jnp.concatenate([_pad_cols(a[:, lo0:lo0 + dl], LANE),
                                _pad_cols(a[:, lo0 + dl:lo0 + dl + il], LANE),
                                _pad_cols(a[:, lo0 + dl + il:lo0 + dl + il + gl], 2 * LANE)], axis=1)

    hy0 = lo0 + dl + il + gl
    w_lat = jnp.concatenate([wi[:, :rkv_cols], wi[:, hy0:], lora_layout(wi)], axis=1).astype(BF16)
    w_ctx = jnp.concatenate([wi[:, :rkv_cols], lora_layout(wi)], axis=1).astype(BF16)
    cw_rkv = conv_rw[0][:, :rkv_cols]
    cw_lo = lora_layout(conv_rw[0])
    wup = _pad_rows(jnp.concatenate([w_up_f[0], w_up_b[0]], axis=1), LANE)
    aup = _pad_rows(jnp.concatenate([a_up_f[0], a_up_b[0]], axis=1), LANE)
    gup = _pad_rows(g_up[0], 2 * LANE)
    w0 = jnp.concatenate([w0_f[0], w0_b[0]])
    a0 = jnp.concatenate([a0_f[0], a0_b[0]])

    cc = _pad_rows(jnp.concatenate([c, c_ctx[None, :]], axis=0), SUB)
    mod = _adaln(cc, w_ada[0], b_ada[0])
    mod3 = mod.reshape(SUB, 1, 6 * d)

    er, ec = _pos_tables(L // GRID_W, d)

    p_lat = _inproj(x, er, ec, mod3, w_lat, tm=1024, tn=512, add_pos=True, ctx_row=None)
    p_ctx = _inproj(ctx, er, ec, mod3, w_ctx, tm=ctx.shape[1], tn=512, add_pos=False, ctx_row=bsz)

    lo_cols = (rkv_cols + 3 * hy, rkv_cols + 3 * hy + lo_w)
    r, v, kk, gate, lw, kh, bvec = _rwkv_prep(p_lat, p_ctx, cw_rkv, cw_lo, wup, aup, gup, w0, a0,
                                              k_k[0], k_a[0], rw=rw, rkv_cols=rkv_cols, lo_cols=lo_cols)
    o_f, o_b = _rwkv_scan(r, v, kk, lw, kh, bvec, r_k[0], gn_g[0], gn_b[0], n_lat_tokens=L)

    tabs = _fft_tables(L)
    zpos, tcol = _filter_positions(L)
    max_decay = math.log(DECAY_TARGET) / FAST_DECAY_PCT
    min_decay = math.log(DECAY_TARGET) / SLOW_DECAY_PCT
    deltas = jnp.abs(jnp.linspace(min_decay, max_decay, hy, dtype=F32))[None, :]
    kf = _filter_spectrum(_pad_cols(zpos, LANE), tcol, _pad_rows(filt_w1[0], LANE), filt_b1[0], filt_w2[0],
                          filt_b2[0], filt_w3[0], sin_freq[0], deltas, tabs, ct=LANE)
    o_hy = _hyena_conv(p_lat, conv_hy[0], conv_hy_b[0], hy_bias[0], kf, tabs,
                       hy_col0=rkv_cols, hy_dim=hy, ct=LANE)

    x1 = _outproj(o_f, o_b, gate, o_hy, x, er, ec, mod3, w_out[0].astype(BF16), ln1_g[0], ln1_b[0], tm=512)
    return _ffn(x1, mod3, ffn_w1[0].astype(BF16), ffn_w3[0].astype(BF16), ffn_w2[0].astype(BF16),
                ln2_g[0], ln2_b[0], tm=512, tf=512)
```

```python
import functools
import math

import numpy as np
import jax
import jax.numpy as jnp
from jax import lax
from jax.experimental import pallas as pl
from jax.experimental.pallas import tpu as pltpu

F32 = jnp.float32
BF16 = jnp.bfloat16

GRID_W = 64
HEAD_DIM = 64
N_BANDS = 16
DECAY_TARGET = 1e-2
FAST_DECAY_PCT = 0.3
SLOW_DECAY_PCT = 1.5
LN_EPS = 1e-5
GN_EPS = 64e-5
ALPHA = 2.0 ** 0.25

LANE = 128
SUB = 8
VMEM_LIMIT = 56 * 1024 * 1024

SCAN_CHUNK = 64
SCAN_GROUP = 2
FFT_N2 = 64
FFT_UNROLL = 4


def _cparams(sem):
    return pltpu.CompilerParams(dimension_semantics=sem, vmem_limit_bytes=VMEM_LIMIT)


def _dot(a, b, dims=(((1,), (0,)), ((), ()))):
    return lax.dot_general(a.astype(BF16), b.astype(BF16), dims, preferred_element_type=F32)


def _dot_nt(a, b):
    return _dot(a, b, (((1,), (1,)), ((), ())))


def _dot_tn(a, b):
    return _dot(a, b, (((0,), (0,)), ((), ())))


def _split2(a):
    hi = a.astype(BF16)
    lo = (a - hi.astype(F32)).astype(BF16)
    return hi, lo


def _dot3(a, b):
    ah, al = _split2(a)
    bh, bl = _split2(b)
    return _dot(ah, bh) + (_dot(al, bh) + _dot(ah, bl))


def _dot_exact_rhs(a, b_exact):
    a1 = a.astype(BF16)
    r1 = a - a1.astype(F32)
    a2 = r1.astype(BF16)
    a3 = (r1 - a2.astype(F32)).astype(BF16)
    return _dot(a1, b_exact) + (_dot(a2, b_exact) + _dot(a3, b_exact))


def _dot_exact_lhs(a_exact, b):
    b1 = b.astype(BF16)
    r1 = b - b1.astype(F32)
    b2 = r1.astype(BF16)
    b3 = (r1 - b2.astype(F32)).astype(BF16)
    return _dot(a_exact, b1) + (_dot(a_exact, b2) + _dot(a_exact, b3))


def _sigmoid(x):
    return 1.0 / (1.0 + jnp.exp(-x))


def _layer_norm(h, g, b):
    mu = jnp.mean(h, -1, keepdims=True)
    hc = h - mu
    var = jnp.mean(hc * hc, -1, keepdims=True)
    return hc * lax.rsqrt(var + LN_EPS) * g + b


def _adaln_kernel(c_ref, w_ref, b_ref, o_ref):
    cv = c_ref[...]
    a = cv * _sigmoid(cv)
    o_ref[...] = _dot3(a, w_ref[...]) + b_ref[...]


def _adaln(cc, w, b, tn=1024):
    m, d = cc.shape
    n = w.shape[1]
    return pl.pallas_call(
        _adaln_kernel,
        grid=(n // tn,),
        in_specs=[pl.BlockSpec((m, d), lambda j: (0, 0)),
                  pl.BlockSpec((d, tn), lambda j: (0, j)),
                  pl.BlockSpec((1, tn), lambda j: (0, j))],
        out_specs=pl.BlockSpec((m, tn), lambda j: (0, j)),
        out_shape=jax.ShapeDtypeStruct((m, n), F32),
        compiler_params=_cparams(("arbitrary",)),
        name="adaln",
    )(cc, w, b.reshape(1, n))


HALO = 16


def _add_pos(xv, er_row, ec_rows):
    half = ec_rows.shape[1]
    return jnp.concatenate([xv[:, :half] + er_row, xv[:, half:] + ec_rows], axis=1)


def _inproj_kernel(x_ref, xp_ref, xn_ref, er_ref, ec_ref, sh_ref, sc_ref, w_ref, cw_ref, cb_ref, o_ref, a_scr,
                   *, add_pos):
    i = pl.program_id(1)
    n_i = pl.num_programs(1)
    tm = x_ref.shape[1]
    gr = tm // GRID_W

    @pl.when(pl.program_id(2) == 0)
    def _():
        scale = 1.0 + sc_ref[0]
        shift = sh_ref[0]

        def put(rows, xv, keep=None):
            a = xv * scale + shift
            if keep is not None:
                a = jnp.where(keep, a, 0.0)
            a_scr[rows, :] = a.astype(BF16)

        has_prev = i > 0
        has_next = i < n_i - 1
        lo_rows, hi_rows = slice(0, HALO), slice(HALO + tm, 2 * HALO + tm)
        if add_pos:
            n_er = er_ref.shape[0]
            for q in range(gr):
                rows = slice(q * GRID_W, (q + 1) * GRID_W)
                put(slice(HALO + q * GRID_W, HALO + (q + 1) * GRID_W),
                    _add_pos(x_ref[0, rows, :], er_ref[pl.ds(i * gr + q, 1), :], ec_ref[...]))
            put(lo_rows, _add_pos(xp_ref[0], er_ref[pl.ds(jnp.maximum(i * gr - 1, 0), 1), :],
                                  ec_ref[GRID_W - HALO:, :]), has_prev)
            put(hi_rows, _add_pos(xn_ref[0], er_ref[pl.ds(jnp.minimum((i + 1) * gr, n_er - 1), 1), :],
                                  ec_ref[:HALO, :]), has_next)
        else:
            put(lo_rows, xp_ref[0], has_prev)
            put(slice(HALO, HALO + tm), x_ref[0])
            put(hi_rows, xn_ref[0], has_next)

    n_rows = tm + 2 * HALO
    half = w_ref.shape[1] // 2
    for h in range(2):
        cols = slice(h * half, (h + 1) * half)
        res = jnp.dot(a_scr[...], w_ref[:, cols], preferred_element_type=F32)
        out = (pltpu.roll(res, 1, 0) * cw_ref[0:1, cols] + res * cw_ref[1:2, cols]
               + pltpu.roll(res, n_rows - 1, 0) * cw_ref[2:3, cols] + cb_ref[:, cols])
        o_ref[0, :, cols] = out[HALO:HALO + tm].astype(o_ref.dtype)


def _inproj(x, er, ec, mod3, w, cw, cb, *, tm, tn, add_pos, ctx_row):
    bsz, lx, d = x.shape
    n = w.shape[1]
    hb = tm // HALO
    n_hb = lx // HALO
    row = (lambda b: ctx_row) if ctx_row is not None else (lambda b: b)
    return pl.pallas_call(
        functools.partial(_inproj_kernel, add_pos=add_pos),
        grid=(bsz, lx // tm, n // tn),
        in_specs=[pl.BlockSpec((1, tm, d), lambda b, i, j: (b, i, 0)),
                  pl.BlockSpec((1, HALO, d), lambda b, i, j: (b, jnp.maximum(i * hb - 1, 0), 0)),
                  pl.BlockSpec((1, HALO, d), lambda b, i, j: (b, jnp.minimum((i + 1) * hb, n_hb - 1), 0)),
                  pl.BlockSpec(er.shape, lambda b, i, j: (0, 0)),
                  pl.BlockSpec(ec.shape, lambda b, i, j: (0, 0)),
                  pl.BlockSpec((1, 1, d), lambda b, i, j: (row(b), 0, 0)),
                  pl.BlockSpec((1, 1, d), lambda b, i, j: (row(b), 0, 1)),
                  pl.BlockSpec((d, tn), lambda b, i, j: (0, j)),
                  pl.BlockSpec((3, tn), lambda b, i, j: (0, j)),
                  pl.BlockSpec((1, tn), lambda b, i, j: (0, j))],
        out_specs=pl.BlockSpec((1, tm, tn), lambda b, i, j: (b, i, j)),
        out_shape=jax.ShapeDtypeStruct((bsz, lx, n), BF16),
        scratch_shapes=[pltpu.VMEM((tm + 2 * HALO, d), BF16)],
        compiler_params=_cparams(("arbitrary", "arbitrary", "arbitrary")),
        name="inproj_pos" if add_pos else "inproj_ctx",
    )(x, x, x, er, ec, mod3, mod3, w, cw, cb)


def _head_sum(x, ones_bd):
    outs = []
    for j in range(x.shape[1] // LANE):
        outs.append(_dot_exact_rhs(x[:, j * LANE:(j + 1) * LANE], ones_bd))
    return jnp.concatenate(outs, axis=1)


def _prep_kernel(p_rkv, p_lo, pc_rkv, pc_lo, wup, aup, gup, w0, a0, kk_w, ka_w, ones_ref,
                 r_out, v_out, kk_out, gate_out, lw_out, kh_out, b_out, *, n_lat):
    is_ctx = pl.program_id(1) == n_lat
    rw = kk_out.shape[2]
    ones_bd = ones_ref[...]

    def rkv(s):
        cols = slice(s * rw, (s + 1) * rw)
        return jnp.where(is_ctx, pc_rkv[0, :, cols], p_rkv[0, :, cols])

    r_out[0] = rkv(0)
    v_out[0] = rkv(2)
    k = rkv(1).astype(F32)
    kkr = k * kk_w[...]
    nrm = jnp.sqrt(_head_sum(kkr * kkr, ones_bd))
    kk = kkr / jnp.maximum(nrm, 1e-12)
    kk_out[0] = kk.astype(kk_out.dtype)

    lo = jnp.where(is_ctx, pc_lo[0], p_lo[0]).astype(F32)
    tw = jnp.tanh(lo[:, 0:LANE])
    al = lo[:, LANE:2 * LANE]
    gl = _sigmoid(lo[:, 2 * LANE:])
    gate_out[0] = _dot(gl, gup[...]).astype(gate_out.dtype)
    wx = _dot3(tw, wup[...]) + w0[...]
    ax = _dot3(al, aup[...]) + a0[...]
    for d in range(2):
        cols = slice(d * rw, (d + 1) * rw)
        lw_out[d, 0] = -math.exp(-0.5) * _sigmoid(wx[:, cols])
        ia = _sigmoid(ax[:, cols])
        kh_out[d, 0] = (k * (1.0 + (ia - 1.0) * ka_w[...])).astype(kh_out.dtype)
        b_out[d, 0] = (kk * ia).astype(b_out.dtype)


def _rwkv_prep(p, pc, wup, aup, gup, w0, a0, k_k, k_a, *, rw, rkv_cols, lo_cols):
    bsz, lq, _ = p.shape
    ctx = pc.shape[1]
    tr = ctx
    n_lat = lq // tr
    t_all = lq + ctx
    lo_w = lo_cols[1] - lo_cols[0]
    lo_blk = lo_cols[0] // lo_w
    lo_blk_c = rkv_cols // lo_w
    lat = lambda i: jnp.minimum(i, n_lat - 1)
    ones_bd = (np.arange(LANE)[:, None] // HEAD_DIM == np.arange(LANE)[None, :] // HEAD_DIM)
    ones_bd = jnp.asarray(ones_bd, BF16)
    full = lambda a: pl.BlockSpec(a.shape, lambda b, i: (0,) * a.ndim)
    row = lambda a: a.reshape(1, -1)
    weights = [wup, aup, gup, row(w0), row(a0), row(k_k), row(k_a), ones_bd]
    shared = jax.ShapeDtypeStruct((bsz, t_all, rw), BF16)
    perdir = jax.ShapeDtypeStruct((2, bsz, t_all, rw), BF16)
    perdir_f32 = jax.ShapeDtypeStruct((2, bsz, t_all, rw), F32)
    o_shared = pl.BlockSpec((1, tr, rw), lambda b, i: (b, i, 0))
    o_dir = pl.BlockSpec((2, 1, tr, rw), lambda b, i: (0, b, i, 0))
    return pl.pallas_call(
        functools.partial(_prep_kernel, n_lat=n_lat),
        grid=(bsz, n_lat + 1),
        in_specs=[pl.BlockSpec((1, tr, rkv_cols), lambda b, i: (b, lat(i), 0)),
                  pl.BlockSpec((1, tr, lo_w), lambda b, i: (b, lat(i), lo_blk)),
                  pl.BlockSpec((1, tr, rkv_cols), lambda b, i: (b, 0, 0)),
                  pl.BlockSpec((1, tr, lo_w), lambda b, i: (b, 0, lo_blk_c))]
                 + [full(a) for a in weights],
        out_specs=[o_shared, o_shared, o_shared, o_shared, o_dir, o_dir, o_dir],
        out_shape=[shared, shared, shared, shared, perdir_f32, perdir, perdir],
        compiler_params=_cparams(("arbitrary", "arbitrary")),
        name="rwkv_prep",
    )(p, p, pc, pc, *weights)


def _scan_kernel(rf_ref, vf_ref, kkf_ref, lwf_ref, khf_ref, bf_ref, rb_ref, vb_ref, kkb_ref, lwb_ref, khb_ref,
                 bb_ref, rk_ref, gng_ref, gnb_ref, ones_ref, of_ref, ob_ref, s_scr):
    C = rf_ref.shape[1]
    G = SCAN_GROUP
    W = G * HEAD_DIM
    n = G * C
    n_groups = rf_ref.shape[2] // W

    @pl.when(pl.program_id(1) == 0)
    def _():
        s_scr[...] = jnp.zeros_like(s_scr)

    bi = lax.broadcasted_iota(jnp.int32, (n, n), 0)
    bj = lax.broadcasted_iota(jnp.int32, (n, n), 1)
    same = (bi // C) == (bj // C)
    li = bi % C
    lj = bj % C
    eye = jnp.where(bi == bj, 1.0, 0.0)
    hrow = lax.broadcasted_iota(jnp.int32, (n, W), 0) // C
    hlane = lax.broadcasted_iota(jnp.int32, (n, W), 1) // HEAD_DIM
    hmask = hrow == hlane
    ti = lax.broadcasted_iota(jnp.int32, (C, C), 0)
    tj = lax.broadcasted_iota(jnp.int32, (C, C), 1)

    def rep(x):
        return jnp.where(hmask, jnp.concatenate([x] * G, axis=0), 0.0).astype(BF16)

    chains = []
    dirs = ((rf_ref, vf_ref, kkf_ref, lwf_ref, khf_ref, bf_ref, False),
            (rb_ref, vb_ref, kkb_ref, lwb_ref, khb_ref, bb_ref, True))
    per_dir = []
    for di, (r_ref, v_ref, kk_ref, lw_ref, kh_ref, b_ref, rev) in enumerate(dirs):
        r = r_ref[0].astype(F32)
        v = v_ref[0].astype(F32)
        kk = kk_ref[0].astype(F32)
        lw = lw_ref[0, 0]
        kh = kh_ref[0, 0].astype(F32)
        bv = b_ref[0, 0].astype(F32)
        before = (lj > li) if rev else (lj < li)
        strict = jnp.logical_and(same, before)
        incl = jnp.logical_and(same, jnp.logical_or(before, li == lj))
        tri = jnp.where((tj >= ti) if rev else (tj <= ti), 1.0, 0.0).astype(BF16)
        cum = _dot_exact_lhs(tri, lw)
        tot = cum[0:1, :] if rev else cum[C - 1:C, :]
        e_neg = jnp.exp(-cum)
        e_end = jnp.exp(tot - cum)
        rt = r * jnp.exp(cum)
        at = -kk * jnp.exp(cum - lw)
        bt = bv * e_neg
        kt = kh * e_neg
        bp = bv * e_end
        kp = kh * e_end
        per_dir.append((r, v, kh, jnp.exp(tot)))
        for g in range(n_groups):
            cols = slice(g * W, (g + 1) * W)
            chains.append(dict(
                di=di, g=g, cols=cols, strict=strict, incl=incl,
                v2=rep(v[:, cols]),
                ar=jnp.concatenate([rep(at[:, cols]), rep(rt[:, cols])], axis=0),
                bk=jnp.concatenate([rep(bt[:, cols]), rep(kt[:, cols])], axis=0),
                bkp=jnp.concatenate([rep(bp[:, cols]), rep(kp[:, cols])], axis=0)))

    for ch in chains:
        ch["tt"] = _dot_nt(ch["ar"], ch["bk"])
    for ch in chains:
        ch["p"] = jnp.where(ch["strict"], ch["tt"][:n, :n], 0.0).astype(BF16)
        ch["tinv"] = eye + ch["p"].astype(F32)
    step = 2
    while step < C:
        for ch in chains:
            ch["p"] = _dot(ch["p"], ch["p"]).astype(BF16)
        for ch in chains:
            ch["tinv"] = ch["tinv"] + _dot(ch["p"], ch["tinv"])
        step *= 2
    for ch in chains:
        ch["s0"] = s_scr[ch["di"], ch["g"]]
        ch["xs"] = _dot_nt(ch["ar"], ch["s0"])
    for ch in chains:
        a_ak = jnp.where(ch["strict"], ch["tt"][:n, n:], 0.0)
        ch["rhs"] = ch["xs"][:n] + _dot(a_ak, ch["v2"])
    for ch in chains:
        u = _dot(ch["tinv"], ch["rhs"])
        ch["uv"] = jnp.concatenate([u.astype(BF16), ch["v2"]], axis=0)
    for ch in chains:
        a_r = jnp.concatenate([jnp.where(ch["incl"], ch["tt"][n:, :n], 0.0),
                               jnp.where(ch["incl"], ch["tt"][n:, n:], 0.0)], axis=1)
        ch["y2"] = ch["xs"][n:] + _dot(a_r, ch["uv"])
    for ch in chains:
        e_tot = per_dir[ch["di"]][3]
        s_scr[ch["di"], ch["g"]] = ch["s0"] * e_tot[:, ch["cols"]] + _dot_tn(ch["uv"], ch["bkp"])

    ones_bd = ones_ref[...]
    inv = 1.0 / HEAD_DIM
    for di, o_ref in enumerate((of_ref, ob_ref)):
        r, v, kh, _ = per_dir[di]
        ys = []
        for ch in chains:
            if ch["di"] == di:
                y = ch["y2"][0:C]
                for h in range(1, G):
                    y = y + ch["y2"][h * C:(h + 1) * C]
                ys.append(y)
        y = jnp.concatenate(ys, axis=1)
        mu = _head_sum(y, ones_bd) * inv
        yc = y - mu
        var = _head_sum(yc * yc, ones_bd) * inv
        gn = yc * lax.rsqrt(var + GN_EPS) * gng_ref[...] + gnb_ref[...]
        bonus = _head_sum(r * kh * rk_ref[...], ones_bd) * v
        o_ref[0] = gn + bonus


def _rwkv_scan(r, v, kk, lw, kh, bvec, r_k, gn_g, gn_b, *, n_lat_tokens):
    bsz, t_all, rw = r.shape
    C = SCAN_CHUNK
    n_chunks = t_all // C
    n_ctx = (t_all - n_lat_tokens) // C
    n_lat = n_lat_tokens // C
    W = SCAN_GROUP * HEAD_DIM

    blk_f = lambda c: jnp.where(c < n_ctx, n_lat + c, c - n_ctx)
    blk_b = lambda c: n_chunks - 1 - c
    ones_bd = (np.arange(LANE)[:, None] // HEAD_DIM == np.arange(LANE)[None, :] // HEAD_DIM)
    ones_bd = jnp.asarray(ones_bd, BF16)
    shared = lambda blk: pl.BlockSpec((1, C, rw), lambda b, c: (b, blk(c), 0))
    perdir = lambda d, blk: pl.BlockSpec((1, 1, C, rw), lambda b, c: (d, b, blk(c), 0))
    vec = pl.BlockSpec((1, rw), lambda b, c: (0, 0))
    out = jax.ShapeDtypeStruct((bsz, t_all, rw), F32)
    return pl.pallas_call(
        _scan_kernel,
        grid=(bsz, n_chunks),
        in_specs=[shared(blk_f), shared(blk_f), shared(blk_f), perdir(0, blk_f), perdir(0, blk_f), perdir(0, blk_f),
                  shared(blk_b), shared(blk_b), shared(blk_b), perdir(1, blk_b), perdir(1, blk_b), perdir(1, blk_b),
                  vec, vec, vec, pl.BlockSpec((LANE, LANE), lambda b, c: (0, 0))],
        out_specs=[shared(blk_f), shared(blk_b)],
        out_shape=[out, out],
        scratch_shapes=[pltpu.VMEM((2, rw // W, W, W), F32)],
        compiler_params=_cparams(("arbitrary", "arbitrary")),
        name="rwkv_scan",
    )(r, v, kk, lw, kh, bvec, r, v, kk, lw, kh, bvec,
      r_k.reshape(1, rw), gn_g.reshape(1, rw), gn_b.reshape(1, rw), ones_bd)


def _fft_tables(L):
    n2 = FFT_N2
    n_all = 2 * L
    n1 = n_all // n2
    nh = n1 // 2
    two_pi = 2.0 * np.pi
    kh = -(-(nh + 1) // 4) * 4
    k1 = np.arange(kh)
    pair_w = np.where(k1 > nh, 0.0, np.where((k1 == 0) | (k1 == nh), 1.0, 2.0))
    th1 = two_pi * np.outer(k1, np.arange(nh)) / n1
    f1 = np.concatenate([np.cos(th1), -np.sin(th1)], 0)
    th2 = two_pi * np.outer(np.arange(n2), np.arange(n2)) / n2
    fr, fi = np.cos(th2), -np.sin(th2)
    a3 = np.block([[fr, -fi], [fi, fr]])
    b3 = np.block([[-fi, -fr], [fr, -fi]])
    tht = two_pi * np.outer(k1, np.arange(n2)) / n_all
    twr = np.tile(np.cos(tht), (1, 2))
    twi = np.tile(-np.sin(tht), (1, 2))
    gr, gi = np.cos(th2), np.sin(th2)
    m3i = np.block([[gr, -gi], [gi, gr]]) / n_all
    th4 = two_pi * np.outer(np.arange(nh), k1) / n1
    cos4, sin4 = np.cos(th4) * pair_w, np.sin(th4) * pair_w
    ca = np.concatenate([cos4, -sin4], 1)
    cb = np.concatenate([-sin4, -cos4], 1)
    th5 = two_pi * np.outer(np.arange(n2), k1) / n_all
    t2r, t2i = np.tile(np.cos(th5), (1, 2)), np.tile(np.sin(th5), (1, 2))
    f = lambda a: jnp.asarray(a, F32)
    return dict(f1=f(f1), a3=f(a3), b3=f(b3), twr=f(twr), twi=f(twi),
                m3i=f(m3i), ca=f(ca), cb=f(cb), t2r=f(t2r), t2i=f(t2i))


def _fft_stage1(src_ref, f1_ref, s_ref):
    kh2, nh = f1_ref.shape
    n2 = FFT_N2
    f1 = f1_ref[...].astype(BF16)

    def body(nlo, carry):
        xin = src_ref[pl.ds(nlo, nh, stride=n2), :]
        s_ref[pl.ds(pl.multiple_of(nlo * kh2, SUB), kh2), :] = _dot(f1, xin)
        return carry

    lax.fori_loop(0, n2, body, 0, unroll=FFT_UNROLL)


def _fft_mid_matrix(k1, a3_ref, b3_ref, twr_ref, twi_ref):
    tr = twr_ref[pl.ds(k1, 1), :]
    ti = twi_ref[pl.ds(k1, 1), :]
    return (a3_ref[...] * tr + b3_ref[...] * ti).astype(BF16)


def _fft_load_k1(s_ref, k1, kh2):
    n2 = FFT_N2
    re = s_ref[pl.ds(k1, n2, stride=kh2), :]
    im = s_ref[pl.ds(kh2 // 2 + k1, n2, stride=kh2), :]
    return jnp.concatenate([re, im], axis=0)


def _hyena_kernel(x0_ref, x1_ref, v_ref, bias_ref, kf_ref,
                  f1_ref, a3_ref, b3_ref, twr_ref, twi_ref, m3i_ref, ca_ref, cb_ref, t2r_ref, t2i_ref,
                  o_ref, z_scr, y_scr, s_scr, q_scr):
    n2 = FFT_N2
    kh2, nh = f1_ref.shape
    kh = kh2 // 2
    z_scr[...] = v_ref[0].astype(F32) * x1_ref[0].astype(F32)
    _fft_stage1(z_scr, f1_ref, s_scr)
    m3i = m3i_ref[...].astype(BF16)

    def mid(k1, carry):
        m = _fft_mid_matrix(k1, a3_ref, b3_ref, twr_ref, twi_ref)
        cc = _dot(m, _fft_load_k1(s_scr, k1, kh2))
        kf = kf_ref[0, k1]
        cr, ci = cc[:n2], cc[n2:]
        kr, ki = kf[:n2], kf[n2:]
        pp = jnp.concatenate([cr * kr - ci * ki, cr * ki + ci * kr], axis=0)
        q = _dot(m3i, pp)
        q_scr[pl.ds(k1, n2, stride=kh2), :] = q[:n2]
        q_scr[pl.ds(kh + k1, n2, stride=kh2), :] = q[n2:]
        return carry

    lax.fori_loop(0, kh, mid, 0, unroll=FFT_UNROLL)

    def last(m2, carry):
        f4 = ca_ref[...] * t2r_ref[pl.ds(m2, 1), :] + cb_ref[...] * t2i_ref[pl.ds(m2, 1), :]
        qm = q_scr[pl.ds(pl.multiple_of(m2 * kh2, SUB), kh2), :]
        y_scr[pl.ds(m2, nh, stride=n2), :] = _dot(f4, qm)
        return carry

    lax.fori_loop(0, n2, last, 0, unroll=FFT_UNROLL)
    z = z_scr[...]
    o_ref[0] = (y_scr[...] + z * bias_ref[...]) * x0_ref[0].astype(F32)


def _hyena_conv(p, hy_bias, kf, tabs, *, hy_col0, hy_dim, ct):
    bsz, L, _ = p.shape
    n_tiles = hy_dim // ct
    blk0 = hy_col0 // ct
    sec = hy_dim // ct
    xspec = lambda s: pl.BlockSpec((1, L, ct), lambda t, b: (b, 0, blk0 + s * sec + t))
    tab_list = [tabs[k] for k in ("f1", "a3", "b3", "twr", "twi", "m3i", "ca", "cb", "t2r", "t2i")]
    full = lambda a: pl.BlockSpec(a.shape, lambda t, b: (0,) * a.ndim)
    kh2 = tabs["f1"].shape[0]
    return pl.pallas_call(
        _hyena_kernel,
        grid=(n_tiles, bsz),
        in_specs=[xspec(0), xspec(1), xspec(2),
                  pl.BlockSpec((1, ct), lambda t, b: (0, t)),
                  pl.BlockSpec((1,) + kf.shape[1:], lambda t, b: (t, 0, 0, 0), pipeline_mode=pl.Buffered(1))]
                 + [full(a) for a in tab_list],
        out_specs=pl.BlockSpec((1, L, ct), lambda t, b: (b, 0, t)),
        out_shape=jax.ShapeDtypeStruct((bsz, L, hy_dim), F32),
        scratch_shapes=[pltpu.VMEM((L, ct), F32), pltpu.VMEM((L, ct), F32),
                        pltpu.VMEM((FFT_N2 * kh2, ct), F32), pltpu.VMEM((FFT_N2 * kh2, ct), F32)],
        compiler_params=_cparams(("arbitrary", "arbitrary")),
        name="hyena_conv",
    )(p, p, p, hy_bias.reshape(1, -1), kf, *tab_list)


def _filter_kernel(z_ref, t_ref, w1_ref, b1_ref, w2_ref, b2_ref, w3f_ref, w3b_ref, freq_ref, delta_ref,
                   f1_ref, a3_ref, b3_ref, twr_ref, twi_ref, kf_ref, h_scr, hf_scr, hb_scr, sf_scr, sb_scr):
    n2 = FFT_N2
    kh2 = f1_ref.shape[0]

    @pl.when(pl.program_id(0) == 0)
    def _():
        freq = freq_ref[...]
        h1 = jnp.sin(freq * (_dot3(z_ref[...], w1_ref[...]) + b1_ref[...]))
        h_scr[...] = jnp.sin(freq * (_dot3(h1, w2_ref[...]) + b2_ref[...]))

    h = h_scr[...]
    window = jnp.exp(-t_ref[...] * delta_ref[...])
    hf_scr[...] = _dot3(h, w3f_ref[...]) * window
    hb = _dot3(h, w3b_ref[...]) * window
    rows = lax.broadcasted_iota(jnp.int32, hb.shape, 0)
    hb_scr[...] = jnp.where(rows == 0, 0.0, hb)
    _fft_stage1(hf_scr, f1_ref, sf_scr)
    _fft_stage1(hb_scr, f1_ref, sb_scr)

    def mid(k1, carry):
        m = _fft_mid_matrix(k1, a3_ref, b3_ref, twr_ref, twi_ref)
        cf = _dot(m, _fft_load_k1(sf_scr, k1, kh2))
        cb = _dot(m, _fft_load_k1(sb_scr, k1, kh2))
        kf_ref[0, k1] = jnp.concatenate([cf[:n2] + cb[:n2], cf[n2:] - cb[n2:]], axis=0)
        return carry

    lax.fori_loop(0, kh2 // 2, mid, 0, unroll=FFT_UNROLL)


def _filter_spectrum(zpos, tcol, w1, b1, w2, b2, w3, freq, deltas, tabs, *, ct):
    L = zpos.shape[0]
    hy_dim = deltas.shape[1]
    n_tiles = hy_dim // ct
    kh2 = tabs["f1"].shape[0]
    kh = kh2 // 2
    tab_list = [tabs[k] for k in ("f1", "a3", "b3", "twr", "twi")]
    full = lambda a: pl.BlockSpec(a.shape, lambda t: (0,) * a.ndim)
    row = lambda a: a.reshape(1, -1)
    small = [zpos, tcol, w1, row(b1), w2, row(b2)]
    return pl.pallas_call(
        _filter_kernel,
        grid=(n_tiles,),
        in_specs=[full(a) for a in small]
                 + [pl.BlockSpec((w3.shape[0], ct), lambda t: (0, t)),
                    pl.BlockSpec((w3.shape[0], ct), lambda t: (0, n_tiles + t)),
                    full(row(freq)),
                    pl.BlockSpec((1, ct), lambda t: (0, t))]
                 + [full(a) for a in tab_list],
        out_specs=pl.BlockSpec((1, kh, 2 * FFT_N2, ct), lambda t: (t, 0, 0, 0)),
        out_shape=jax.ShapeDtypeStruct((n_tiles, kh, 2 * FFT_N2, ct), F32),
        scratch_shapes=[pltpu.VMEM((L, w2.shape[1]), F32), pltpu.VMEM((L, ct), F32), pltpu.VMEM((L, ct), F32),
                        pltpu.VMEM((FFT_N2 * kh2, ct), F32), pltpu.VMEM((FFT_N2 * kh2, ct), F32)],
        compiler_params=_cparams(("arbitrary",)),
        name="hyena_filter",
    )(*small, w3, w3, row(freq), deltas, *tab_list)


def _outproj_kernel(of_ref, ob_ref, gate_ref, ohy_ref, x_ref, er_ref, ec_ref, g1_ref, w_ref, lng_ref, lnb_ref,
                    o_ref):
    rw = of_ref.shape[2]
    a_rw = ((of_ref[0] + ob_ref[0]) * gate_ref[0].astype(F32)).astype(BF16)
    mix = jnp.dot(a_rw, w_ref[0:rw, :], preferred_element_type=F32)
    mix = mix + jnp.dot(ohy_ref[0].astype(BF16), w_ref[rw:, :], preferred_element_type=F32)
    gmix = g1_ref[0] * mix
    tm = x_ref.shape[1]
    for q in range(tm // GRID_W):
        rows = slice(q * GRID_W, (q + 1) * GRID_W)
        xv = _add_pos(x_ref[0, rows, :], er_ref[q:q + 1, :], ec_ref[...])
        o_ref[0, rows, :] = _layer_norm(ALPHA * xv + gmix[rows], lng_ref[...], lnb_ref[...])


def _outproj(o_f, o_b, gate, o_hy, x, er, ec, mod3, w_out, ln_g, ln_b, *, tm):
    bsz, L, d = x.shape
    rw = gate.shape[2]
    hy = o_hy.shape[2]
    gr = tm // GRID_W
    return pl.pallas_call(
        _outproj_kernel,
        grid=(bsz, L // tm),
        in_specs=[pl.BlockSpec((1, tm, rw), lambda b, i: (b, i, 0)),
                  pl.BlockSpec((1, tm, rw), lambda b, i: (b, i, 0)),
                  pl.BlockSpec((1, tm, rw), lambda b, i: (b, i, 0)),
                  pl.BlockSpec((1, tm, hy), lambda b, i: (b, i, 0)),
                  pl.BlockSpec((1, tm, d), lambda b, i: (b, i, 0)),
                  pl.BlockSpec((gr, d // 2), lambda b, i: (i, 0)),
                  pl.BlockSpec((GRID_W, d // 2), lambda b, i: (0, 0)),
                  pl.BlockSpec((1, 1, d), lambda b, i: (b, 0, 2)),
                  pl.BlockSpec((d, d), lambda b, i: (0, 0)),
                  pl.BlockSpec((1, d), lambda b, i: (0, 0)),
                  pl.BlockSpec((1, d), lambda b, i: (0, 0))],
        out_specs=pl.BlockSpec((1, tm, d), lambda b, i: (b, i, 0)),
        out_shape=jax.ShapeDtypeStruct((bsz, L, d), F32),
        compiler_params=_cparams(("arbitrary", "arbitrary")),
        name="outproj_ln",
    )(o_f, o_b, gate, o_hy, x, er, ec, mod3, w_out, ln_g.reshape(1, d), ln_b.reshape(1, d))


def _ffn_kernel(x_ref, sh_ref, sc_ref, g2_ref, w1_ref, w3_ref, w2_ref, lng_ref, lnb_ref, o_ref, a_scr, acc_scr):
    f = pl.program_id(2)

    @pl.when(f == 0)
    def _():
        a_scr[...] = (x_ref[0] * (1.0 + sc_ref[0]) + sh_ref[0]).astype(BF16)
        acc_scr[...] = jnp.zeros_like(acc_scr)

    a = a_scr[...]
    h1 = jnp.dot(a, w1_ref[...], preferred_element_type=F32)
    h3 = jnp.dot(a, w3_ref[...], preferred_element_type=F32)
    h = (h1 * _sigmoid(h1) * h3).astype(BF16)
    acc_scr[...] += jnp.dot(h, w2_ref[...], preferred_element_type=F32)

    @pl.when(f == pl.num_programs(2) - 1)
    def _():
        o_ref[0] = _layer_norm(ALPHA * x_ref[0] + g2_ref[0] * acc_scr[...], lng_ref[...], lnb_ref[...])


def _ffn(x1, mod3, w1, w3, w2, ln_g, ln_b, *, tm, tf):
    bsz, L, d = x1.shape
    dff = w1.shape[1]
    return pl.pallas_call(
        _ffn_kernel,
        grid=(bsz, L // tm, dff // tf),
        in_specs=[pl.BlockSpec((1, tm, d), lambda b, i, f: (b, i, 0)),
                  pl.BlockSpec((1, 1, d), lambda b, i, f: (b, 0, 3)),
                  pl.BlockSpec((1, 1, d), lambda b, i, f: (b, 0, 4)),
                  pl.BlockSpec((1, 1, d), lambda b, i, f: (b, 0, 5)),
                  pl.BlockSpec((d, tf), lambda b, i, f: (0, f)),
                  pl.BlockSpec((d, tf), lambda b, i, f: (0, f)),
                  pl.BlockSpec((tf, d), lambda b, i, f: (f, 0)),
                  pl.BlockSpec((1, d), lambda b, i, f: (0, 0)),
                  pl.BlockSpec((1, d), lambda b, i, f: (0, 0))],
        out_specs=pl.BlockSpec((1, tm, d), lambda b, i, f: (b, i, 0)),
        out_shape=jax.ShapeDtypeStruct((bsz, L, d), F32),
        scratch_shapes=[pltpu.VMEM((tm, d), BF16), pltpu.VMEM((tm, d), F32)],
        compiler_params=_cparams(("arbitrary", "arbitrary", "arbitrary")),
        name="ffn_ln",
    )(x1, mod3, mod3, mod3, w1, w3, w2, ln_g.reshape(1, d), ln_b.reshape(1, d))


def _pos_tables(rows, d):
    quarter = d // 4
    omega = 1.0 / (10000.0 ** (jnp.arange(quarter, dtype=F32) / quarter))
    er = jnp.arange(rows, dtype=F32)[:, None] * omega
    ec = jnp.arange(GRID_W, dtype=F32)[:, None] * omega
    er = jnp.concatenate([jnp.sin(er), jnp.cos(er)], -1)
    ec = jnp.concatenate([jnp.sin(ec), jnp.cos(ec)], -1)
    return er, ec


def _filter_positions(L):
    pos = jnp.arange(L, dtype=F32)[:, None]
    t = jnp.linspace(0.0, 1.0, L, dtype=F32)[:, None]
    bands = jnp.linspace(1e-4, N_BANDS - 1, N_BANDS, dtype=F32)[None, :]
    ang = 2.0 * math.pi * bands * pos / L
    z = jnp.concatenate([t, jnp.cos(ang), -jnp.sin(ang)], -1)
    return z, t


def _pad_cols(a, width):
    return jnp.pad(a, ((0, 0), (0, width - a.shape[1])))


def _pad_rows(a, height):
    return jnp.pad(a, ((0, height - a.shape[0]), (0, 0)))


def kernel(x, c, ctx, c_ctx, w_ada, b_ada, w_in, conv_rw, conv_hy, conv_hy_b, w0_f, w_up_f, a0_f, a_up_f, w0_b, w_up_b, a0_b, a_up_b, k_k, k_a, r_k, g_up, gn_g, gn_b, filt_w1, filt_b1, filt_w2, filt_b2, filt_w3, sin_freq, hy_bias, w_out, ln1_g, ln1_b, ffn_w1, ffn_w3, ffn_w2, ln2_g, ln2_b):
    bsz, L, d = x.shape
    rw = k_k.shape[1]
    hy = hy_bias.shape[1]
    dl, il, gl = w_up_f.shape[1], a_up_f.shape[1], g_up.shape[1]
    lo_w = 4 * LANE
    rkv_cols = 3 * rw

    wi = w_in[0]
    lo0 = rkv_cols

    def lora_layout(a):
        return jnp.concatenate([_pad_cols(a[:, lo0:lo0 + dl], LANE),
                                _pad_cols(a[:, lo0 + dl:lo0 + dl + il], LANE),
                                _pad_cols(a[:, lo0 + dl + il:lo0 + dl + il + gl], 2 * LANE)], axis=1)

    hy0 = lo0 + dl + il + gl
    w_lat = jnp.concatenate([wi[:, :rkv_cols], wi[:, hy0:], lora_layout(wi)], axis=1).astype(BF16)
    w_ctx = jnp.concatenate([wi[:, :rkv_cols], lora_layout(wi)], axis=1).astype(BF16)
    cw_rkv = conv_rw[0][:, :rkv_cols]
    cw_lo = lora_layout(conv_rw[0])
    cw_lat = jnp.concatenate([cw_rkv, conv_hy[0], cw_lo], axis=1)
    cw_ctx = jnp.concatenate([cw_rkv, cw_lo], axis=1)
    cb_lat = jnp.concatenate([jnp.zeros((rkv_cols,), F32), conv_hy_b[0], jnp.zeros((lo_w,), F32)])[None, :]
    cb_ctx = jnp.zeros((1, rkv_cols + lo_w), F32)
    wup = _pad_rows(jnp.concatenate([w_up_f[0], w_up_b[0]], axis=1), LANE)
    aup = _pad_rows(jnp.concatenate([a_up_f[0], a_up_b[0]], axis=1), LANE)
    gup = _pad_rows(g_up[0], 2 * LANE)
    w0 = jnp.concatenate([w0_f[0], w0_b[0]])
    a0 = jnp.concatenate([a0_f[0], a0_b[0]])

    cc = _pad_rows(jnp.concatenate([c, c_ctx[None, :]], axis=0), SUB)
    mod = _adaln(cc, w_ada[0], b_ada[0])
    mod3 = mod.reshape(SUB, 1, 6 * d)

    er, ec = _pos_tables(L // GRID_W, d)

    p_lat = _inproj(x, er, ec, mod3, w_lat, cw_lat, cb_lat, tm=1024, tn=512, add_pos=True, ctx_row=None)
    p_ctx = _inproj(ctx, er, ec, mod3, w_ctx, cw_ctx, cb_ctx, tm=ctx.shape[1], tn=512, add_pos=False,
                    ctx_row=bsz)

    lo_cols = (rkv_cols + 3 * hy, rkv_cols + 3 * hy + lo_w)
    r, v, kk, gate, lw, kh, bvec = _rwkv_prep(p_lat, p_ctx, wup, aup, gup, w0, a0,
                                              k_k[0], k_a[0], rw=rw, rkv_cols=rkv_cols, lo_cols=lo_cols)
    o_f, o_b = _rwkv_scan(r, v, kk, lw, kh, bvec, r_k[0], gn_g[0], gn_b[0], n_lat_tokens=L)

    tabs = _fft_tables(L)
    zpos, tcol = _filter_positions(L)
    max_decay = math.log(DECAY_TARGET) / FAST_DECAY_PCT
    min_decay = math.log(DECAY_TARGET) / SLOW_DECAY_PCT
    deltas = jnp.abs(jnp.linspace(min_decay, max_decay, hy, dtype=F32))[None, :]
    kf = _filter_spectrum(_pad_cols(zpos, LANE), tcol, _pad_rows(filt_w1[0], LANE), filt_b1[0], filt_w2[0],
                          filt_b2[0], filt_w3[0], sin_freq[0], deltas, tabs, ct=LANE)
    o_hy = _hyena_conv(p_lat, hy_bias[0], kf, tabs, hy_col0=rkv_cols, hy_dim=hy, ct=LANE)

    x1 = _outproj(o_f, o_b, gate, o_hy, x, er, ec, mod3, w_out[0].astype(BF16), ln1_g[0], ln1_b[0], tm=512)
    return _ffn(x1, mod3, ffn_w1[0].astype(BF16), ffn_w3[0].astype(BF16), ffn_w2[0].astype(BF16),
                ln2_g[0], ln2_b[0], tm=512, tf=512)
```

```python
import functools
import math

import numpy as np
import jax
import jax.numpy as jnp
from jax import lax
from jax.experimental import pallas as pl
from jax.experimental.pallas import tpu as pltpu

F32 = jnp.float32
BF16 = jnp.bfloat16

GRID_W = 64
HEAD_DIM = 64
N_BANDS = 16
DECAY_TARGET = 1e-2
FAST_DECAY_PCT = 0.3
SLOW_DECAY_PCT = 1.5
LN_EPS = 1e-5
GN_EPS = 64e-5
ALPHA = 2.0 ** 0.25

LANE = 128
SUB = 8
VMEM_LIMIT = 56 * 1024 * 1024

SCAN_CHUNK = 64
SCAN_GROUP = 2
SCAN_BLOCK = 128
FFT_N2 = 64
FFT_UNROLL = 8


def _cparams(sem):
    return pltpu.CompilerParams(dimension_semantics=sem, vmem_limit_bytes=VMEM_LIMIT)


def _dot(a, b, dims=(((1,), (0,)), ((), ()))):
    return lax.dot_general(a.astype(BF16), b.astype(BF16), dims, preferred_element_type=F32)


def _dot_nt(a, b):
    return _dot(a, b, (((1,), (1,)), ((), ())))


def _dot_tn(a, b):
    return _dot(a, b, (((0,), (0,)), ((), ())))


def _split2(a):
    hi = a.astype(BF16)
    lo = (a - hi.astype(F32)).astype(BF16)
    return hi, lo


def _dot3(a, b, b_split=None):
    ah, al = _split2(a)
    bh, bl = _split2(b) if b_split is None else b_split
    return _dot(ah, bh) + (_dot(al, bh) + _dot(ah, bl))


def _dot_exact_rhs(a, b_exact):
    a1 = a.astype(BF16)
    r1 = a - a1.astype(F32)
    a2 = r1.astype(BF16)
    a3 = (r1 - a2.astype(F32)).astype(BF16)
    return _dot(a1, b_exact) + (_dot(a2, b_exact) + _dot(a3, b_exact))


def _dot_exact_lhs(a_exact, b):
    b1 = b.astype(BF16)
    r1 = b - b1.astype(F32)
    b2 = r1.astype(BF16)
    b3 = (r1 - b2.astype(F32)).astype(BF16)
    return _dot(a_exact, b1) + (_dot(a_exact, b2) + _dot(a_exact, b3))


def _sigmoid(x):
    return 0.5 * jnp.tanh(0.5 * x) + 0.5


def _layer_norm(h, g, b):
    mu = jnp.mean(h, -1, keepdims=True)
    hc = h - mu
    var = jnp.mean(hc * hc, -1, keepdims=True)
    return hc * lax.rsqrt(var + LN_EPS) * g + b


def _adaln_kernel(c_ref, w_ref, b_ref, o_ref):
    cv = c_ref[...]
    a = cv * _sigmoid(cv)
    o_ref[...] = _dot3(a, w_ref[...]) + b_ref[...]


def _adaln(cc, w, b, tn=1024):
    m, d = cc.shape
    n = w.shape[1]
    return pl.pallas_call(
        _adaln_kernel,
        grid=(n // tn,),
        in_specs=[pl.BlockSpec((m, d), lambda j: (0, 0)),
                  pl.BlockSpec((d, tn), lambda j: (0, j)),
                  pl.BlockSpec((1, tn), lambda j: (0, j))],
        out_specs=pl.BlockSpec((m, tn), lambda j: (0, j)),
        out_shape=jax.ShapeDtypeStruct((m, n), F32),
        compiler_params=_cparams(("arbitrary",)),
        name="adaln",
    )(cc, w, b.reshape(1, n))


HALO = 16


def _add_pos(xv, er_row, ec_rows):
    half = ec_rows.shape[1]
    return jnp.concatenate([xv[:, :half] + er_row, xv[:, half:] + ec_rows], axis=1)


def _inproj_kernel(x_ref, xp_ref, xn_ref, er_ref, ec_ref, sh_ref, sc_ref, w_ref, cw_ref, cb_ref, o_ref, a_scr,
                   *, add_pos):
    i = pl.program_id(1)
    n_i = pl.num_programs(1)
    tm = x_ref.shape[1]
    gr = tm // GRID_W

    @pl.when(pl.program_id(2) == 0)
    def _():
        scale = 1.0 + sc_ref[0]
        shift = sh_ref[0]

        def put(rows, xv, keep=None):
            a = xv * scale + shift
            if keep is not None:
                a = jnp.where(keep, a, 0.0)
            a_scr[rows, :] = a.astype(BF16)

        has_prev = i > 0
        has_next = i < n_i - 1
        lo_rows, hi_rows = slice(0, HALO), slice(HALO + tm, 2 * HALO + tm)
        if add_pos:
            n_er = er_ref.shape[0]
            for q in range(gr):
                rows = slice(q * GRID_W, (q + 1) * GRID_W)
                put(slice(HALO + q * GRID_W, HALO + (q + 1) * GRID_W),
                    _add_pos(x_ref[0, rows, :], er_ref[pl.ds(i * gr + q, 1), :], ec_ref[...]))
            put(lo_rows, _add_pos(xp_ref[0], er_ref[pl.ds(jnp.maximum(i * gr - 1, 0), 1), :],
                                  ec_ref[GRID_W - HALO:, :]), has_prev)
            put(hi_rows, _add_pos(xn_ref[0], er_ref[pl.ds(jnp.minimum((i + 1) * gr, n_er - 1), 1), :],
                                  ec_ref[:HALO, :]), has_next)
        else:
            put(lo_rows, xp_ref[0], has_prev)
            put(slice(HALO, HALO + tm), x_ref[0])
            put(hi_rows, xn_ref[0], has_next)

    n_rows = tm + 2 * HALO
    half = w_ref.shape[1] // 2
    for h in range(2):
        cols = slice(h * half, (h + 1) * half)
        res = jnp.dot(a_scr[...], w_ref[:, cols], preferred_element_type=F32)
        out = (pltpu.roll(res, 1, 0) * cw_ref[0:1, cols] + res * cw_ref[1:2, cols]
               + pltpu.roll(res, n_rows - 1, 0) * cw_ref[2:3, cols] + cb_ref[:, cols])
        o_ref[0, :, cols] = out[HALO:HALO + tm].astype(o_ref.dtype)


def _inproj(x, er, ec, mod3, w, cw, cb, *, tm, tn, add_pos, ctx_row, col_tiles=None):
    bsz, lx, d = x.shape
    if col_tiles is None:
        n = w.shape[1]
        col = lambda j: j
    else:
        n = len(col_tiles) * tn
        run = col_tiles[:-1]
        assert run == list(range(len(run)))
        col = lambda j: jnp.where(j < len(run), j, col_tiles[-1])
    hb = tm // HALO
    n_hb = lx // HALO
    row = (lambda b: ctx_row) if ctx_row is not None else (lambda b: b)
    return pl.pallas_call(
        functools.partial(_inproj_kernel, add_pos=add_pos),
        grid=(bsz, lx // tm, n // tn),
        in_specs=[pl.BlockSpec((1, tm, d), lambda b, i, j: (b, i, 0)),
                  pl.BlockSpec((1, HALO, d), lambda b, i, j: (b, jnp.maximum(i * hb - 1, 0), 0)),
                  pl.BlockSpec((1, HALO, d), lambda b, i, j: (b, jnp.minimum((i + 1) * hb, n_hb - 1), 0)),
                  pl.BlockSpec(er.shape, lambda b, i, j: (0, 0)),
                  pl.BlockSpec(ec.shape, lambda b, i, j: (0, 0)),
                  pl.BlockSpec((1, 1, d), lambda b, i, j: (row(b), 0, 0)),
                  pl.BlockSpec((1, 1, d), lambda b, i, j: (row(b), 0, 1)),
                  pl.BlockSpec((d, tn), lambda b, i, j: (0, col(j))),
                  pl.BlockSpec((3, tn), lambda b, i, j: (0, col(j))),
                  pl.BlockSpec((1, tn), lambda b, i, j: (0, col(j)))],
        out_specs=pl.BlockSpec((1, tm, tn), lambda b, i, j: (b, i, j)),
        out_shape=jax.ShapeDtypeStruct((bsz, lx, n), BF16),
        scratch_shapes=[pltpu.VMEM((tm + 2 * HALO, d), BF16)],
        compiler_params=_cparams(("arbitrary", "arbitrary", "arbitrary")),
        name="inproj_pos" if add_pos else "inproj_ctx",
    )(x, x, x, er, ec, mod3, mod3, w, cw, cb)


def _head_sum(x, ones_bd):
    outs = []
    for j in range(x.shape[1] // LANE):
        outs.append(_dot_exact_rhs(x[:, j * LANE:(j + 1) * LANE], ones_bd))
    return jnp.concatenate(outs, axis=1)


def _prep_kernel(p_rkv, p_lo, pc_rkv, pc_lo, wup, aup, gup, w0, a0, kk_w, ka_w, ones_ref,
                 r_out, v_out, kk_out, gate_out, lw_out, kh_out, b_out, *, n_lat):
    is_ctx = pl.program_id(1) == n_lat
    rw = kk_out.shape[2]
    ones_bd = ones_ref[...]
    wup_split = _split2(wup[...])
    aup_split = _split2(aup[...])
    gup_bf = gup[...].astype(BF16)

    def row_block(rows):
        def rkv(s):
            cols = slice(s * rw, (s + 1) * rw)
            return jnp.where(is_ctx, pc_rkv[0, rows, cols], p_rkv[0, rows, cols])

        r_out[0, rows, :] = rkv(0)
        v_out[0, rows, :] = rkv(2)
        k = rkv(1).astype(F32)
        kkr = k * kk_w[...]
        nrm = jnp.sqrt(_head_sum(kkr * kkr, ones_bd))
        kk = kkr / jnp.maximum(nrm, 1e-12)
        kk_out[0, rows, :] = kk.astype(kk_out.dtype)

        lo = jnp.where(is_ctx, pc_lo[0, rows, :], p_lo[0, rows, :]).astype(F32)
        tw = jnp.tanh(lo[:, 0:LANE])
        al = lo[:, LANE:2 * LANE]
        gl = _sigmoid(lo[:, 2 * LANE:])
        gate_out[0, rows, :] = _dot(gl, gup_bf).astype(gate_out.dtype)
        wx = _dot3(tw, None, wup_split) + w0[...]
        ax = _dot3(al, None, aup_split) + a0[...]
        for d in range(2):
            cols = slice(d * rw, (d + 1) * rw)
            lw_out[d, 0, rows, :] = -math.exp(-0.5) * _sigmoid(wx[:, cols])
            ia = _sigmoid(ax[:, cols])
            kh_out[d, 0, rows, :] = (k * (1.0 + (ia - 1.0) * ka_w[...])).astype(kh_out.dtype)
            b_out[d, 0, rows, :] = (kk * ia).astype(b_out.dtype)

    row_block(slice(0, kk_out.shape[1]))


def _rwkv_prep(p, pc, wup, aup, gup, w0, a0, k_k, k_a, *, rw, rkv_cols, lo_cols):
    bsz, lq, _ = p.shape
    ctx = pc.shape[1]
    tr = ctx
    n_lat = lq // tr
    t_all = lq + ctx
    lo_w = lo_cols[1] - lo_cols[0]
    lo_blk = lo_cols[0] // lo_w
    lo_blk_c = rkv_cols // lo_w
    lat = lambda i: jnp.minimum(i, n_lat - 1)
    ones_bd = (np.arange(LANE)[:, None] // HEAD_DIM == np.arange(LANE)[None, :] // HEAD_DIM)
    ones_bd = jnp.asarray(ones_bd, BF16)
    full = lambda a: pl.BlockSpec(a.shape, lambda b, i: (0,) * a.ndim)
    row = lambda a: a.reshape(1, -1)
    weights = [wup, aup, gup, row(w0), row(a0), row(k_k), row(k_a), ones_bd]
    shared = jax.ShapeDtypeStruct((bsz, t_all, rw), BF16)
    perdir = jax.ShapeDtypeStruct((2, bsz, t_all, rw), BF16)
    perdir_f32 = jax.ShapeDtypeStruct((2, bsz, t_all, rw), F32)
    o_shared = pl.BlockSpec((1, tr, rw), lambda b, i: (b, i, 0))
    o_dir = pl.BlockSpec((2, 1, tr, rw), lambda b, i: (0, b, i, 0))
    return pl.pallas_call(
        functools.partial(_prep_kernel, n_lat=n_lat),
        grid=(bsz, n_lat + 1),
        in_specs=[pl.BlockSpec((1, tr, rkv_cols), lambda b, i: (b, lat(i), 0)),
                  pl.BlockSpec((1, tr, lo_w), lambda b, i: (b, lat(i), lo_blk)),
                  pl.BlockSpec((1, tr, rkv_cols), lambda b, i: (b, 0, 0)),
                  pl.BlockSpec((1, tr, lo_w), lambda b, i: (b, 0, lo_blk_c))]
                 + [full(a) for a in weights],
        out_specs=[o_shared, o_shared, o_shared, o_shared, o_dir, o_dir, o_dir],
        out_shape=[shared, shared, shared, shared, perdir_f32, perdir, perdir],
        compiler_params=_cparams(("arbitrary", "arbitrary")),
        name="rwkv_prep",
    )(p, p, pc, pc, *weights)


def _scan_kernel(rf_ref, vf_ref, kkf_ref, lwf_ref, khf_ref, bf_ref, rb_ref, vb_ref, kkb_ref, lwb_ref, khb_ref,
                 bb_ref, rk_ref, gng_ref, gnb_ref, ones_ref, of_ref, ob_ref, s_scr):
    C = SCAN_CHUNK
    n_sub = rf_ref.shape[1] // C
    G = SCAN_GROUP
    W = G * HEAD_DIM
    n = G * C
    n_groups = rf_ref.shape[2] // W

    @pl.when(pl.program_id(1) == 0)
    def _():
        s_scr[...] = jnp.zeros_like(s_scr)

    bi = lax.broadcasted_iota(jnp.int32, (n, n), 0)
    bj = lax.broadcasted_iota(jnp.int32, (n, n), 1)
    same = (bi // C) == (bj // C)
    li = bi % C
    lj = bj % C
    eye = jnp.where(bi == bj, 1.0, 0.0)
    hrow = lax.broadcasted_iota(jnp.int32, (n, W), 0) // C
    hlane = lax.broadcasted_iota(jnp.int32, (n, W), 1) // HEAD_DIM
    hmask = hrow == hlane
    ti = lax.broadcasted_iota(jnp.int32, (C, C), 0)
    tj = lax.broadcasted_iota(jnp.int32, (C, C), 1)

    def rep(x):
        return jnp.where(hmask, jnp.concatenate([x] * G, axis=0), 0.0).astype(BF16)

    chains = []
    dirs = ((rf_ref, vf_ref, kkf_ref, lwf_ref, khf_ref, bf_ref, False),
            (rb_ref, vb_ref, kkb_ref, lwb_ref, khb_ref, bb_ref, True))
    per_chunk = {}
    for di, (r_ref, v_ref, kk_ref, lw_ref, kh_ref, b_ref, rev) in enumerate(dirs):
        before = (lj > li) if rev else (lj < li)
        strict = jnp.logical_and(same, before)
        incl = jnp.logical_and(same, jnp.logical_or(before, li == lj))
        tri = jnp.where((tj >= ti) if rev else (tj <= ti), 1.0, 0.0).astype(BF16)
        for t in range(n_sub):
            sub = n_sub - 1 - t if rev else t
            rows = slice(sub * C, (sub + 1) * C)
            r = r_ref[0, rows, :].astype(F32)
            v = v_ref[0, rows, :].astype(F32)
            kk = kk_ref[0, rows, :].astype(F32)
            lw = lw_ref[0, 0, rows, :]
            kh = kh_ref[0, 0, rows, :].astype(F32)
            bv = b_ref[0, 0, rows, :].astype(F32)
            cum = _dot_exact_lhs(tri, lw)
            tot = cum[0:1, :] if rev else cum[C - 1:C, :]
            e_neg = jnp.exp(-cum)
            e_end = jnp.exp(tot - cum)
            rt = r * jnp.exp(cum)
            at = -kk * jnp.exp(cum - lw)
            bt = bv * e_neg
            kt = kh * e_neg
            bp = bv * e_end
            kp = kh * e_end
            per_chunk[(di, t)] = (r, v, kh, jnp.exp(tot), rows)
            for g in range(n_groups):
                cols = slice(g * W, (g + 1) * W)
                chains.append(dict(
                    di=di, t=t, g=g, cols=cols, strict=strict, incl=incl,
                    v2=rep(v[:, cols]),
                    ar=jnp.concatenate([rep(at[:, cols]), rep(rt[:, cols])], axis=0),
                    bk=jnp.concatenate([rep(bt[:, cols]), rep(kt[:, cols])], axis=0),
                    bkp=jnp.concatenate([rep(bp[:, cols]), rep(kp[:, cols])], axis=0)))

    for ch in chains:
        ch["tt"] = _dot_nt(ch["ar"], ch["bk"])
    for ch in chains:
        ch["p"] = jnp.where(ch["strict"], ch["tt"][:n, :n], 0.0).astype(BF16)
        ch["tinv"] = eye + ch["p"].astype(F32)
    step = 2
    while step < C:
        for ch in chains:
            ch["p"] = _dot(ch["p"], ch["p"]).astype(BF16)
        for ch in chains:
            ch["tinv"] = ch["tinv"] + _dot(ch["p"], ch["tinv"])
        step *= 2
    state = {(di, g): s_scr[di, g] for di in range(2) for g in range(n_groups)}
    for t in range(n_sub):
        cur = [ch for ch in chains if ch["t"] == t]
        for ch in cur:
            ch["s0"] = state[(ch["di"], ch["g"])]
            ch["xs"] = _dot_nt(ch["ar"], ch["s0"])
        for ch in cur:
            a_ak = jnp.where(ch["strict"], ch["tt"][:n, n:], 0.0)
            ch["rhs"] = ch["xs"][:n] + _dot(a_ak, ch["v2"])
        for ch in cur:
            u = _dot(ch["tinv"], ch["rhs"])
            ch["uv"] = jnp.concatenate([u.astype(BF16), ch["v2"]], axis=0)
        for ch in cur:
            e_tot = per_chunk[(ch["di"], t)][3]
            state[(ch["di"], ch["g"])] = ch["s0"] * e_tot[:, ch["cols"]] + _dot_tn(ch["uv"], ch["bkp"])
        for ch in cur:
            a_r = jnp.concatenate([jnp.where(ch["incl"], ch["tt"][n:, :n], 0.0),
                                   jnp.where(ch["incl"], ch["tt"][n:, n:], 0.0)], axis=1)
            ch["y2"] = ch["xs"][n:] + _dot(a_r, ch["uv"])
    for (di, g), s_new in state.items():
        s_scr[di, g] = s_new

    ones_bd = ones_ref[...]
    inv = 1.0 / HEAD_DIM
    for di, o_ref in enumerate((of_ref, ob_ref)):
        for t in range(n_sub):
            r, v, kh, _, rows = per_chunk[(di, t)]
            ys = []
            for ch in chains:
                if ch["di"] == di and ch["t"] == t:
                    y = ch["y2"][0:C]
                    for h in range(1, G):
                        y = y + ch["y2"][h * C:(h + 1) * C]
                    ys.append(y)
            y = jnp.concatenate(ys, axis=1)
            mu = _head_sum(y, ones_bd) * inv
            yc = y - mu
            var = _head_sum(yc * yc, ones_bd) * inv
            gn = yc * lax.rsqrt(var + GN_EPS) * gng_ref[...] + gnb_ref[...]
            bonus = _head_sum(r * kh * rk_ref[...], ones_bd) * v
            o_ref[0, rows, :] = gn + bonus


def _rwkv_scan(r, v, kk, lw, kh, bvec, r_k, gn_g, gn_b, *, n_lat_tokens):
    bsz, t_all, rw = r.shape
    C = SCAN_BLOCK
    n_chunks = t_all // C
    n_ctx = (t_all - n_lat_tokens) // C
    n_lat = n_lat_tokens // C
    W = SCAN_GROUP * HEAD_DIM

    blk_f = lambda c: jnp.where(c < n_ctx, n_lat + c, c - n_ctx)
    blk_b = lambda c: n_chunks - 1 - c
    ones_bd = (np.arange(LANE)[:, None] // HEAD_DIM == np.arange(LANE)[None, :] // HEAD_DIM)
    ones_bd = jnp.asarray(ones_bd, BF16)
    shared = lambda blk: pl.BlockSpec((1, C, rw), lambda b, c: (b, blk(c), 0))
    perdir = lambda d, blk: pl.BlockSpec((1, 1, C, rw), lambda b, c: (d, b, blk(c), 0))
    vec = pl.BlockSpec((1, rw), lambda b, c: (0, 0))
    out = jax.ShapeDtypeStruct((bsz, t_all, rw), F32)
    return pl.pallas_call(
        _scan_kernel,
        grid=(bsz, n_chunks),
        in_specs=[shared(blk_f), shared(blk_f), shared(blk_f), perdir(0, blk_f), perdir(0, blk_f), perdir(0, blk_f),
                  shared(blk_b), shared(blk_b), shared(blk_b), perdir(1, blk_b), perdir(1, blk_b), perdir(1, blk_b),
                  vec, vec, vec, pl.BlockSpec((LANE, LANE), lambda b, c: (0, 0))],
        out_specs=[shared(blk_f), shared(blk_b)],
        out_shape=[out, out],
        scratch_shapes=[pltpu.VMEM((2, rw // W, W, W), F32)],
        compiler_params=_cparams(("arbitrary", "arbitrary")),
        name="rwkv_scan",
    )(r, v, kk, lw, kh, bvec, r, v, kk, lw, kh, bvec,
      r_k.reshape(1, rw), gn_g.reshape(1, rw), gn_b.reshape(1, rw), ones_bd)


def _fft_tables(L):
    n2 = FFT_N2
    n_all = 2 * L
    n1 = n_all // n2
    nh = n1 // 2
    two_pi = 2.0 * np.pi
    kh = -(-(nh + 1) // FFT_UNROLL) * FFT_UNROLL
    k1 = np.arange(kh)
    pair_w = np.where(k1 > nh, 0.0, np.where((k1 == 0) | (k1 == nh), 1.0, 2.0))
    th1 = two_pi * np.outer(k1, np.arange(nh)) / n1
    f1 = np.concatenate([np.cos(th1), -np.sin(th1)], 0)
    th2 = two_pi * np.outer(np.arange(n2), np.arange(n2)) / n2
    fr, fi = np.cos(th2), -np.sin(th2)
    a3 = np.block([[fr, -fi], [fi, fr]])
    b3 = np.block([[-fi, -fr], [fr, -fi]])
    tht = two_pi * np.outer(k1, np.arange(n2)) / n_all
    twr = np.tile(np.cos(tht), (1, 2))
    twi = np.tile(-np.sin(tht), (1, 2))
    gr, gi = np.cos(th2), np.sin(th2)
    m3i = np.block([[gr, -gi], [gi, gr]]) / n_all
    th4 = two_pi * np.outer(np.arange(nh), k1) / n1
    cos4, sin4 = np.cos(th4) * pair_w, np.sin(th4) * pair_w
    ca = np.concatenate([cos4, -sin4], 1)
    cb = np.concatenate([-sin4, -cos4], 1)
    th5 = two_pi * np.outer(np.arange(n2), k1) / n_all
    t2r, t2i = np.tile(np.cos(th5), (1, 2)), np.tile(np.sin(th5), (1, 2))
    f = lambda a: jnp.asarray(a, F32)
    return dict(f1=f(f1), a3=f(a3), b3=f(b3), twr=f(twr), twi=f(twi),
                m3i=f(m3i), ca=f(ca), cb=f(cb), t2r=f(t2r), t2i=f(t2i))


def _fft_stage1(src_ref, f1_ref, s_ref):
    kh2, nh = f1_ref.shape
    n2 = FFT_N2
    f1 = f1_ref[...].astype(BF16)

    def body(nlo, carry):
        xin = src_ref[pl.ds(nlo, nh, stride=n2), :]
        s_ref[pl.ds(pl.multiple_of(nlo * kh2, SUB), kh2), :] = _dot(f1, xin)
        return carry

    lax.fori_loop(0, n2, body, 0, unroll=FFT_UNROLL)


def _fft_mid_matrix(k1, a3_ref, b3_ref, twr_ref, twi_ref):
    tr = twr_ref[pl.ds(k1, 1), :]
    ti = twi_ref[pl.ds(k1, 1), :]
    return (a3_ref[...] * tr + b3_ref[...] * ti).astype(BF16)


def _fft_load_k1(s_ref, k1, kh2):
    n2 = FFT_N2
    re = s_ref[pl.ds(k1, n2, stride=kh2), :]
    im = s_ref[pl.ds(kh2 // 2 + k1, n2, stride=kh2), :]
    return jnp.concatenate([re, im], axis=0)


def _hyena_kernel(x0_ref, x1_ref, v_ref, bias_ref, kf_ref,
                  f1_ref, a3_ref, b3_ref, twr_ref, twi_ref, m3i_ref, ca_ref, cb_ref, t2r_ref, t2i_ref,
                  o_ref, z_scr, y_scr, s_scr, q_scr):
    n2 = FFT_N2
    kh2, nh = f1_ref.shape
    kh = kh2 // 2
    z_scr[...] = v_ref[0].astype(F32) * x1_ref[0].astype(F32)
    _fft_stage1(z_scr, f1_ref, s_scr)
    m3i = m3i_ref[...].astype(BF16)

    def mid(k1, carry):
        m = _fft_mid_matrix(k1, a3_ref, b3_ref, twr_ref, twi_ref)
        cc = _dot(m, _fft_load_k1(s_scr, k1, kh2))
        kf = kf_ref[0, k1]
        cr, ci = cc[:n2], cc[n2:]
        kr, ki = kf[:n2], kf[n2:]
        pp = jnp.concatenate([cr * kr - ci * ki, cr * ki + ci * kr], axis=0)
        q = _dot(m3i, pp)
        q_scr[pl.ds(k1, n2, stride=kh2), :] = q[:n2]
        q_scr[pl.ds(kh + k1, n2, stride=kh2), :] = q[n2:]
        return carry

    lax.fori_loop(0, kh, mid, 0, unroll=FFT_UNROLL)

    def last(m2, carry):
        f4 = ca_ref[...] * t2r_ref[pl.ds(m2, 1), :] + cb_ref[...] * t2i_ref[pl.ds(m2, 1), :]
        qm = q_scr[pl.ds(pl.multiple_of(m2 * kh2, SUB), kh2), :]
        y_scr[pl.ds(m2, nh, stride=n2), :] = _dot(f4, qm)
        return carry

    lax.fori_loop(0, n2, last, 0, unroll=FFT_UNROLL)
    z = z_scr[...]
    o_ref[0] = (y_scr[...] + z * bias_ref[...]) * x0_ref[0].astype(F32)


def _hyena_conv(p, hy_bias, kf, tabs, *, hy_col0, hy_dim, ct):
    bsz, L, _ = p.shape
    n_tiles = hy_dim // ct
    blk0 = hy_col0 // ct
    sec = hy_dim // ct
    xspec = lambda s: pl.BlockSpec((1, L, ct), lambda t, b: (b, 0, blk0 + s * sec + t))
    tab_list = [tabs[k] for k in ("f1", "a3", "b3", "twr", "twi", "m3i", "ca", "cb", "t2r", "t2i")]
    full = lambda a: pl.BlockSpec(a.shape, lambda t, b: (0,) * a.ndim)
    kh2 = tabs["f1"].shape[0]
    return pl.pallas_call(
        _hyena_kernel,
        grid=(n_tiles, bsz),
        in_specs=[xspec(0), xspec(1), xspec(2),
                  pl.BlockSpec((1, ct), lambda t, b: (0, t)),
                  pl.BlockSpec((1,) + kf.shape[1:], lambda t, b: (t, 0, 0, 0), pipeline_mode=pl.Buffered(1))]
                 + [full(a) for a in tab_list],
        out_specs=pl.BlockSpec((1, L, ct), lambda t, b: (b, 0, t)),
        out_shape=jax.ShapeDtypeStruct((bsz, L, hy_dim), F32),
        scratch_shapes=[pltpu.VMEM((L, ct), F32), pltpu.VMEM((L, ct), F32),
                        pltpu.VMEM((FFT_N2 * kh2, ct), F32), pltpu.VMEM((FFT_N2 * kh2, ct), F32)],
        compiler_params=_cparams(("arbitrary", "arbitrary")),
        name="hyena_conv",
    )(p, p, p, hy_bias.reshape(1, -1), kf, *tab_list)


def _filter_kernel(z_ref, t_ref, w1_ref, b1_ref, w2_ref, b2_ref, w3f_ref, w3b_ref, freq_ref, delta_ref,
                   f1_ref, a3_ref, b3_ref, twr_ref, twi_ref, kf_ref, h_scr, hf_scr, hb_scr, sf_scr, sb_scr):
    n2 = FFT_N2
    kh2 = f1_ref.shape[0]

    @pl.when(pl.program_id(0) == 0)
    def _():
        freq = freq_ref[...]
        h1 = jnp.sin(freq * (_dot3(z_ref[...], w1_ref[...]) + b1_ref[...]))
        h_scr[...] = jnp.sin(freq * (_dot3(h1, w2_ref[...]) + b2_ref[...]))

    h = h_scr[...]
    window = jnp.exp(-t_ref[...] * delta_ref[...])
    hf_scr[...] = _dot3(h, w3f_ref[...]) * window
    hb = _dot3(h, w3b_ref[...]) * window
    rows = lax.broadcasted_iota(jnp.int32, hb.shape, 0)
    hb_scr[...] = jnp.where(rows == 0, 0.0, hb)
    _fft_stage1(hf_scr, f1_ref, sf_scr)
    _fft_stage1(hb_scr, f1_ref, sb_scr)

    def mid(k1, carry):
        m = _fft_mid_matrix(k1, a3_ref, b3_ref, twr_ref, twi_ref)
        cf = _dot(m, _fft_load_k1(sf_scr, k1, kh2))
        cb = _dot(m, _fft_load_k1(sb_scr, k1, kh2))
        kf_ref[0, k1] = jnp.concatenate([cf[:n2] + cb[:n2], cf[n2:] - cb[n2:]], axis=0)
        return carry

    lax.fori_loop(0, kh2 // 2, mid, 0, unroll=FFT_UNROLL)


def _filter_spectrum(zpos, tcol, w1, b1, w2, b2, w3, freq, deltas, tabs, *, ct):
    L = zpos.shape[0]
    hy_dim = deltas.shape[1]
    n_tiles = hy_dim // ct
    kh2 = tabs["f1"].shape[0]
    kh = kh2 // 2
    tab_list = [tabs[k] for k in ("f1", "a3", "b3", "twr", "twi")]
    full = lambda a: pl.BlockSpec(a.shape, lambda t: (0,) * a.ndim)
    row = lambda a: a.reshape(1, -1)
    small = [zpos, tcol, w1, row(b1), w2, row(b2)]
    return pl.pallas_call(
        _filter_kernel,
        grid=(n_tiles,),
        in_specs=[full(a) for a in small]
                 + [pl.BlockSpec((w3.shape[0], ct), lambda t: (0, t)),
                    pl.BlockSpec((w3.shape[0], ct), lambda t: (0, n_tiles + t)),
                    full(row(freq)),
                    pl.BlockSpec((1, ct), lambda t: (0, t))]
                 + [full(a) for a in tab_list],
        out_specs=pl.BlockSpec((1, kh, 2 * FFT_N2, ct), lambda t: (t, 0, 0, 0)),
        out_shape=jax.ShapeDtypeStruct((n_tiles, kh, 2 * FFT_N2, ct), F32),
        scratch_shapes=[pltpu.VMEM((L, w2.shape[1]), F32), pltpu.VMEM((L, ct), F32), pltpu.VMEM((L, ct), F32),
                        pltpu.VMEM((FFT_N2 * kh2, ct), F32), pltpu.VMEM((FFT_N2 * kh2, ct), F32)],
        compiler_params=_cparams(("arbitrary",)),
        name="hyena_filter",
    )(*small, w3, w3, row(freq), deltas, *tab_list)


def _outproj_kernel(of_ref, ob_ref, gate_ref, ohy_ref, x_ref, er_ref, ec_ref, g1_ref, w_ref, lng_ref, lnb_ref,
                    o_ref):
    rw = of_ref.shape[2]
    a_rw = ((of_ref[0] + ob_ref[0]) * gate_ref[0].astype(F32)).astype(BF16)
    mix = jnp.dot(a_rw, w_ref[0:rw, :], preferred_element_type=F32)
    mix = mix + jnp.dot(ohy_ref[0].astype(BF16), w_ref[rw:, :], preferred_element_type=F32)
    gmix = g1_ref[0] * mix
    tm = x_ref.shape[1]
    for q in range(tm // GRID_W):
        rows = slice(q * GRID_W, (q + 1) * GRID_W)
        xv = _add_pos(x_ref[0, rows, :], er_ref[q:q + 1, :], ec_ref[...])
        o_ref[0, rows, :] = _layer_norm(ALPHA * xv + gmix[rows], lng_ref[...], lnb_ref[...])


def _outproj(o_f, o_b, gate, o_hy, x, er, ec, mod3, w_out, ln_g, ln_b, *, tm):
    bsz, L, d = x.shape
    rw = gate.shape[2]
    hy = o_hy.shape[2]
    gr = tm // GRID_W
    return pl.pallas_call(
        _outproj_kernel,
        grid=(bsz, L // tm),
        in_specs=[pl.BlockSpec((1, tm, rw), lambda b, i: (b, i, 0)),
                  pl.BlockSpec((1, tm, rw), lambda b, i: (b, i, 0)),
                  pl.BlockSpec((1, tm, rw), lambda b, i: (b, i, 0)),
                  pl.BlockSpec((1, tm, hy), lambda b, i: (b, i, 0)),
                  pl.BlockSpec((1, tm, d), lambda b, i: (b, i, 0)),
                  pl.BlockSpec((gr, d // 2), lambda b, i: (i, 0)),
                  pl.BlockSpec((GRID_W, d // 2), lambda b, i: (0, 0)),
                  pl.BlockSpec((1, 1, d), lambda b, i: (b, 0, 2)),
                  pl.BlockSpec((d, d), lambda b, i: (0, 0)),
                  pl.BlockSpec((1, d), lambda b, i: (0, 0)),
                  pl.BlockSpec((1, d), lambda b, i: (0, 0))],
        out_specs=pl.BlockSpec((1, tm, d), lambda b, i: (b, i, 0)),
        out_shape=jax.ShapeDtypeStruct((bsz, L, d), F32),
        compiler_params=_cparams(("arbitrary", "arbitrary")),
        name="outproj_ln",
    )(o_f, o_b, gate, o_hy, x, er, ec, mod3, w_out, ln_g.reshape(1, d), ln_b.reshape(1, d))


def _ffn_kernel(x_ref, sh_ref, sc_ref, g2_ref, w1_ref, w3_ref, w2_ref, lng_ref, lnb_ref, o_ref, a_scr, acc_scr):
    f = pl.program_id(2)

    @pl.when(f == 0)
    def _():
        a_scr[...] = (x_ref[0] * (1.0 + sc_ref[0]) + sh_ref[0]).astype(BF16)
        acc_scr[...] = jnp.zeros_like(acc_scr)

    a = a_scr[...]
    h1 = jnp.dot(a, w1_ref[...], preferred_element_type=F32)
    h3 = jnp.dot(a, w3_ref[...], preferred_element_type=F32)
    h = (h1 * _sigmoid(h1) * h3).astype(BF16)
    acc_scr[...] += jnp.dot(h, w2_ref[...], preferred_element_type=F32)

    @pl.when(f == pl.num_programs(2) - 1)
    def _():
        o_ref[0] = _layer_norm(ALPHA * x_ref[0] + g2_ref[0] * acc_scr[...], lng_ref[...], lnb_ref[...])


def _ffn(x1, mod3, w1, w3, w2, ln_g, ln_b, *, tm, tf):
    bsz, L, d = x1.shape
    dff = w1.shape[1]
    return pl.pallas_call(
        _ffn_kernel,
        grid=(bsz, L // tm, dff // tf),
        in_specs=[pl.BlockSpec((1, tm, d), lambda b, i, f: (b, i, 0)),
                  pl.BlockSpec((1, 1, d), lambda b, i, f: (b, 0, 3)),
                  pl.BlockSpec((1, 1, d), lambda b, i, f: (b, 0, 4)),
                  pl.BlockSpec((1, 1, d), lambda b, i, f: (b, 0, 5)),
                  pl.BlockSpec((d, tf), lambda b, i, f: (0, f)),
                  pl.BlockSpec((d, tf), lambda b, i, f: (0, f)),
                  pl.BlockSpec((tf, d), lambda b, i, f: (f, 0)),
                  pl.BlockSpec((1, d), lambda b, i, f: (0, 0)),
                  pl.BlockSpec((1, d), lambda b, i, f: (0, 0))],
        out_specs=pl.BlockSpec((1, tm, d), lambda b, i, f: (b, i, 0)),
        out_shape=jax.ShapeDtypeStruct((bsz, L, d), F32),
        scratch_shapes=[pltpu.VMEM((tm, d), BF16), pltpu.VMEM((tm, d), F32)],
        compiler_params=_cparams(("arbitrary", "arbitrary", "arbitrary")),
        name="ffn_ln",
    )(x1, mod3, mod3, mod3, w1, w3, w2, ln_g.reshape(1, d), ln_b.reshape(1, d))


def _pos_tables(rows, d):
    quarter = d // 4
    omega = 1.0 / (10000.0 ** (jnp.arange(quarter, dtype=F32) / quarter))
    er = jnp.arange(rows, dtype=F32)[:, None] * omega
    ec = jnp.arange(GRID_W, dtype=F32)[:, None] * omega
    er = jnp.concatenate([jnp.sin(er), jnp.cos(er)], -1)
    ec = jnp.concatenate([jnp.sin(ec), jnp.cos(ec)], -1)
    return er, ec


def _filter_positions(L):
    pos = jnp.arange(L, dtype=F32)[:, None]
    t = jnp.linspace(0.0, 1.0, L, dtype=F32)[:, None]
    bands = jnp.linspace(1e-4, N_BANDS - 1, N_BANDS, dtype=F32)[None, :]
    ang = 2.0 * math.pi * bands * pos / L
    z = jnp.concatenate([t, jnp.cos(ang), -jnp.sin(ang)], -1)
    return z, t


def _pad_cols(a, width):
    return jnp.pad(a, ((0, 0), (0, width - a.shape[1])))


def _pad_rows(a, height):
    return jnp.pad(a, ((0, height - a.shape[0]), (0, 0)))


def kernel(x, c, ctx, c_ctx, w_ada, b_ada, w_in, conv_rw, conv_hy, conv_hy_b, w0_f, w_up_f, a0_f, a_up_f, w0_b, w_up_b, a0_b, a_up_b, k_k, k_a, r_k, g_up, gn_g, gn_b, filt_w1, filt_b1, filt_w2, filt_b2, filt_w3, sin_freq, hy_bias, w_out, ln1_g, ln1_b, ffn_w1, ffn_w3, ffn_w2, ln2_g, ln2_b):
    bsz, L, d = x.shape
    rw = k_k.shape[1]
    hy = hy_bias.shape[1]
    dl, il, gl = w_up_f.shape[1], a_up_f.shape[1], g_up.shape[1]
    lo_w = 4 * LANE
    rkv_cols = 3 * rw

    wi = w_in[0]
    lo0 = rkv_cols

    def lora_layout(a):
        return jnp.concatenate([_pad_cols(a[:, lo0:lo0 + dl], LANE),
                                _pad_cols(a[:, lo0 + dl:lo0 + dl + il], LANE),
                                _pad_cols(a[:, lo0 + dl + il:lo0 + dl + il + gl], 2 * LANE)], axis=1)

    hy0 = lo0 + dl + il + gl
    w_lat = jnp.concatenate([wi[:, :rkv_cols], wi[:, hy0:], lora_layout(wi)], axis=1).astype(BF16)
    cw_lat = jnp.concatenate([conv_rw[0][:, :rkv_cols], conv_hy[0], lora_layout(conv_rw[0])], axis=1)
    cb_lat = jnp.concatenate([jnp.zeros((rkv_cols,), F32), conv_hy_b[0], jnp.zeros((lo_w,), F32)])[None, :]
    wup = _pad_rows(jnp.concatenate([w_up_f[0], w_up_b[0]], axis=1), LANE)
    aup = _pad_rows(jnp.concatenate([a_up_f[0], a_up_b[0]], axis=1), LANE)
    gup = _pad_rows(g_up[0], 2 * LANE)
    w0 = jnp.concatenate([w0_f[0], w0_b[0]])
    a0 = jnp.concatenate([a0_f[0], a0_b[0]])

    cc = _pad_rows(jnp.concatenate([c, c_ctx[None, :]], axis=0), SUB)
    mod = _adaln(cc, w_ada[0], b_ada[0])
    mod3 = mod.reshape(SUB, 1, 6 * d)

    er, ec = _pos_tables(L // GRID_W, d)

    tn = lo_w
    p_lat = _inproj(x, er, ec, mod3, w_lat, cw_lat, cb_lat, tm=1024, tn=tn, add_pos=True, ctx_row=None)
    ctx_tiles = list(range(rkv_cols // tn)) + [(rkv_cols + 3 * hy) // tn]
    p_ctx = _inproj(ctx, er, ec, mod3, w_lat, cw_lat, cb_lat, tm=ctx.shape[1], tn=tn, add_pos=False,
                    ctx_row=bsz, col_tiles=ctx_tiles)

    lo_cols = (rkv_cols + 3 * hy, rkv_cols + 3 * hy + lo_w)
    r, v, kk, gate, lw, kh, bvec = _rwkv_prep(p_lat, p_ctx, wup, aup, gup, w0, a0,
                                              k_k[0], k_a[0], rw=rw, rkv_cols=rkv_cols, lo_cols=lo_cols)
    o_f, o_b = _rwkv_scan(r, v, kk, lw, kh, bvec, r_k[0], gn_g[0], gn_b[0], n_lat_tokens=L)

    tabs = _fft_tables(L)
    zpos, tcol = _filter_positions(L)
    max_decay = math.log(DECAY_TARGET) / FAST_DECAY_PCT
    min_decay = math.log(DECAY_TARGET) / SLOW_DECAY_PCT
    deltas = jnp.abs(jnp.linspace(min_decay, max_decay, hy, dtype=F32))[None, :]
    kf = _filter_spectrum(_pad_cols(zpos, LANE), tcol, _pad_rows(filt_w1[0], LANE), filt_b1[0], filt_w2[0],
                          filt_b2[0], filt_w3[0], sin_freq[0], deltas, tabs, ct=LANE)
    o_hy = _hyena_conv(p_lat, hy_bias[0], kf, tabs, hy_col0=rkv_cols, hy_dim=hy, ct=LANE)

    x1 = _outproj(o_f, o_b, gate, o_hy, x, er, ec, mod3, w_out[0].astype(BF16), ln1_g[0], ln1_b[0], tm=512)
    return _ffn(x1, mod3, ffn_w1[0].astype(BF16), ffn_w3[0].astype(BF16), ffn_w2[0].astype(BF16),
                ln2_g[0], ln2_b[0], tm=512, tf=512)
```

```python
import functools
import math

import numpy as np
import jax
import jax.numpy as jnp
from jax import lax
from jax.experimental import pallas as pl
from jax.experimental.pallas import tpu as pltpu

F32 = jnp.float32
BF16 = jnp.bfloat16

GRID_W = 64
HEAD_DIM = 64
N_BANDS = 16
DECAY_TARGET = 1e-2
FAST_DECAY_PCT = 0.3
SLOW_DECAY_PCT = 1.5
LN_EPS = 1e-5
GN_EPS = 64e-5
ALPHA = 2.0 ** 0.25

LANE = 128
SUB = 8
VMEM_LIMIT = 56 * 1024 * 1024

SCAN_CHUNK = 64
SCAN_GROUP = 2
SCAN_BLOCK = 128
FFN_LN_ROWS = 256
FFT_N2 = 64
FFT_UNROLL = 8


def _cparams(sem):
    return pltpu.CompilerParams(dimension_semantics=sem, vmem_limit_bytes=VMEM_LIMIT)


def _dot(a, b, dims=(((1,), (0,)), ((), ()))):
    return lax.dot_general(a.astype(BF16), b.astype(BF16), dims, preferred_element_type=F32)


def _dot_nt(a, b):
    return _dot(a, b, (((1,), (1,)), ((), ())))


def _dot_tn(a, b):
    return _dot(a, b, (((0,), (0,)), ((), ())))


def _split2(a):
    hi = a.astype(BF16)
    lo = (a - hi.astype(F32)).astype(BF16)
    return hi, lo


def _dot3(a, b, b_split=None):
    ah, al = _split2(a)
    bh, bl = _split2(b) if b_split is None else b_split
    return _dot(ah, bh) + (_dot(al, bh) + _dot(ah, bl))


def _dot_exact_rhs(a, b_exact):
    a1 = a.astype(BF16)
    r1 = a - a1.astype(F32)
    a2 = r1.astype(BF16)
    a3 = (r1 - a2.astype(F32)).astype(BF16)
    return _dot(a1, b_exact) + (_dot(a2, b_exact) + _dot(a3, b_exact))


def _dot_exact_lhs(a_exact, b):
    b1 = b.astype(BF16)
    r1 = b - b1.astype(F32)
    b2 = r1.astype(BF16)
    b3 = (r1 - b2.astype(F32)).astype(BF16)
    return _dot(a_exact, b1) + (_dot(a_exact, b2) + _dot(a_exact, b3))


def _sigmoid(x):
    return 0.5 * jnp.tanh(0.5 * x) + 0.5


def _layer_norm(h, g, b):
    mu = jnp.mean(h, -1, keepdims=True)
    hc = h - mu
    var = jnp.mean(hc * hc, -1, keepdims=True)
    return hc * lax.rsqrt(var + LN_EPS) * g + b


def _adaln_kernel(c_ref, w_ref, b_ref, o_ref):
    cv = c_ref[...]
    a = cv * _sigmoid(cv)
    o_ref[...] = _dot3(a, w_ref[...]) + b_ref[...]


def _adaln(cc, w, b, tn=1024):
    m, d = cc.shape
    n = w.shape[1]
    return pl.pallas_call(
        _adaln_kernel,
        grid=(n // tn,),
        in_specs=[pl.BlockSpec((m, d), lambda j: (0, 0)),
                  pl.BlockSpec((d, tn), lambda j: (0, j)),
                  pl.BlockSpec((1, tn), lambda j: (0, j))],
        out_specs=pl.BlockSpec((m, tn), lambda j: (0, j)),
        out_shape=jax.ShapeDtypeStruct((m, n), F32),
        compiler_params=_cparams(("arbitrary",)),
        name="adaln",
    )(cc, w, b.reshape(1, n))


HALO = 16


def _add_pos(xv, er_row, ec_rows):
    half = ec_rows.shape[1]
    return jnp.concatenate([xv[:, :half] + er_row, xv[:, half:] + ec_rows], axis=1)


def _inproj_kernel(x_ref, xp_ref, xn_ref, er_ref, ec_ref, sh_ref, sc_ref, w_ref, cw_ref, cb_ref, o_ref, a_scr,
                   *, add_pos):
    i = pl.program_id(1)
    n_i = pl.num_programs(1)
    tm = x_ref.shape[1]
    gr = tm // GRID_W

    @pl.when(pl.program_id(2) == 0)
    def _():
        scale = 1.0 + sc_ref[0]
        shift = sh_ref[0]

        def put(rows, xv, keep=None):
            a = xv * scale + shift
            if keep is not None:
                a = jnp.where(keep, a, 0.0)
            a_scr[rows, :] = a.astype(BF16)

        has_prev = i > 0
        has_next = i < n_i - 1
        lo_rows, hi_rows = slice(0, HALO), slice(HALO + tm, 2 * HALO + tm)
        if add_pos:
            n_er = er_ref.shape[0]
            for q in range(gr):
                rows = slice(q * GRID_W, (q + 1) * GRID_W)
                put(slice(HALO + q * GRID_W, HALO + (q + 1) * GRID_W),
                    _add_pos(x_ref[0, rows, :], er_ref[pl.ds(i * gr + q, 1), :], ec_ref[...]))
            put(lo_rows, _add_pos(xp_ref[0], er_ref[pl.ds(jnp.maximum(i * gr - 1, 0), 1), :],
                                  ec_ref[GRID_W - HALO:, :]), has_prev)
            put(hi_rows, _add_pos(xn_ref[0], er_ref[pl.ds(jnp.minimum((i + 1) * gr, n_er - 1), 1), :],
                                  ec_ref[:HALO, :]), has_next)
        else:
            put(lo_rows, xp_ref[0], has_prev)
            put(slice(HALO, HALO + tm), x_ref[0])
            put(hi_rows, xn_ref[0], has_next)

    n_rows = tm + 2 * HALO
    half = w_ref.shape[1] // 2
    for h in range(2):
        cols = slice(h * half, (h + 1) * half)
        res = jnp.dot(a_scr[...], w_ref[:, cols], preferred_element_type=F32)
        out = (pltpu.roll(res, 1, 0) * cw_ref[0:1, cols] + res * cw_ref[1:2, cols]
               + pltpu.roll(res, n_rows - 1, 0) * cw_ref[2:3, cols] + cb_ref[:, cols])
        o_ref[0, :, cols] = out[HALO:HALO + tm].astype(o_ref.dtype)


def _inproj(x, er, ec, mod3, w, cw, cb, *, tm, tn, add_pos, ctx_row, col_tiles=None):
    bsz, lx, d = x.shape
    if col_tiles is None:
        n = w.shape[1]
        col = lambda j: j
    else:
        n = len(col_tiles) * tn
        run = col_tiles[:-1]
        assert run == list(range(len(run)))
        col = lambda j: jnp.where(j < len(run), j, col_tiles[-1])
    hb = tm // HALO
    n_hb = lx // HALO
    row = (lambda b: ctx_row) if ctx_row is not None else (lambda b: b)
    return pl.pallas_call(
        functools.partial(_inproj_kernel, add_pos=add_pos),
        grid=(bsz, lx // tm, n // tn),
        in_specs=[pl.BlockSpec((1, tm, d), lambda b, i, j: (b, i, 0)),
                  pl.BlockSpec((1, HALO, d), lambda b, i, j: (b, jnp.maximum(i * hb - 1, 0), 0)),
                  pl.BlockSpec((1, HALO, d), lambda b, i, j: (b, jnp.minimum((i + 1) * hb, n_hb - 1), 0)),
                  pl.BlockSpec(er.shape, lambda b, i, j: (0, 0)),
                  pl.BlockSpec(ec.shape, lambda b, i, j: (0, 0)),
                  pl.BlockSpec((1, 1, d), lambda b, i, j: (row(b), 0, 0)),
                  pl.BlockSpec((1, 1, d), lambda b, i, j: (row(b), 0, 1)),
                  pl.BlockSpec((d, tn), lambda b, i, j: (0, col(j))),
                  pl.BlockSpec((3, tn), lambda b, i, j: (0, col(j))),
                  pl.BlockSpec((1, tn), lambda b, i, j: (0, col(j)))],
        out_specs=pl.BlockSpec((1, tm, tn), lambda b, i, j: (b, i, j)),
        out_shape=jax.ShapeDtypeStruct((bsz, lx, n), BF16),
        scratch_shapes=[pltpu.VMEM((tm + 2 * HALO, d), BF16)],
        compiler_params=_cparams(("arbitrary", "arbitrary", "arbitrary")),
        name="inproj_pos" if add_pos else "inproj_ctx",
    )(x, x, x, er, ec, mod3, mod3, w, cw, cb)


def _head_sum(x, ones_bd):
    outs = []
    for j in range(x.shape[1] // LANE):
        outs.append(_dot_exact_rhs(x[:, j * LANE:(j + 1) * LANE], ones_bd))
    return jnp.concatenate(outs, axis=1)


def _prep_kernel(p_rkv, p_lo, pc_rkv, pc_lo, wup, aup, gup, w0, a0, kk_w, ka_w, ones_ref,
                 r_out, v_out, kk_out, gate_out, lw_out, kh_out, b_out, *, n_lat):
    is_ctx = pl.program_id(1) == n_lat
    rw = kk_out.shape[2]
    ones_bd = ones_ref[...]
    wup_split = _split2(wup[...])
    aup_split = _split2(aup[...])
    gup_bf = gup[...].astype(BF16)

    def row_block(rows):
        def rkv(s):
            cols = slice(s * rw, (s + 1) * rw)
            return jnp.where(is_ctx, pc_rkv[0, rows, cols], p_rkv[0, rows, cols])

        r_out[0, rows, :] = rkv(0)
        v_out[0, rows, :] = rkv(2)
        k = rkv(1).astype(F32)
        kkr = k * kk_w[...]
        nrm = jnp.sqrt(_head_sum(kkr * kkr, ones_bd))
        kk = kkr / jnp.maximum(nrm, 1e-12)
        kk_out[0, rows, :] = kk.astype(kk_out.dtype)

        lo = jnp.where(is_ctx, pc_lo[0, rows, :], p_lo[0, rows, :]).astype(F32)
        tw = jnp.tanh(lo[:, 0:LANE])
        al = lo[:, LANE:2 * LANE]
        gl = _sigmoid(lo[:, 2 * LANE:])
        gate_out[0, rows, :] = _dot(gl, gup_bf).astype(gate_out.dtype)
        wx = _dot3(tw, None, wup_split) + w0[...]
        ax = _dot3(al, None, aup_split) + a0[...]
        for d in range(2):
            cols = slice(d * rw, (d + 1) * rw)
            lw_out[d, 0, rows, :] = -math.exp(-0.5) * _sigmoid(wx[:, cols])
            ia = _sigmoid(ax[:, cols])
            kh_out[d, 0, rows, :] = (k * (1.0 + (ia - 1.0) * ka_w[...])).astype(kh_out.dtype)
            b_out[d, 0, rows, :] = (kk * ia).astype(b_out.dtype)

    row_block(slice(0, kk_out.shape[1]))


def _rwkv_prep(p, pc, wup, aup, gup, w0, a0, k_k, k_a, *, rw, rkv_cols, lo_cols):
    bsz, lq, _ = p.shape
    ctx = pc.shape[1]
    tr = ctx
    n_lat = lq // tr
    t_all = lq + ctx
    lo_w = lo_cols[1] - lo_cols[0]
    lo_blk = lo_cols[0] // lo_w
    lo_blk_c = rkv_cols // lo_w
    lat = lambda i: jnp.minimum(i, n_lat - 1)
    ones_bd = (np.arange(LANE)[:, None] // HEAD_DIM == np.arange(LANE)[None, :] // HEAD_DIM)
    ones_bd = jnp.asarray(ones_bd, BF16)
    full = lambda a: pl.BlockSpec(a.shape, lambda b, i: (0,) * a.ndim)
    row = lambda a: a.reshape(1, -1)
    weights = [wup, aup, gup, row(w0), row(a0), row(k_k), row(k_a), ones_bd]
    shared = jax.ShapeDtypeStruct((bsz, t_all, rw), BF16)
    perdir = jax.ShapeDtypeStruct((2, bsz, t_all, rw), BF16)
    perdir_f32 = jax.ShapeDtypeStruct((2, bsz, t_all, rw), F32)
    o_shared = pl.BlockSpec((1, tr, rw), lambda b, i: (b, i, 0))
    o_dir = pl.BlockSpec((2, 1, tr, rw), lambda b, i: (0, b, i, 0))
    return pl.pallas_call(
        functools.partial(_prep_kernel, n_lat=n_lat),
        grid=(bsz, n_lat + 1),
        in_specs=[pl.BlockSpec((1, tr, rkv_cols), lambda b, i: (b, lat(i), 0)),
                  pl.BlockSpec((1, tr, lo_w), lambda b, i: (b, lat(i), lo_blk)),
                  pl.BlockSpec((1, tr, rkv_cols), lambda b, i: (b, 0, 0)),
                  pl.BlockSpec((1, tr, lo_w), lambda b, i: (b, 0, lo_blk_c))]
                 + [full(a) for a in weights],
        out_specs=[o_shared, o_shared, o_shared, o_shared, o_dir, o_dir, o_dir],
        out_shape=[shared, shared, shared, shared, perdir_f32, perdir, perdir],
        compiler_params=_cparams(("arbitrary", "arbitrary")),
        name="rwkv_prep",
    )(p, p, pc, pc, *weights)


def _scan_kernel(rf_ref, vf_ref, kkf_ref, lwf_ref, khf_ref, bf_ref, rb_ref, vb_ref, kkb_ref, lwb_ref, khb_ref,
                 bb_ref, rk_ref, gng_ref, gnb_ref, ones_ref, of_ref, ob_ref, s_scr):
    C = SCAN_CHUNK
    n_sub = rf_ref.shape[1] // C
    G = SCAN_GROUP
    W = G * HEAD_DIM
    n = G * C
    n_groups = rf_ref.shape[2] // W

    @pl.when(pl.program_id(1) == 0)
    def _():
        s_scr[...] = jnp.zeros_like(s_scr)

    bi = lax.broadcasted_iota(jnp.int32, (n, n), 0)
    bj = lax.broadcasted_iota(jnp.int32, (n, n), 1)
    same = (bi // C) == (bj // C)
    li = bi % C
    lj = bj % C
    eye = jnp.where(bi == bj, 1.0, 0.0)
    hrow = lax.broadcasted_iota(jnp.int32, (n, W), 0) // C
    hlane = lax.broadcasted_iota(jnp.int32, (n, W), 1) // HEAD_DIM
    hmask = hrow == hlane
    ti = lax.broadcasted_iota(jnp.int32, (C, C), 0)
    tj = lax.broadcasted_iota(jnp.int32, (C, C), 1)

    def rep(x):
        return jnp.where(hmask, jnp.concatenate([x] * G, axis=0), 0.0).astype(BF16)

    chains = []
    dirs = ((rf_ref, vf_ref, kkf_ref, lwf_ref, khf_ref, bf_ref, False),
            (rb_ref, vb_ref, kkb_ref, lwb_ref, khb_ref, bb_ref, True))
    per_chunk = {}
    for di, (r_ref, v_ref, kk_ref, lw_ref, kh_ref, b_ref, rev) in enumerate(dirs):
        before = (lj > li) if rev else (lj < li)
        strict = jnp.logical_and(same, before)
        incl = jnp.logical_and(same, jnp.logical_or(before, li == lj))
        tri = jnp.where((tj >= ti) if rev else (tj <= ti), 1.0, 0.0).astype(BF16)
        for t in range(n_sub):
            sub = n_sub - 1 - t if rev else t
            rows = slice(sub * C, (sub + 1) * C)
            r = r_ref[0, rows, :].astype(F32)
            v = v_ref[0, rows, :].astype(F32)
            kk = kk_ref[0, rows, :].astype(F32)
            lw = lw_ref[0, 0, rows, :]
            kh = kh_ref[0, 0, rows, :].astype(F32)
            bv = b_ref[0, 0, rows, :].astype(F32)
            cum = _dot_exact_lhs(tri, lw)
            tot = cum[0:1, :] if rev else cum[C - 1:C, :]
            e_neg = jnp.exp(-cum)
            e_end = jnp.exp(tot - cum)
            rt = r * jnp.exp(cum)
            at = -kk * jnp.exp(cum - lw)
            bt = bv * e_neg
            kt = kh * e_neg
            bp = bv * e_end
            kp = kh * e_end
            per_chunk[(di, t)] = (r, v, kh, jnp.exp(tot), rows)
            for g in range(n_groups):
                cols = slice(g * W, (g + 1) * W)
                chains.append(dict(
                    di=di, t=t, g=g, cols=cols, strict=strict, incl=incl,
                    v2=rep(v[:, cols]),
                    ar=jnp.concatenate([rep(at[:, cols]), rep(rt[:, cols])], axis=0),
                    bk=jnp.concatenate([rep(bt[:, cols]), rep(kt[:, cols])], axis=0),
                    bkp=jnp.concatenate([rep(bp[:, cols]), rep(kp[:, cols])], axis=0)))

    for ch in chains:
        ch["tt"] = _dot_nt(ch["ar"], ch["bk"])
    for ch in chains:
        ch["p"] = jnp.where(ch["strict"], ch["tt"][:n, :n], 0.0).astype(BF16)
        ch["tinv"] = eye + ch["p"].astype(F32)
    step = 2
    while step < C:
        for ch in chains:
            ch["p"] = _dot(ch["p"], ch["p"]).astype(BF16)
        for ch in chains:
            ch["tinv"] = ch["tinv"] + _dot(ch["p"], ch["tinv"])
        step *= 2
    state = {(di, g): s_scr[di, g] for di in range(2) for g in range(n_groups)}
    for t in range(n_sub):
        cur = [ch for ch in chains if ch["t"] == t]
        for ch in cur:
            ch["s0"] = state[(ch["di"], ch["g"])]
            ch["xs"] = _dot_nt(ch["ar"], ch["s0"])
        for ch in cur:
            a_ak = jnp.where(ch["strict"], ch["tt"][:n, n:], 0.0)
            ch["rhs"] = ch["xs"][:n] + _dot(a_ak, ch["v2"])
        for ch in cur:
            u = _dot(ch["tinv"], ch["rhs"])
            ch["uv"] = jnp.concatenate([u.astype(BF16), ch["v2"]], axis=0)
        for ch in cur:
            e_tot = per_chunk[(ch["di"], t)][3]
            state[(ch["di"], ch["g"])] = ch["s0"] * e_tot[:, ch["cols"]] + _dot_tn(ch["uv"], ch["bkp"])
        for ch in cur:
            a_r = jnp.concatenate([jnp.where(ch["incl"], ch["tt"][n:, :n], 0.0),
                                   jnp.where(ch["incl"], ch["tt"][n:, n:], 0.0)], axis=1)
            ch["y2"] = ch["xs"][n:] + _dot(a_r, ch["uv"])
    for (di, g), s_new in state.items():
        s_scr[di, g] = s_new

    ones_bd = ones_ref[...]
    inv = 1.0 / HEAD_DIM
    for di, o_ref in enumerate((of_ref, ob_ref)):
        for t in range(n_sub):
            r, v, kh, _, rows = per_chunk[(di, t)]
            ys = []
            for ch in chains:
                if ch["di"] == di and ch["t"] == t:
                    y = ch["y2"][0:C]
                    for h in range(1, G):
                        y = y + ch["y2"][h * C:(h + 1) * C]
                    ys.append(y)
            y = jnp.concatenate(ys, axis=1)
            mu = _head_sum(y, ones_bd) * inv
            yc = y - mu
            var = _head_sum(yc * yc, ones_bd) * inv
            gn = yc * lax.rsqrt(var + GN_EPS) * gng_ref[...] + gnb_ref[...]
            bonus = _head_sum(r * kh * rk_ref[...], ones_bd) * v
            o_ref[0, rows, :] = gn + bonus


def _rwkv_scan(r, v, kk, lw, kh, bvec, r_k, gn_g, gn_b, *, n_lat_tokens):
    bsz, t_all, rw = r.shape
    C = SCAN_BLOCK
    n_chunks = t_all // C
    n_ctx = (t_all - n_lat_tokens) // C
    n_lat = n_lat_tokens // C
    W = SCAN_GROUP * HEAD_DIM

    blk_f = lambda c: jnp.where(c < n_ctx, n_lat + c, c - n_ctx)
    blk_b = lambda c: n_chunks - 1 - c
    ones_bd = (np.arange(LANE)[:, None] // HEAD_DIM == np.arange(LANE)[None, :] // HEAD_DIM)
    ones_bd = jnp.asarray(ones_bd, BF16)
    shared = lambda blk: pl.BlockSpec((1, C, rw), lambda b, c: (b, blk(c), 0))
    perdir = lambda d, blk: pl.BlockSpec((1, 1, C, rw), lambda b, c: (d, b, blk(c), 0))
    vec = pl.BlockSpec((1, rw), lambda b, c: (0, 0))
    out = jax.ShapeDtypeStruct((bsz, t_all, rw), F32)
    return pl.pallas_call(
        _scan_kernel,
        grid=(bsz, n_chunks),
        in_specs=[shared(blk_f), shared(blk_f), shared(blk_f), perdir(0, blk_f), perdir(0, blk_f), perdir(0, blk_f),
                  shared(blk_b), shared(blk_b), shared(blk_b), perdir(1, blk_b), perdir(1, blk_b), perdir(1, blk_b),
                  vec, vec, vec, pl.BlockSpec((LANE, LANE), lambda b, c: (0, 0))],
        out_specs=[shared(blk_f), shared(blk_b)],
        out_shape=[out, out],
        scratch_shapes=[pltpu.VMEM((2, rw // W, W, W), F32)],
        compiler_params=_cparams(("arbitrary", "arbitrary")),
        name="rwkv_scan",
    )(r, v, kk, lw, kh, bvec, r, v, kk, lw, kh, bvec,
      r_k.reshape(1, rw), gn_g.reshape(1, rw), gn_b.reshape(1, rw), ones_bd)


def _fft_tables(L):
    n2 = FFT_N2
    n_all = 2 * L
    n1 = n_all // n2
    nh = n1 // 2
    two_pi = 2.0 * np.pi
    kh = -(-(nh + 1) // FFT_UNROLL) * FFT_UNROLL
    k1 = np.arange(kh)
    pair_w = np.where(k1 > nh, 0.0, np.where((k1 == 0) | (k1 == nh), 1.0, 2.0))
    th1 = two_pi * np.outer(k1, np.arange(nh)) / n1
    f1 = np.concatenate([np.cos(th1), -np.sin(th1)], 0)
    th2 = two_pi * np.outer(np.arange(n2), np.arange(n2)) / n2
    fr, fi = np.cos(th2), -np.sin(th2)
    a3 = np.block([[fr, -fi], [fi, fr]])
    b3 = np.block([[-fi, -fr], [fr, -fi]])
    tht = two_pi * np.outer(k1, np.arange(n2)) / n_all
    twr = np.tile(np.cos(tht), (1, 2))
    twi = np.tile(-np.sin(tht), (1, 2))
    gr, gi = np.cos(th2), np.sin(th2)
    m3i = np.block([[gr, -gi], [gi, gr]]) / n_all
    th4 = two_pi * np.outer(np.arange(nh), k1) / n1
    cos4, sin4 = np.cos(th4) * pair_w, np.sin(th4) * pair_w
    ca = np.concatenate([cos4, -sin4], 1)
    cb = np.concatenate([-sin4, -cos4], 1)
    th5 = two_pi * np.outer(np.arange(n2), k1) / n_all
    t2r, t2i = np.tile(np.cos(th5), (1, 2)), np.tile(np.sin(th5), (1, 2))
    f = lambda a: jnp.asarray(a, F32)
    return dict(f1=f(f1), a3=f(a3), b3=f(b3), twr=f(twr), twi=f(twi),
                m3i=f(m3i), ca=f(ca), cb=f(cb), t2r=f(t2r), t2i=f(t2i))


def _fft_stage1(src_ref, f1_ref, s_ref):
    kh2, nh = f1_ref.shape
    n2 = FFT_N2
    f1 = f1_ref[...].astype(BF16)

    def body(nlo, carry):
        xin = src_ref[pl.ds(nlo, nh, stride=n2), :]
        s_ref[pl.ds(pl.multiple_of(nlo * kh2, SUB), kh2), :] = _dot(f1, xin)
        return carry

    lax.fori_loop(0, n2, body, 0, unroll=FFT_UNROLL)


def _fft_mid_matrix(k1, a3_ref, b3_ref, twr_ref, twi_ref):
    tr = twr_ref[pl.ds(k1, 1), :]
    ti = twi_ref[pl.ds(k1, 1), :]
    return (a3_ref[...] * tr + b3_ref[...] * ti).astype(BF16)


def _fft_load_k1(s_ref, k1, kh2):
    n2 = FFT_N2
    re = s_ref[pl.ds(k1, n2, stride=kh2), :]
    im = s_ref[pl.ds(kh2 // 2 + k1, n2, stride=kh2), :]
    return jnp.concatenate([re, im], axis=0)


def _hyena_kernel(x0_ref, x1_ref, v_ref, bias_ref, kf_ref,
                  f1_ref, a3_ref, b3_ref, twr_ref, twi_ref, m3i_ref, ca_ref, cb_ref, t2r_ref, t2i_ref,
                  o_ref, z_scr, y_scr, s_scr, q_scr):
    n2 = FFT_N2
    kh2, nh = f1_ref.shape
    kh = kh2 // 2
    z_scr[...] = v_ref[0].astype(F32) * x1_ref[0].astype(F32)
    _fft_stage1(z_scr, f1_ref, s_scr)
    m3i = m3i_ref[...].astype(BF16)

    def mid(k1, carry):
        m = _fft_mid_matrix(k1, a3_ref, b3_ref, twr_ref, twi_ref)
        cc = _dot(m, _fft_load_k1(s_scr, k1, kh2))
        kf = kf_ref[0, k1]
        cr, ci = cc[:n2], cc[n2:]
        kr, ki = kf[:n2], kf[n2:]
        pp = jnp.concatenate([cr * kr - ci * ki, cr * ki + ci * kr], axis=0)
        q = _dot(m3i, pp)
        q_scr[pl.ds(k1, n2, stride=kh2), :] = q[:n2]
        q_scr[pl.ds(kh + k1, n2, stride=kh2), :] = q[n2:]
        return carry

    lax.fori_loop(0, kh, mid, 0, unroll=FFT_UNROLL)

    def last(m2, carry):
        f4 = ca_ref[...] * t2r_ref[pl.ds(m2, 1), :] + cb_ref[...] * t2i_ref[pl.ds(m2, 1), :]
        qm = q_scr[pl.ds(pl.multiple_of(m2 * kh2, SUB), kh2), :]
        y_scr[pl.ds(m2, nh, stride=n2), :] = _dot(f4, qm)
        return carry

    lax.fori_loop(0, n2, last, 0, unroll=FFT_UNROLL)
    z = z_scr[...]
    o_ref[0] = (y_scr[...] + z * bias_ref[...]) * x0_ref[0].astype(F32)


def _hyena_conv(p, hy_bias, kf, tabs, *, hy_col0, hy_dim, ct):
    bsz, L, _ = p.shape
    n_tiles = hy_dim // ct
    blk0 = hy_col0 // ct
    sec = hy_dim // ct
    xspec = lambda s: pl.BlockSpec((1, L, ct), lambda t, b: (b, 0, blk0 + s * sec + t))
    tab_list = [tabs[k] for k in ("f1", "a3", "b3", "twr", "twi", "m3i", "ca", "cb", "t2r", "t2i")]
    full = lambda a: pl.BlockSpec(a.shape, lambda t, b: (0,) * a.ndim)
    kh2 = tabs["f1"].shape[0]
    return pl.pallas_call(
        _hyena_kernel,
        grid=(n_tiles, bsz),
        in_specs=[xspec(0), xspec(1), xspec(2),
                  pl.BlockSpec((1, ct), lambda t, b: (0, t)),
                  pl.BlockSpec((1,) + kf.shape[1:], lambda t, b: (t, 0, 0, 0), pipeline_mode=pl.Buffered(1))]
                 + [full(a) for a in tab_list],
        out_specs=pl.BlockSpec((1, L, ct), lambda t, b: (b, 0, t)),
        out_shape=jax.ShapeDtypeStruct((bsz, L, hy_dim), F32),
        scratch_shapes=[pltpu.VMEM((L, ct), F32), pltpu.VMEM((L, ct), F32),
                        pltpu.VMEM((FFT_N2 * kh2, ct), F32), pltpu.VMEM((FFT_N2 * kh2, ct), F32)],
        compiler_params=_cparams(("arbitrary", "arbitrary")),
        name="hyena_conv",
    )(p, p, p, hy_bias.reshape(1, -1), kf, *tab_list)


def _filter_kernel(z_ref, t_ref, w1_ref, b1_ref, w2_ref, b2_ref, w3f_ref, w3b_ref, freq_ref, delta_ref,
                   f1_ref, a3_ref, b3_ref, twr_ref, twi_ref, kf_ref, h_scr, hf_scr, hb_scr, sf_scr, sb_scr):
    n2 = FFT_N2
    kh2 = f1_ref.shape[0]

    @pl.when(pl.program_id(0) == 0)
    def _():
        freq = freq_ref[...]
        h1 = jnp.sin(freq * (_dot3(z_ref[...], w1_ref[...]) + b1_ref[...]))
        h_scr[...] = jnp.sin(freq * (_dot3(h1, w2_ref[...]) + b2_ref[...]))

    h = h_scr[...]
    window = jnp.exp(-t_ref[...] * delta_ref[...])
    hf_scr[...] = _dot3(h, w3f_ref[...]) * window
    hb = _dot3(h, w3b_ref[...]) * window
    rows = lax.broadcasted_iota(jnp.int32, hb.shape, 0)
    hb_scr[...] = jnp.where(rows == 0, 0.0, hb)
    _fft_stage1(hf_scr, f1_ref, sf_scr)
    _fft_stage1(hb_scr, f1_ref, sb_scr)

    def mid(k1, carry):
        m = _fft_mid_matrix(k1, a3_ref, b3_ref, twr_ref, twi_ref)
        cf = _dot(m, _fft_load_k1(sf_scr, k1, kh2))
        cb = _dot(m, _fft_load_k1(sb_scr, k1, kh2))
        kf_ref[0, k1] = jnp.concatenate([cf[:n2] + cb[:n2], cf[n2:] - cb[n2:]], axis=0)
        return carry

    lax.fori_loop(0, kh2 // 2, mid, 0, unroll=FFT_UNROLL)


def _filter_spectrum(zpos, tcol, w1, b1, w2, b2, w3, freq, deltas, tabs, *, ct):
    L = zpos.shape[0]
    hy_dim = deltas.shape[1]
    n_tiles = hy_dim // ct
    kh2 = tabs["f1"].shape[0]
    kh = kh2 // 2
    tab_list = [tabs[k] for k in ("f1", "a3", "b3", "twr", "twi")]
    full = lambda a: pl.BlockSpec(a.shape, lambda t: (0,) * a.ndim)
    row = lambda a: a.reshape(1, -1)
    small = [zpos, tcol, w1, row(b1), w2, row(b2)]
    return pl.pallas_call(
        _filter_kernel,
        grid=(n_tiles,),
        in_specs=[full(a) for a in small]
                 + [pl.BlockSpec((w3.shape[0], ct), lambda t: (0, t)),
                    pl.BlockSpec((w3.shape[0], ct), lambda t: (0, n_tiles + t)),
                    full(row(freq)),
                    pl.BlockSpec((1, ct), lambda t: (0, t))]
                 + [full(a) for a in tab_list],
        out_specs=pl.BlockSpec((1, kh, 2 * FFT_N2, ct), lambda t: (t, 0, 0, 0)),
        out_shape=jax.ShapeDtypeStruct((n_tiles, kh, 2 * FFT_N2, ct), F32),
        scratch_shapes=[pltpu.VMEM((L, w2.shape[1]), F32), pltpu.VMEM((L, ct), F32), pltpu.VMEM((L, ct), F32),
                        pltpu.VMEM((FFT_N2 * kh2, ct), F32), pltpu.VMEM((FFT_N2 * kh2, ct), F32)],
        compiler_params=_cparams(("arbitrary",)),
        name="hyena_filter",
    )(*small, w3, w3, row(freq), deltas, *tab_list)


def _outproj_kernel(of_ref, ob_ref, gate_ref, ohy_ref, x_ref, er_ref, ec_ref, g1_ref, w_ref, lng_ref, lnb_ref,
                    o_ref):
    rw = of_ref.shape[2]
    a_rw = ((of_ref[0] + ob_ref[0]) * gate_ref[0].astype(F32)).astype(BF16)
    mix = jnp.dot(a_rw, w_ref[0:rw, :], preferred_element_type=F32)
    mix = mix + jnp.dot(ohy_ref[0].astype(BF16), w_ref[rw:, :], preferred_element_type=F32)
    gmix = g1_ref[0] * mix
    tm = x_ref.shape[1]
    for q in range(tm // GRID_W):
        rows = slice(q * GRID_W, (q + 1) * GRID_W)
        xv = _add_pos(x_ref[0, rows, :], er_ref[q:q + 1, :], ec_ref[...])
        o_ref[0, rows, :] = _layer_norm(ALPHA * xv + gmix[rows], lng_ref[...], lnb_ref[...]).astype(o_ref.dtype)


def _outproj(o_f, o_b, gate, o_hy, x, er, ec, mod3, w_out, ln_g, ln_b, *, tm):
    bsz, L, d = x.shape
    rw = gate.shape[2]
    hy = o_hy.shape[2]
    gr = tm // GRID_W
    return pl.pallas_call(
        _outproj_kernel,
        grid=(bsz, L // tm),
        in_specs=[pl.BlockSpec((1, tm, rw), lambda b, i: (b, i, 0)),
                  pl.BlockSpec((1, tm, rw), lambda b, i: (b, i, 0)),
                  pl.BlockSpec((1, tm, rw), lambda b, i: (b, i, 0)),
                  pl.BlockSpec((1, tm, hy), lambda b, i: (b, i, 0)),
                  pl.BlockSpec((1, tm, d), lambda b, i: (b, i, 0)),
                  pl.BlockSpec((gr, d // 2), lambda b, i: (i, 0)),
                  pl.BlockSpec((GRID_W, d // 2), lambda b, i: (0, 0)),
                  pl.BlockSpec((1, 1, d), lambda b, i: (b, 0, 2)),
                  pl.BlockSpec((d, d), lambda b, i: (0, 0)),
                  pl.BlockSpec((1, d), lambda b, i: (0, 0)),
                  pl.BlockSpec((1, d), lambda b, i: (0, 0))],
        out_specs=pl.BlockSpec((1, tm, d), lambda b, i: (b, i, 0)),
        out_shape=jax.ShapeDtypeStruct((bsz, L, d), BF16),
        compiler_params=_cparams(("arbitrary", "arbitrary")),
        name="outproj_ln",
    )(o_f, o_b, gate, o_hy, x, er, ec, mod3, w_out, ln_g.reshape(1, d), ln_b.reshape(1, d))


def _ffn_kernel(x_ref, sh_ref, sc_ref, g2_ref, w1_ref, w3_ref, w2_ref, lng_ref, lnb_ref, o_ref, a_scr):
    f = pl.program_id(2)
    tm = x_ref.shape[1]

    @pl.when(f == 0)
    def _():
        a_scr[...] = (x_ref[0].astype(F32) * (1.0 + sc_ref[0]) + sh_ref[0]).astype(BF16)
        o_ref[0] = jnp.zeros((tm, o_ref.shape[2]), F32)

    a = a_scr[...]
    h1 = jnp.dot(a, w1_ref[...], preferred_element_type=F32)
    h3 = jnp.dot(a, w3_ref[...], preferred_element_type=F32)
    h = (h1 * _sigmoid(h1) * h3).astype(BF16)
    o_ref[0] += jnp.dot(h, w2_ref[...], preferred_element_type=F32)

    @pl.when(f == pl.num_programs(2) - 1)
    def _():
        for r0 in range(0, tm, FFN_LN_ROWS):
            rows = slice(r0, r0 + FFN_LN_ROWS)
            res = ALPHA * x_ref[0, rows, :].astype(F32) + g2_ref[0] * o_ref[0, rows, :]
            o_ref[0, rows, :] = _layer_norm(res, lng_ref[...], lnb_ref[...])


def _ffn(x1, mod3, w1, w3, w2, ln_g, ln_b, *, tm, tf):
    bsz, L, d = x1.shape
    dff = w1.shape[1]
    return pl.pallas_call(
        _ffn_kernel,
        grid=(bsz, L // tm, dff // tf),
        in_specs=[pl.BlockSpec((1, tm, d), lambda b, i, f: (b, i, 0)),
                  pl.BlockSpec((1, 1, d), lambda b, i, f: (b, 0, 3)),
                  pl.BlockSpec((1, 1, d), lambda b, i, f: (b, 0, 4)),
                  pl.BlockSpec((1, 1, d), lambda b, i, f: (b, 0, 5)),
                  pl.BlockSpec((d, tf), lambda b, i, f: (0, f)),
                  pl.BlockSpec((d, tf), lambda b, i, f: (0, f)),
                  pl.BlockSpec((tf, d), lambda b, i, f: (f, 0)),
                  pl.BlockSpec((1, d), lambda b, i, f: (0, 0)),
                  pl.BlockSpec((1, d), lambda b, i, f: (0, 0))],
        out_specs=pl.BlockSpec((1, tm, d), lambda b, i, f: (b, i, 0)),
        out_shape=jax.ShapeDtypeStruct((bsz, L, d), F32),
        scratch_shapes=[pltpu.VMEM((tm, d), BF16)],
        compiler_params=_cparams(("arbitrary", "arbitrary", "arbitrary")),
        name="ffn_ln",
    )(x1, mod3, mod3, mod3, w1, w3, w2, ln_g.reshape(1, d), ln_b.reshape(1, d))


def _pos_tables(rows, d):
    quarter = d // 4
    omega = 1.0 / (10000.0 ** (jnp.arange(quarter, dtype=F32) / quarter))
    er = jnp.arange(rows, dtype=F32)[:, None] * omega
    ec = jnp.arange(GRID_W, dtype=F32)[:, None] * omega
    er = jnp.concatenate([jnp.sin(er), jnp.cos(er)], -1)
    ec = jnp.concatenate([jnp.sin(ec), jnp.cos(ec)], -1)
    return er, ec


def _filter_positions(L):
    pos = jnp.arange(L, dtype=F32)[:, None]
    t = jnp.linspace(0.0, 1.0, L, dtype=F32)[:, None]
    bands = jnp.linspace(1e-4, N_BANDS - 1, N_BANDS, dtype=F32)[None, :]
    ang = 2.0 * math.pi * bands * pos / L
    z = jnp.concatenate([t, jnp.cos(ang), -jnp.sin(ang)], -1)
    return z, t


def _pad_cols(a, width):
    return jnp.pad(a, ((0, 0), (0, width - a.shape[1])))


def _pad_rows(a, height):
    return jnp.pad(a, ((0, height - a.shape[0]), (0, 0)))


def kernel(x, c, ctx, c_ctx, w_ada, b_ada, w_in, conv_rw, conv_hy, conv_hy_b, w0_f, w_up_f, a0_f, a_up_f, w0_b, w_up_b, a0_b, a_up_b, k_k, k_a, r_k, g_up, gn_g, gn_b, filt_w1, filt_b1, filt_w2, filt_b2, filt_w3, sin_freq, hy_bias, w_out, ln1_g, ln1_b, ffn_w1, ffn_w3, ffn_w2, ln2_g, ln2_b):
    bsz, L, d = x.shape
    rw = k_k.shape[1]
    hy = hy_bias.shape[1]
    dl, il, gl = w_up_f.shape[1], a_up_f.shape[1], g_up.shape[1]
    lo_w = 4 * LANE
    rkv_cols = 3 * rw

    wi = w_in[0]
    lo0 = rkv_cols

    def lora_layout(a):
        return jnp.concatenate([_pad_cols(a[:, lo0:lo0 + dl], LANE),
                                _pad_cols(a[:, lo0 + dl:lo0 + dl + il], LANE),
                                _pad_cols(a[:, lo0 + dl + il:lo0 + dl + il + gl], 2 * LANE)], axis=1)

    hy0 = lo0 + dl + il + gl
    w_lat = jnp.concatenate([wi[:, :rkv_cols].astype(BF16), wi[:, hy0:].astype(BF16),
                             lora_layout(wi).astype(BF16)], axis=1)
    cw_lat = jnp.concatenate([conv_rw[0][:, :rkv_cols], conv_hy[0], lora_layout(conv_rw[0])], axis=1)
    cb_lat = jnp.concatenate([jnp.zeros((rkv_cols,), F32), conv_hy_b[0], jnp.zeros((lo_w,), F32)])[None, :]
    wup = _pad_rows(jnp.concatenate([w_up_f[0], w_up_b[0]], axis=1), LANE)
    aup = _pad_rows(jnp.concatenate([a_up_f[0], a_up_b[0]], axis=1), LANE)
    gup = _pad_rows(g_up[0], 2 * LANE)
    w0 = jnp.concatenate([w0_f[0], w0_b[0]])
    a0 = jnp.concatenate([a0_f[0], a0_b[0]])

    cc = _pad_rows(jnp.concatenate([c, c_ctx[None, :]], axis=0), SUB)
    mod = _adaln(cc, w_ada[0], b_ada[0])
    mod3 = mod.reshape(SUB, 1, 6 * d)

    er, ec = _pos_tables(L // GRID_W, d)

    tn = lo_w
    p_lat = _inproj(x, er, ec, mod3, w_lat, cw_lat, cb_lat, tm=1024, tn=tn, add_pos=True, ctx_row=None)
    ctx_tiles = list(range(rkv_cols // tn)) + [(rkv_cols + 3 * hy) // tn]
    p_ctx = _inproj(ctx, er, ec, mod3, w_lat, cw_lat, cb_lat, tm=ctx.shape[1], tn=tn, add_pos=False,
                    ctx_row=bsz, col_tiles=ctx_tiles)

    lo_cols = (rkv_cols + 3 * hy, rkv_cols + 3 * hy + lo_w)
    r, v, kk, gate, lw, kh, bvec = _rwkv_prep(p_lat, p_ctx, wup, aup, gup, w0, a0,
                                              k_k[0], k_a[0], rw=rw, rkv_cols=rkv_cols, lo_cols=lo_cols)
    o_f, o_b = _rwkv_scan(r, v, kk, lw, kh, bvec, r_k[0], gn_g[0], gn_b[0], n_lat_tokens=L)

    tabs = _fft_tables(L)
    zpos, tcol = _filter_positions(L)
    max_decay = math.log(DECAY_TARGET) / FAST_DECAY_PCT
    min_decay = math.log(DECAY_TARGET) / SLOW_DECAY_PCT
    deltas = jnp.abs(jnp.linspace(min_decay, max_decay, hy, dtype=F32))[None, :]
    kf = _filter_spectrum(_pad_cols(zpos, LANE), tcol, _pad_rows(filt_w1[0], LANE), filt_b1[0], filt_w2[0],
                          filt_b2[0], filt_w3[0], sin_freq[0], deltas, tabs, ct=LANE)
    o_hy = _hyena_conv(p_lat, hy_bias[0], kf, tabs, hy_col0=rkv_cols, hy_dim=hy, ct=LANE)

    x1 = _outproj(o_f, o_b, gate, o_hy, x, er, ec, mod3, w_out[0].astype(BF16), ln1_g[0], ln1_b[0], tm=512)
    return _ffn(x1, mod3, ffn_w1[0].astype(BF16), ffn_w3[0].astype(BF16), ffn_w2[0].astype(BF16),
                ln2_g[0], ln2_b[0], tm=1024, tf=512)
```

```python
import functools
import math

import numpy as np
import jax
import jax.numpy as jnp
from jax import lax
from jax.experimental import pallas as pl
from jax.experimental.pallas import tpu as pltpu

F32 = jnp.float32
BF16 = jnp.bfloat16

GRID_W = 64
HEAD_DIM = 64
N_BANDS = 16
DECAY_TARGET = 1e-2
FAST_DECAY_PCT = 0.3
SLOW_DECAY_PCT = 1.5
LN_EPS = 1e-5
GN_EPS = 64e-5
ALPHA = 2.0 ** 0.25

LANE = 128
SUB = 8
VMEM_LIMIT = 56 * 1024 * 1024

SCAN_CHUNK = 64
SCAN_GROUP = 2
SCAN_BLOCK = 128
FFN_LN_ROWS = 256
FFT_N2 = 64
FFT_UNROLL = 8


def _cparams(sem):
    return pltpu.CompilerParams(dimension_semantics=sem, vmem_limit_bytes=VMEM_LIMIT)


def _dot(a, b, dims=(((1,), (0,)), ((), ()))):
    return lax.dot_general(a.astype(BF16), b.astype(BF16), dims, preferred_element_type=F32)


def _dot_nt(a, b):
    return _dot(a, b, (((1,), (1,)), ((), ())))


def _dot_tn(a, b):
    return _dot(a, b, (((0,), (0,)), ((), ())))


def _split2(a):
    hi = a.astype(BF16)
    lo = (a - hi.astype(F32)).astype(BF16)
    return hi, lo


def _dot3(a, b, b_split=None):
    ah, al = _split2(a)
    bh, bl = _split2(b) if b_split is None else b_split
    return _dot(ah, bh) + (_dot(al, bh) + _dot(ah, bl))


def _dot_exact_rhs(a, b_exact):
    a1 = a.astype(BF16)
    r1 = a - a1.astype(F32)
    a2 = r1.astype(BF16)
    a3 = (r1 - a2.astype(F32)).astype(BF16)
    return _dot(a1, b_exact) + (_dot(a2, b_exact) + _dot(a3, b_exact))


def _dot_exact_lhs(a_exact, b):
    b1 = b.astype(BF16)
    r1 = b - b1.astype(F32)
    b2 = r1.astype(BF16)
    b3 = (r1 - b2.astype(F32)).astype(BF16)
    return _dot(a_exact, b1) + (_dot(a_exact, b2) + _dot(a_exact, b3))


def _sigmoid(x):
    return 0.5 * jnp.tanh(0.5 * x) + 0.5


def _layer_norm(h, g, b):
    mu = jnp.mean(h, -1, keepdims=True)
    hc = h - mu
    var = jnp.mean(hc * hc, -1, keepdims=True)
    return hc * lax.rsqrt(var + LN_EPS) * g + b


def _adaln_kernel(c_ref, w_ref, b_ref, o_ref):
    cv = c_ref[...]
    a = cv * _sigmoid(cv)
    o_ref[...] = _dot3(a, w_ref[...]) + b_ref[...]


def _adaln(cc, w, b, tn=1024):
    m, d = cc.shape
    n = w.shape[1]
    return pl.pallas_call(
        _adaln_kernel,
        grid=(n // tn,),
        in_specs=[pl.BlockSpec((m, d), lambda j: (0, 0)),
                  pl.BlockSpec((d, tn), lambda j: (0, j)),
                  pl.BlockSpec((1, tn), lambda j: (0, j))],
        out_specs=pl.BlockSpec((m, tn), lambda j: (0, j)),
        out_shape=jax.ShapeDtypeStruct((m, n), F32),
        compiler_params=_cparams(("arbitrary",)),
        name="adaln",
    )(cc, w, b.reshape(1, n))


def _wlayout_kernel(w_ref, o_ref, *, rkv_cols, lora_widths, hy_cols):
    o_ref[:, 0:rkv_cols] = w_ref[:, 0:rkv_cols].astype(o_ref.dtype)
    src = rkv_cols
    dst = rkv_cols + hy_cols
    for width in lora_widths:
        padded = -(-width // LANE) * LANE
        piece = w_ref[:, src:src + width].astype(o_ref.dtype)
        if padded > width:
            piece = jnp.concatenate([piece, jnp.zeros((piece.shape[0], padded - width), o_ref.dtype)], axis=1)
        o_ref[:, dst:dst + padded] = piece
        src += width
        dst += padded
    o_ref[:, rkv_cols:rkv_cols + hy_cols] = w_ref[:, src:src + hy_cols].astype(o_ref.dtype)


def _weight_layout(w, *, rkv_cols, lora_widths, hy_cols, tk=256):
    k, n_src = w.shape
    n_dst = rkv_cols + hy_cols + sum(-(-wd // LANE) * LANE for wd in lora_widths)
    return pl.pallas_call(
        functools.partial(_wlayout_kernel, rkv_cols=rkv_cols, lora_widths=lora_widths, hy_cols=hy_cols),
        grid=(k // tk,),
        in_specs=[pl.BlockSpec((tk, n_src), lambda i: (i, 0))],
        out_specs=pl.BlockSpec((tk, n_dst), lambda i: (i, 0)),
        out_shape=jax.ShapeDtypeStruct((k, n_dst), BF16),
        compiler_params=_cparams(("arbitrary",)),
        name="weight_layout",
    )(w)


HALO = 16


def _add_pos(xv, er_row, ec_rows):
    half = ec_rows.shape[1]
    return jnp.concatenate([xv[:, :half] + er_row, xv[:, half:] + ec_rows], axis=1)


def _inproj_kernel(x_ref, xp_ref, xn_ref, er_ref, ec_ref, sh_ref, sc_ref, w_ref, cw_ref, cb_ref, o_ref, a_scr,
                   *, add_pos):
    i = pl.program_id(1)
    n_i = pl.num_programs(1)
    tm = x_ref.shape[1]
    gr = tm // GRID_W

    @pl.when(pl.program_id(2) == 0)
    def _():
        scale = 1.0 + sc_ref[0]
        shift = sh_ref[0]

        def put(rows, xv, keep=None):
            a = xv * scale + shift
            if keep is not None:
                a = jnp.where(keep, a, 0.0)
            a_scr[rows, :] = a.astype(BF16)

        has_prev = i > 0
        has_next = i < n_i - 1
        lo_rows, hi_rows = slice(0, HALO), slice(HALO + tm, 2 * HALO + tm)
        if add_pos:
            n_er = er_ref.shape[0]
            for q in range(gr):
                rows = slice(q * GRID_W, (q + 1) * GRID_W)
                put(slice(HALO + q * GRID_W, HALO + (q + 1) * GRID_W),
                    _add_pos(x_ref[0, rows, :], er_ref[pl.ds(i * gr + q, 1), :], ec_ref[...]))
            put(lo_rows, _add_pos(xp_ref[0], er_ref[pl.ds(jnp.maximum(i * gr - 1, 0), 1), :],
                                  ec_ref[GRID_W - HALO:, :]), has_prev)
            put(hi_rows, _add_pos(xn_ref[0], er_ref[pl.ds(jnp.minimum((i + 1) * gr, n_er - 1), 1), :],
                                  ec_ref[:HALO, :]), has_next)
        else:
            put(lo_rows, xp_ref[0], has_prev)
            put(slice(HALO, HALO + tm), x_ref[0])
            put(hi_rows, xn_ref[0], has_next)

    n_rows = tm + 2 * HALO
    half = w_ref.shape[1] // 2
    for h in range(2):
        cols = slice(h * half, (h + 1) * half)
        res = jnp.dot(a_scr[...], w_ref[:, cols], preferred_element_type=F32)
        out = (pltpu.roll(res, 1, 0) * cw_ref[0:1, cols] + res * cw_ref[1:2, cols]
               + pltpu.roll(res, n_rows - 1, 0) * cw_ref[2:3, cols] + cb_ref[:, cols])
        o_ref[0, :, cols] = out[HALO:HALO + tm].astype(o_ref.dtype)


def _inproj(x, er, ec, mod3, w, cw, cb, *, tm, tn, add_pos, ctx_row, col_tiles=None):
    bsz, lx, d = x.shape
    if col_tiles is None:
        n = w.shape[1]
        col = lambda j: j
    else:
        n = len(col_tiles) * tn
        run = col_tiles[:-1]
        assert run == list(range(len(run)))
        col = lambda j: jnp.where(j < len(run), j, col_tiles[-1])
    hb = tm // HALO
    n_hb = lx // HALO
    row = (lambda b: ctx_row) if ctx_row is not None else (lambda b: b)
    return pl.pallas_call(
        functools.partial(_inproj_kernel, add_pos=add_pos),
        grid=(bsz, lx // tm, n // tn),
        in_specs=[pl.BlockSpec((1, tm, d), lambda b, i, j: (b, i, 0)),
                  pl.BlockSpec((1, HALO, d), lambda b, i, j: (b, jnp.maximum(i * hb - 1, 0), 0)),
                  pl.BlockSpec((1, HALO, d), lambda b, i, j: (b, jnp.minimum((i + 1) * hb, n_hb - 1), 0)),
                  pl.BlockSpec(er.shape, lambda b, i, j: (0, 0)),
                  pl.BlockSpec(ec.shape, lambda b, i, j: (0, 0)),
                  pl.BlockSpec((1, 1, d), lambda b, i, j: (row(b), 0, 0)),
                  pl.BlockSpec((1, 1, d), lambda b, i, j: (row(b), 0, 1)),
                  pl.BlockSpec((d, tn), lambda b, i, j: (0, col(j))),
                  pl.BlockSpec((3, tn), lambda b, i, j: (0, col(j))),
                  pl.BlockSpec((1, tn), lambda b, i, j: (0, col(j)))],
        out_specs=pl.BlockSpec((1, tm, tn), lambda b, i, j: (b, i, j)),
        out_shape=jax.ShapeDtypeStruct((bsz, lx, n), BF16),
        scratch_shapes=[pltpu.VMEM((tm + 2 * HALO, d), BF16)],
        compiler_params=_cparams(("arbitrary", "arbitrary", "arbitrary")),
        name="inproj_pos" if add_pos else "inproj_ctx",
    )(x, x, x, er, ec, mod3, mod3, w, cw, cb)


def _head_sum(x, ones_bd):
    rows, groups = x.shape[0], x.shape[1] // LANE
    xs = jnp.concatenate([x[:, j * LANE:(j + 1) * LANE] for j in range(groups)], axis=0)
    hi, lo = _split2(xs)
    s = _dot(jnp.concatenate([hi, lo], axis=0), ones_bd)
    s = s[:groups * rows] + s[groups * rows:]
    return jnp.concatenate([s[j * rows:(j + 1) * rows] for j in range(groups)], axis=1)


def _prep_kernel(p_rkv, p_lo, pc_rkv, pc_lo, wup, aup, gup, w0, a0, kk_w, ka_w, ones_ref,
                 r_out, v_out, kk_out, gate_out, lw_out, kh_out, b_out, *, n_lat):
    is_ctx = pl.program_id(1) == n_lat
    rw = kk_out.shape[2]
    ones_bd = ones_ref[...]
    wup_split = _split2(wup[...])
    aup_split = _split2(aup[...])
    gup_bf = gup[...].astype(BF16)

    def row_block(rows):
        def rkv(s):
            cols = slice(s * rw, (s + 1) * rw)
            return jnp.where(is_ctx, pc_rkv[0, rows, cols], p_rkv[0, rows, cols])

        r_out[0, rows, :] = rkv(0)
        v_out[0, rows, :] = rkv(2)
        k = rkv(1).astype(F32)
        kkr = k * kk_w[...]
        nrm = jnp.sqrt(_head_sum(kkr * kkr, ones_bd))
        kk = kkr / jnp.maximum(nrm, 1e-12)
        kk_out[0, rows, :] = kk.astype(kk_out.dtype)

        lo = jnp.where(is_ctx, pc_lo[0, rows, :], p_lo[0, rows, :]).astype(F32)
        tw = jnp.tanh(lo[:, 0:LANE])
        al = lo[:, LANE:2 * LANE]
        gl = _sigmoid(lo[:, 2 * LANE:])
        gate_out[0, rows, :] = _dot(gl, gup_bf).astype(gate_out.dtype)
        wx = _dot3(tw, None, wup_split) + w0[...]
        ax = _dot3(al, None, aup_split) + a0[...]
        for d in range(2):
            cols = slice(d * rw, (d + 1) * rw)
            lw_out[d, 0, rows, :] = -math.exp(-0.5) * _sigmoid(wx[:, cols])
            ia = _sigmoid(ax[:, cols])
            kh_out[d, 0, rows, :] = (k * (1.0 + (ia - 1.0) * ka_w[...])).astype(kh_out.dtype)
            b_out[d, 0, rows, :] = (kk * ia).astype(b_out.dtype)

    row_block(slice(0, kk_out.shape[1]))


def _rwkv_prep(p, pc, wup, aup, gup, w0, a0, k_k, k_a, *, rw, rkv_cols, lo_cols):
    bsz, lq, _ = p.shape
    ctx = pc.shape[1]
    tr = ctx
    n_lat = lq // tr
    t_all = lq + ctx
    lo_w = lo_cols[1] - lo_cols[0]
    lo_blk = lo_cols[0] // lo_w
    lo_blk_c = rkv_cols // lo_w
    lat = lambda i: jnp.minimum(i, n_lat - 1)
    ones_bd = (np.arange(LANE)[:, None] // HEAD_DIM == np.arange(LANE)[None, :] // HEAD_DIM)
    ones_bd = jnp.asarray(ones_bd, BF16)
    full = lambda a: pl.BlockSpec(a.shape, lambda b, i: (0,) * a.ndim)
    row = lambda a: a.reshape(1, -1)
    weights = [wup, aup, gup, row(w0), row(a0), row(k_k), row(k_a), ones_bd]
    shared = jax.ShapeDtypeStruct((bsz, t_all, rw), BF16)
    perdir = jax.ShapeDtypeStruct((2, bsz, t_all, rw), BF16)
    perdir_f32 = jax.ShapeDtypeStruct((2, bsz, t_all, rw), F32)
    o_shared = pl.BlockSpec((1, tr, rw), lambda b, i: (b, i, 0))
    o_dir = pl.BlockSpec((2, 1, tr, rw), lambda b, i: (0, b, i, 0))
    return pl.pallas_call(
        functools.partial(_prep_kernel, n_lat=n_lat),
        grid=(bsz, n_lat + 1),
        in_specs=[pl.BlockSpec((1, tr, rkv_cols), lambda b, i: (b, lat(i), 0)),
                  pl.BlockSpec((1, tr, lo_w), lambda b, i: (b, lat(i), lo_blk)),
                  pl.BlockSpec((1, tr, rkv_cols), lambda b, i: (b, 0, 0)),
                  pl.BlockSpec((1, tr, lo_w), lambda b, i: (b, 0, lo_blk_c))]
                 + [full(a) for a in weights],
        out_specs=[o_shared, o_shared, o_shared, o_shared, o_dir, o_dir, o_dir],
        out_shape=[shared, shared, shared, shared, perdir_f32, perdir, perdir],
        compiler_params=_cparams(("arbitrary", "arbitrary")),
        name="rwkv_prep",
    )(p, p, pc, pc, *weights)


def _scan_kernel(rf_ref, vf_ref, kkf_ref, lwf_ref, khf_ref, bf_ref, rb_ref, vb_ref, kkb_ref, lwb_ref, khb_ref,
                 bb_ref, rk_ref, gng_ref, gnb_ref, ones_ref, of_ref, ob_ref, s_scr):
    C = SCAN_CHUNK
    n_sub = rf_ref.shape[1] // C
    G = SCAN_GROUP
    W = G * HEAD_DIM
    n = G * C
    n_groups = rf_ref.shape[2] // W

    @pl.when(pl.program_id(1) == 0)
    def _():
        s_scr[...] = jnp.zeros_like(s_scr)

    bi = lax.broadcasted_iota(jnp.int32, (n, n), 0)
    bj = lax.broadcasted_iota(jnp.int32, (n, n), 1)
    same = (bi // C) == (bj // C)
    li = bi % C
    lj = bj % C
    eye = jnp.where(bi == bj, 1.0, 0.0)
    hrow = lax.broadcasted_iota(jnp.int32, (n, W), 0) // C
    hlane = lax.broadcasted_iota(jnp.int32, (n, W), 1) // HEAD_DIM
    hmask = hrow == hlane
    ti = lax.broadcasted_iota(jnp.int32, (C, C), 0)
    tj = lax.broadcasted_iota(jnp.int32, (C, C), 1)

    def rep(x):
        return jnp.where(hmask, jnp.concatenate([x] * G, axis=0), 0.0).astype(BF16)

    chains = []
    dirs = ((rf_ref, vf_ref, kkf_ref, lwf_ref, khf_ref, bf_ref, False),
            (rb_ref, vb_ref, kkb_ref, lwb_ref, khb_ref, bb_ref, True))
    per_chunk = {}
    for di, (r_ref, v_ref, kk_ref, lw_ref, kh_ref, b_ref, rev) in enumerate(dirs):
        before = (lj > li) if rev else (lj < li)
        strict = jnp.logical_and(same, before)
        incl = jnp.logical_and(same, jnp.logical_or(before, li == lj))
        tri = jnp.where((tj >= ti) if rev else (tj <= ti), 1.0, 0.0).astype(BF16)
        for t in range(n_sub):
            sub = n_sub - 1 - t if rev else t
            rows = slice(sub * C, (sub + 1) * C)
            r = r_ref[0, rows, :].astype(F32)
            v = v_ref[0, rows, :].astype(F32)
            kk = kk_ref[0, rows, :].astype(F32)
            lw = lw_ref[0, 0, rows, :]
            kh = kh_ref[0, 0, rows, :].astype(F32)
            bv = b_ref[0, 0, rows, :].astype(F32)
            cum = _dot_exact_lhs(tri, lw)
            tot = cum[0:1, :] if rev else cum[C - 1:C, :]
            e_neg = jnp.exp(-cum)
            e_end = jnp.exp(tot - cum)
            rt = r * jnp.exp(cum)
            at = -kk * jnp.exp(cum - lw)
            bt = bv * e_neg
            kt = kh * e_neg
            bp = bv * e_end
            kp = kh * e_end
            per_chunk[(di, t)] = (r, v, kh, jnp.exp(tot), rows)
            for g in range(n_groups):
                cols = slice(g * W, (g + 1) * W)
                chains.append(dict(
                    di=di, t=t, g=g, cols=cols, strict=strict, incl=incl,
                    v2=rep(v[:, cols]),
                    ar=jnp.concatenate([rep(at[:, cols]), rep(rt[:, cols])], axis=0),
                    bk=jnp.concatenate([rep(bt[:, cols]), rep(kt[:, cols])], axis=0),
                    bkp=jnp.concatenate([rep(bp[:, cols]), rep(kp[:, cols])], axis=0)))

    for ch in chains:
        ch["tt"] = _dot_nt(ch["ar"], ch["bk"])
    for ch in chains:
        ch["p"] = jnp.where(ch["strict"], ch["tt"][:n, :n], 0.0).astype(BF16)
        ch["tinv"] = eye + ch["p"].astype(F32)
    step = 2
    while step < C:
        for ch in chains:
            ch["p"] = _dot(ch["p"], ch["p"]).astype(BF16)
        for ch in chains:
            ch["tinv"] = ch["tinv"] + _dot(ch["p"], ch["tinv"])
        step *= 2
    state = {(di, g): s_scr[di, g] for di in range(2) for g in range(n_groups)}
    for t in range(n_sub):
        cur = [ch for ch in chains if ch["t"] == t]
        for ch in cur:
            ch["s0"] = state[(ch["di"], ch["g"])]
            ch["xs"] = _dot_nt(ch["ar"], ch["s0"])
        for ch in cur:
            a_ak = jnp.where(ch["strict"], ch["tt"][:n, n:], 0.0)
            ch["rhs"] = ch["xs"][:n] + _dot(a_ak, ch["v2"])
        for ch in cur:
            u = _dot(ch["tinv"], ch["rhs"])
            ch["uv"] = jnp.concatenate([u.astype(BF16), ch["v2"]], axis=0)
        for ch in cur:
            e_tot = per_chunk[(ch["di"], t)][3]
            state[(ch["di"], ch["g"])] = ch["s0"] * e_tot[:, ch["cols"]] + _dot_tn(ch["uv"], ch["bkp"])
        for ch in cur:
            a_r = jnp.concatenate([jnp.where(ch["incl"], ch["tt"][n:, :n], 0.0),
                                   jnp.where(ch["incl"], ch["tt"][n:, n:], 0.0)], axis=1)
            ch["y2"] = ch["xs"][n:] + _dot(a_r, ch["uv"])
    for (di, g), s_new in state.items():
        s_scr[di, g] = s_new

    ones_bd = ones_ref[...]
    inv = 1.0 / HEAD_DIM
    for di, o_ref in enumerate((of_ref, ob_ref)):
        for t in range(n_sub):
            r, v, kh, _, rows = per_chunk[(di, t)]
            ys = []
            for ch in chains:
                if ch["di"] == di and ch["t"] == t:
                    y = ch["y2"][0:C]
                    for h in range(1, G):
                        y = y + ch["y2"][h * C:(h + 1) * C]
                    ys.append(y)
            y = jnp.concatenate(ys, axis=1)
            mu = _head_sum(y, ones_bd) * inv
            yc = y - mu
            var = _head_sum(yc * yc, ones_bd) * inv
            gn = yc * lax.rsqrt(var + GN_EPS) * gng_ref[...] + gnb_ref[...]
            bonus = _head_sum(r * kh * rk_ref[...], ones_bd) * v
            o_ref[0, rows, :] = gn + bonus


def _rwkv_scan(r, v, kk, lw, kh, bvec, r_k, gn_g, gn_b, *, n_lat_tokens):
    bsz, t_all, rw = r.shape
    C = SCAN_BLOCK
    n_chunks = t_all // C
    n_ctx = (t_all - n_lat_tokens) // C
    n_lat = n_lat_tokens // C
    W = SCAN_GROUP * HEAD_DIM

    blk_f = lambda c: jnp.where(c < n_ctx, n_lat + c, c - n_ctx)
    blk_b = lambda c: n_chunks - 1 - c
    ones_bd = (np.arange(LANE)[:, None] // HEAD_DIM == np.arange(LANE)[None, :] // HEAD_DIM)
    ones_bd = jnp.asarray(ones_bd, BF16)
    shared = lambda blk: pl.BlockSpec((1, C, rw), lambda b, c: (b, blk(c), 0))
    perdir = lambda d, blk: pl.BlockSpec((1, 1, C, rw), lambda b, c: (d, b, blk(c), 0))
    vec = pl.BlockSpec((1, rw), lambda b, c: (0, 0))
    out = jax.ShapeDtypeStruct((bsz, t_all, rw), F32)
    return pl.pallas_call(
        _scan_kernel,
        grid=(bsz, n_chunks),
        in_specs=[shared(blk_f), shared(blk_f), shared(blk_f), perdir(0, blk_f), perdir(0, blk_f), perdir(0, blk_f),
                  shared(blk_b), shared(blk_b), shared(blk_b), perdir(1, blk_b), perdir(1, blk_b), perdir(1, blk_b),
                  vec, vec, vec, pl.BlockSpec((LANE, LANE), lambda b, c: (0, 0))],
        out_specs=[shared(blk_f), shared(blk_b)],
        out_shape=[out, out],
        scratch_shapes=[pltpu.VMEM((2, rw // W, W, W), F32)],
        compiler_params=_cparams(("arbitrary", "arbitrary")),
        name="rwkv_scan",
    )(r, v, kk, lw, kh, bvec, r, v, kk, lw, kh, bvec,
      r_k.reshape(1, rw), gn_g.reshape(1, rw), gn_b.reshape(1, rw), ones_bd)


def _fft_tables(L):
    n2 = FFT_N2
    n_all = 2 * L
    n1 = n_all // n2
    nh = n1 // 2
    two_pi = 2.0 * np.pi
    kh = -(-(nh + 1) // FFT_UNROLL) * FFT_UNROLL
    k1 = np.arange(kh)
    pair_w = np.where(k1 > nh, 0.0, np.where((k1 == 0) | (k1 == nh), 1.0, 2.0))
    th1 = two_pi * np.outer(k1, np.arange(nh)) / n1
    f1 = np.concatenate([np.cos(th1), -np.sin(th1)], 0)
    th2 = two_pi * np.outer(np.arange(n2), np.arange(n2)) / n2
    fr, fi = np.cos(th2), -np.sin(th2)
    a3 = np.block([[fr, -fi], [fi, fr]])
    b3 = np.block([[-fi, -fr], [fr, -fi]])
    tht = two_pi * np.outer(k1, np.arange(n2)) / n_all
    twr = np.tile(np.cos(tht), (1, 2))
    twi = np.tile(-np.sin(tht), (1, 2))
    gr, gi = np.cos(th2), np.sin(th2)
    m3i = np.block([[gr, -gi], [gi, gr]]) / n_all
    th4 = two_pi * np.outer(np.arange(nh), k1) / n1
    cos4, sin4 = np.cos(th4) * pair_w, np.sin(th4) * pair_w
    ca = np.concatenate([cos4, -sin4], 1)
    cb = np.concatenate([-sin4, -cos4], 1)
    th5 = two_pi * np.outer(np.arange(n2), k1) / n_all
    t2r, t2i = np.tile(np.cos(th5), (1, 2)), np.tile(np.sin(th5), (1, 2))
    f = lambda a: jnp.asarray(a, F32)
    return dict(f1=f(f1), a3=f(a3), b3=f(b3), twr=f(twr), twi=f(twi),
                m3i=f(m3i), ca=f(ca), cb=f(cb), t2r=f(t2r), t2i=f(t2i))


def _fft_stage1(src_ref, f1_ref, s_ref):
    kh2, nh = f1_ref.shape
    n2 = FFT_N2
    f1 = f1_ref[...].astype(BF16)

    def body(nlo, carry):
        xin = src_ref[pl.ds(nlo, nh, stride=n2), :]
        s_ref[pl.ds(pl.multiple_of(nlo * kh2, SUB), kh2), :] = _dot(f1, xin)
        return carry

    lax.fori_loop(0, n2, body, 0, unroll=FFT_UNROLL)


def _fft_mid_matrix(k1, a3_ref, b3_ref, twr_ref, twi_ref):
    tr = twr_ref[pl.ds(k1, 1), :]
    ti = twi_ref[pl.ds(k1, 1), :]
    return (a3_ref[...] * tr + b3_ref[...] * ti).astype(BF16)


def _fft_load_k1(s_ref, k1, kh2):
    n2 = FFT_N2
    re = s_ref[pl.ds(k1, n2, stride=kh2), :]
    im = s_ref[pl.ds(kh2 // 2 + k1, n2, stride=kh2), :]
    return jnp.concatenate([re, im], axis=0)


def _hyena_kernel(x0_ref, x1_ref, v_ref, bias_ref, kf_ref,
                  f1_ref, a3_ref, b3_ref, twr_ref, twi_ref, m3i_ref, ca_ref, cb_ref, t2r_ref, t2i_ref,
                  o_ref, z_scr, y_scr, s_scr, q_scr):
    n2 = FFT_N2
    kh2, nh = f1_ref.shape
    kh = kh2 // 2
    z_scr[...] = v_ref[0].astype(F32) * x1_ref[0].astype(F32)
    _fft_stage1(z_scr, f1_ref, s_scr)
    m3i = m3i_ref[...].astype(BF16)

    def mid(k1, carry):
        m = _fft_mid_matrix(k1, a3_ref, b3_ref, twr_ref, twi_ref)
        cc = _dot(m, _fft_load_k1(s_scr, k1, kh2))
        kf = kf_ref[0, k1]
        cr, ci = cc[:n2], cc[n2:]
        kr, ki = kf[:n2], kf[n2:]
        pp = jnp.concatenate([cr * kr - ci * ki, cr * ki + ci * kr], axis=0)
        q = _dot(m3i, pp)
        q_scr[pl.ds(k1, n2, stride=kh2), :] = q[:n2]
        q_scr[pl.ds(kh + k1, n2, stride=kh2), :] = q[n2:]
        return carry

    lax.fori_loop(0, kh, mid, 0, unroll=FFT_UNROLL)

    def last(m2, carry):
        f4 = ca_ref[...] * t2r_ref[pl.ds(m2, 1), :] + cb_ref[...] * t2i_ref[pl.ds(m2, 1), :]
        qm = q_scr[pl.ds(pl.multiple_of(m2 * kh2, SUB), kh2), :]
        y_scr[pl.ds(m2, nh, stride=n2), :] = _dot(f4, qm)
        return carry

    lax.fori_loop(0, n2, last, 0, unroll=FFT_UNROLL)
    z = z_scr[...]
    o_ref[0] = (y_scr[...] + z * bias_ref[...]) * x0_ref[0].astype(F32)


def _hyena_conv(p, hy_bias, kf, tabs, *, hy_col0, hy_dim, ct):
    bsz, L, _ = p.shape
    n_tiles = hy_dim // ct
    blk0 = hy_col0 // ct
    sec = hy_dim // ct
    xspec = lambda s: pl.BlockSpec((1, L, ct), lambda t, b: (b, 0, blk0 + s * sec + t))
    tab_list = [tabs[k] for k in ("f1", "a3", "b3", "twr", "twi", "m3i", "ca", "cb", "t2r", "t2i")]
    full = lambda a: pl.BlockSpec(a.shape, lambda t, b: (0,) * a.ndim)
    kh2 = tabs["f1"].shape[0]
    return pl.pallas_call(
        _hyena_kernel,
        grid=(n_tiles, bsz),
        in_specs=[xspec(0), xspec(1), xspec(2),
                  pl.BlockSpec((1, ct), lambda t, b: (0, t)),
                  pl.BlockSpec((1,) + kf.shape[1:], lambda t, b: (t, 0, 0, 0), pipeline_mode=pl.Buffered(1))]
                 + [full(a) for a in tab_list],
        out_specs=pl.BlockSpec((1, L, ct), lambda t, b: (b, 0, t)),
        out_shape=jax.ShapeDtypeStruct((bsz, L, hy_dim), F32),
        scratch_shapes=[pltpu.VMEM((L, ct), F32), pltpu.VMEM((L, ct), F32),
                        pltpu.VMEM((FFT_N2 * kh2, ct), F32), pltpu.VMEM((FFT_N2 * kh2, ct), F32)],
        compiler_params=_cparams(("arbitrary", "arbitrary")),
        name="hyena_conv",
    )(p, p, p, hy_bias.reshape(1, -1), kf, *tab_list)


def _filter_kernel(z_ref, t_ref, w1_ref, b1_ref, w2_ref, b2_ref, w3f_ref, w3b_ref, freq_ref, delta_ref,
                   f1_ref, a3_ref, b3_ref, twr_ref, twi_ref, kf_ref, h_scr, hf_scr, hb_scr, sf_scr, sb_scr):
    n2 = FFT_N2
    kh2 = f1_ref.shape[0]

    @pl.when(pl.program_id(0) == 0)
    def _():
        freq = freq_ref[...]
        h1 = jnp.sin(freq * (_dot3(z_ref[...], w1_ref[...]) + b1_ref[...]))
        h_scr[...] = jnp.sin(freq * (_dot3(h1, w2_ref[...]) + b2_ref[...]))

    h = h_scr[...]
    window = jnp.exp(-t_ref[...] * delta_ref[...])
    hf_scr[...] = _dot3(h, w3f_ref[...]) * window
    hb = _dot3(h, w3b_ref[...]) * window
    rows = lax.broadcasted_iota(jnp.int32, hb.shape, 0)
    hb_scr[...] = jnp.where(rows == 0, 0.0, hb)
    _fft_stage1(hf_scr, f1_ref, sf_scr)
    _fft_stage1(hb_scr, f1_ref, sb_scr)

    def mid(k1, carry):
        m = _fft_mid_matrix(k1, a3_ref, b3_ref, twr_ref, twi_ref)
        cf = _dot(m, _fft_load_k1(sf_scr, k1, kh2))
        cb = _dot(m, _fft_load_k1(sb_scr, k1, kh2))
        kf_ref[0, k1] = jnp.concatenate([cf[:n2] + cb[:n2], cf[n2:] - cb[n2:]], axis=0)
        return carry

    lax.fori_loop(0, kh2 // 2, mid, 0, unroll=FFT_UNROLL)


def _filter_spectrum(zpos, tcol, w1, b1, w2, b2, w3, freq, deltas, tabs, *, ct):
    L = zpos.shape[0]
    hy_dim = deltas.shape[1]
    n_tiles = hy_dim // ct
    kh2 = tabs["f1"].shape[0]
    kh = kh2 // 2
    tab_list = [tabs[k] for k in ("f1", "a3", "b3", "twr", "twi")]
    full = lambda a: pl.BlockSpec(a.shape, lambda t: (0,) * a.ndim)
    row = lambda a: a.reshape(1, -1)
    small = [zpos, tcol, w1, row(b1), w2, row(b2)]
    return pl.pallas_call(
        _filter_kernel,
        grid=(n_tiles,),
        in_specs=[full(a) for a in small]
                 + [pl.BlockSpec((w3.shape[0], ct), lambda t: (0, t)),
                    pl.BlockSpec((w3.shape[0], ct), lambda t: (0, n_tiles + t)),
                    full(row(freq)),
                    pl.BlockSpec((1, ct), lambda t: (0, t))]
                 + [full(a) for a in tab_list],
        out_specs=pl.BlockSpec((1, kh, 2 * FFT_N2, ct), lambda t: (t, 0, 0, 0)),
        out_shape=jax.ShapeDtypeStruct((n_tiles, kh, 2 * FFT_N2, ct), F32),
        scratch_shapes=[pltpu.VMEM((L, w2.shape[1]), F32), pltpu.VMEM((L, ct), F32), pltpu.VMEM((L, ct), F32),
                        pltpu.VMEM((FFT_N2 * kh2, ct), F32), pltpu.VMEM((FFT_N2 * kh2, ct), F32)],
        compiler_params=_cparams(("arbitrary",)),
        name="hyena_filter",
    )(*small, w3, w3, row(freq), deltas, *tab_list)


def _outproj_kernel(of_ref, ob_ref, gate_ref, ohy_ref, x_ref, er_ref, ec_ref, g1_ref, w_ref, lng_ref, lnb_ref,
                    o_ref):
    rw = of_ref.shape[2]
    a_rw = ((of_ref[0] + ob_ref[0]) * gate_ref[0].astype(F32)).astype(BF16)
    mix = jnp.dot(a_rw, w_ref[0:rw, :], preferred_element_type=F32)
    mix = mix + jnp.dot(ohy_ref[0].astype(BF16), w_ref[rw:, :], preferred_element_type=F32)
    gmix = g1_ref[0] * mix
    tm = x_ref.shape[1]
    for q in range(tm // GRID_W):
        rows = slice(q * GRID_W, (q + 1) * GRID_W)
        xv = _add_pos(x_ref[0, rows, :], er_ref[q:q + 1, :], ec_ref[...])
        o_ref[0, rows, :] = _layer_norm(ALPHA * xv + gmix[rows], lng_ref[...], lnb_ref[...]).astype(o_ref.dtype)


def _outproj(o_f, o_b, gate, o_hy, x, er, ec, mod3, w_out, ln_g, ln_b, *, tm):
    bsz, L, d = x.shape
    rw = gate.shape[2]
    hy = o_hy.shape[2]
    gr = tm // GRID_W
    return pl.pallas_call(
        _outproj_kernel,
        grid=(bsz, L // tm),
        in_specs=[pl.BlockSpec((1, tm, rw), lambda b, i: (b, i, 0)),
                  pl.BlockSpec((1, tm, rw), lambda b, i: (b, i, 0)),
                  pl.BlockSpec((1, tm, rw), lambda b, i: (b, i, 0)),
                  pl.BlockSpec((1, tm, hy), lambda b, i: (b, i, 0)),
                  pl.BlockSpec((1, tm, d), lambda b, i: (b, i, 0)),
                  pl.BlockSpec((gr, d // 2), lambda b, i: (i, 0)),
                  pl.BlockSpec((GRID_W, d // 2), lambda b, i: (0, 0)),
                  pl.BlockSpec((1, 1, d), lambda b, i: (b, 0, 2)),
                  pl.BlockSpec((d, d), lambda b, i: (0, 0)),
                  pl.BlockSpec((1, d), lambda b, i: (0, 0)),
                  pl.BlockSpec((1, d), lambda b, i: (0, 0))],
        out_specs=pl.BlockSpec((1, tm, d), lambda b, i: (b, i, 0)),
        out_shape=jax.ShapeDtypeStruct((bsz, L, d), BF16),
        compiler_params=_cparams(("arbitrary", "arbitrary")),
        name="outproj_ln",
    )(o_f, o_b, gate, o_hy, x, er, ec, mod3, w_out, ln_g.reshape(1, d), ln_b.reshape(1, d))


def _ffn_kernel(x_ref, sh_ref, sc_ref, g2_ref, w1_ref, w3_ref, w2_ref, lng_ref, lnb_ref, o_ref, a_scr):
    f = pl.program_id(2)
    tm = x_ref.shape[1]

    @pl.when(f == 0)
    def _():
        a_scr[...] = (x_ref[0].astype(F32) * (1.0 + sc_ref[0]) + sh_ref[0]).astype(BF16)
        o_ref[0] = jnp.zeros((tm, o_ref.shape[2]), F32)

    a = a_scr[...]
    h1 = jnp.dot(a, w1_ref[...], preferred_element_type=F32)
    h3 = jnp.dot(a, w3_ref[...], preferred_element_type=F32)
    h = (h1 * _sigmoid(h1) * h3).astype(BF16)
    o_ref[0] += jnp.dot(h, w2_ref[...], preferred_element_type=F32)

    @pl.when(f == pl.num_programs(2) - 1)
    def _():
        for r0 in range(0, tm, FFN_LN_ROWS):
            rows = slice(r0, r0 + FFN_LN_ROWS)
            res = ALPHA * x_ref[0, rows, :].astype(F32) + g2_ref[0] * o_ref[0, rows, :]
            o_ref[0, rows, :] = _layer_norm(res, lng_ref[...], lnb_ref[...])


def _ffn(x1, mod3, w1, w3, w2, ln_g, ln_b, *, tm, tf):
    bsz, L, d = x1.shape
    dff = w1.shape[1]
    return pl.pallas_call(
        _ffn_kernel,
        grid=(bsz, L // tm, dff // tf),
        in_specs=[pl.BlockSpec((1, tm, d), lambda b, i, f: (b, i, 0)),
                  pl.BlockSpec((1, 1, d), lambda b, i, f: (b, 0, 3)),
                  pl.BlockSpec((1, 1, d), lambda b, i, f: (b, 0, 4)),
                  pl.BlockSpec((1, 1, d), lambda b, i, f: (b, 0, 5)),
                  pl.BlockSpec((d, tf), lambda b, i, f: (0, f)),
                  pl.BlockSpec((d, tf), lambda b, i, f: (0, f)),
                  pl.BlockSpec((tf, d), lambda b, i, f: (f, 0)),
                  pl.BlockSpec((1, d), lambda b, i, f: (0, 0)),
                  pl.BlockSpec((1, d), lambda b, i, f: (0, 0))],
        out_specs=pl.BlockSpec((1, tm, d), lambda b, i, f: (b, i, 0)),
        out_shape=jax.ShapeDtypeStruct((bsz, L, d), F32),
        scratch_shapes=[pltpu.VMEM((tm, d), BF16)],
        compiler_params=_cparams(("arbitrary", "arbitrary", "arbitrary")),
        name="ffn_ln",
    )(x1, mod3, mod3, mod3, w1, w3, w2, ln_g.reshape(1, d), ln_b.reshape(1, d))


def _pos_tables(rows, d):
    quarter = d // 4
    omega = 1.0 / (10000.0 ** (jnp.arange(quarter, dtype=F32) / quarter))
    er = jnp.arange(rows, dtype=F32)[:, None] * omega
    ec = jnp.arange(GRID_W, dtype=F32)[:, None] * omega
    er = jnp.concatenate([jnp.sin(er), jnp.cos(er)], -1)
    ec = jnp.concatenate([jnp.sin(ec), jnp.cos(ec)], -1)
    return er, ec


def _filter_positions(L):
    pos = jnp.arange(L, dtype=F32)[:, None]
    t = jnp.linspace(0.0, 1.0, L, dtype=F32)[:, None]
    bands = jnp.linspace(1e-4, N_BANDS - 1, N_BANDS, dtype=F32)[None, :]
    ang = 2.0 * math.pi * bands * pos / L
    z = jnp.concatenate([t, jnp.cos(ang), -jnp.sin(ang)], -1)
    return z, t


def _pad_cols(a, width):
    return jnp.pad(a, ((0, 0), (0, width - a.shape[1])))


def _pad_rows(a, height):
    return jnp.pad(a, ((0, height - a.shape[0]), (0, 0)))


def kernel(x, c, ctx, c_ctx, w_ada, b_ada, w_in, conv_rw, conv_hy, conv_hy_b, w0_f, w_up_f, a0_f, a_up_f, w0_b, w_up_b, a0_b, a_up_b, k_k, k_a, r_k, g_up, gn_g, gn_b, filt_w1, filt_b1, filt_w2, filt_b2, filt_w3, sin_freq, hy_bias, w_out, ln1_g, ln1_b, ffn_w1, ffn_w3, ffn_w2, ln2_g, ln2_b):
    bsz, L, d = x.shape
    rw = k_k.shape[1]
    hy = hy_bias.shape[1]
    dl, il, gl = w_up_f.shape[1], a_up_f.shape[1], g_up.shape[1]
    lo_w = 4 * LANE
    rkv_cols = 3 * rw

    wi = w_in[0]
    lo0 = rkv_cols

    def lora_layout(a):
        return jnp.concatenate([_pad_cols(a[:, lo0:lo0 + dl], LANE),
                                _pad_cols(a[:, lo0 + dl:lo0 + dl + il], LANE),
                                _pad_cols(a[:, lo0 + dl + il:lo0 + dl + il + gl], 2 * LANE)], axis=1)

    w_lat = _weight_layout(wi, rkv_cols=rkv_cols, lora_widths=(dl, il, gl), hy_cols=3 * hy)
    cw_lat = jnp.concatenate([conv_rw[0][:, :rkv_cols], conv_hy[0], lora_layout(conv_rw[0])], axis=1)
    cb_lat = jnp.concatenate([jnp.zeros((rkv_cols,), F32), conv_hy_b[0], jnp.zeros((lo_w,), F32)])[None, :]
    wup = _pad_rows(jnp.concatenate([w_up_f[0], w_up_b[0]], axis=1), LANE)
    aup = _pad_rows(jnp.concatenate([a_up_f[0], a_up_b[0]], axis=1), LANE)
    gup = _pad_rows(g_up[0], 2 * LANE)
    w0 = jnp.concatenate([w0_f[0], w0_b[0]])
    a0 = jnp.concatenate([a0_f[0], a0_b[0]])

    cc = _pad_rows(jnp.concatenate([c, c_ctx[None, :]], axis=0), SUB)
    mod = _adaln(cc, w_ada[0], b_ada[0])
    mod3 = mod.reshape(SUB, 1, 6 * d)

    er, ec = _pos_tables(L // GRID_W, d)

    tn = lo_w
    p_lat = _inproj(x, er, ec, mod3, w_lat, cw_lat, cb_lat, tm=1024, tn=tn, add_pos=True, ctx_row=None)
    ctx_tiles = list(range(rkv_cols // tn)) + [(rkv_cols + 3 * hy) // tn]
    p_ctx = _inproj(ctx, er, ec, mod3, w_lat, cw_lat, cb_lat, tm=ctx.shape[1], tn=tn, add_pos=False,
                    ctx_row=bsz, col_tiles=ctx_tiles)

    lo_cols = (rkv_cols + 3 * hy, rkv_cols + 3 * hy + lo_w)
    r, v, kk, gate, lw, kh, bvec = _rwkv_prep(p_lat, p_ctx, wup, aup, gup, w0, a0,
                                              k_k[0], k_a[0], rw=rw, rkv_cols=rkv_cols, lo_cols=lo_cols)
    o_f, o_b = _rwkv_scan(r, v, kk, lw, kh, bvec, r_k[0], gn_g[0], gn_b[0], n_lat_tokens=L)

    tabs = _fft_tables(L)
    zpos, tcol = _filter_positions(L)
    max_decay = math.log(DECAY_TARGET) / FAST_DECAY_PCT
    min_decay = math.log(DECAY_TARGET) / SLOW_DECAY_PCT
    deltas = jnp.abs(jnp.linspace(min_decay, max_decay, hy, dtype=F32))[None, :]
    kf = _filter_spectrum(_pad_cols(zpos, LANE), tcol, _pad_rows(filt_w1[0], LANE), filt_b1[0], filt_w2[0],
                          filt_b2[0], filt_w3[0], sin_freq[0], deltas, tabs, ct=LANE)
    o_hy = _hyena_conv(p_lat, hy_bias[0], kf, tabs, hy_col0=rkv_cols, hy_dim=hy, ct=LANE)

    x1 = _outproj(o_f, o_b, gate, o_hy, x, er, ec, mod3, w_out[0].astype(BF16), ln1_g[0], ln1_b[0], tm=512)
    return _ffn(x1, mod3, ffn_w1[0].astype(BF16), ffn_w3[0].astype(BF16), ffn_w2[0].astype(BF16),
                ln2_g[0], ln2_b[0], tm=1024, tf=512)
```

```python
import functools
import math

import numpy as np
import jax
import jax.numpy as jnp
from jax import lax
from jax.experimental import pallas as pl
from jax.experimental.pallas import tpu as pltpu

F32 = jnp.float32
BF16 = jnp.bfloat16

GRID_W = 64
HEAD_DIM = 64
N_BANDS = 16
DECAY_TARGET = 1e-2
FAST_DECAY_PCT = 0.3
SLOW_DECAY_PCT = 1.5
LN_EPS = 1e-5
GN_EPS = 64e-5
ALPHA = 2.0 ** 0.25

LANE = 128
SUB = 8
VMEM_LIMIT = 56 * 1024 * 1024

SCAN_CHUNK = 64
SCAN_GROUP = 2
SCAN_BLOCK = 128
FFN_LN_ROWS = 256
FFT_N2 = 64
FFT_UNROLL = 8


def _cparams(sem):
    return pltpu.CompilerParams(dimension_semantics=sem, vmem_limit_bytes=VMEM_LIMIT)


def _dot(a, b, dims=(((1,), (0,)), ((), ()))):
    return lax.dot_general(a.astype(BF16), b.astype(BF16), dims, preferred_element_type=F32)


def _dot_nt(a, b):
    return _dot(a, b, (((1,), (1,)), ((), ())))


def _dot_tn(a, b):
    return _dot(a, b, (((0,), (0,)), ((), ())))


def _split2(a):
    hi = a.astype(BF16)
    lo = (a - hi.astype(F32)).astype(BF16)
    return hi, lo


def _dot3(a, b, b_split=None):
    ah, al = _split2(a)
    bh, bl = _split2(b) if b_split is None else b_split
    return _dot(ah, bh) + (_dot(al, bh) + _dot(ah, bl))


def _dot_exact_lhs(a_exact, b):
    b1, b2 = _split2(b)
    return _dot(a_exact, b1) + _dot(a_exact, b2)


def _sigmoid(x):
    return 0.5 * jnp.tanh(0.5 * x) + 0.5


def _layer_norm(h, g, b):
    mu = jnp.mean(h, -1, keepdims=True)
    hc = h - mu
    var = jnp.mean(hc * hc, -1, keepdims=True)
    return hc * lax.rsqrt(var + LN_EPS) * g + b


def _adaln_kernel(c_ref, w_ref, b_ref, o_ref):
    cv = c_ref[...]
    a = cv * _sigmoid(cv)
    o_ref[...] = _dot3(a, w_ref[...]) + b_ref[...]


def _adaln(cc, w, b, tn=1024):
    m, d = cc.shape
    n = w.shape[1]
    return pl.pallas_call(
        _adaln_kernel,
        grid=(n // tn,),
        in_specs=[pl.BlockSpec((m, d), lambda j: (0, 0)),
                  pl.BlockSpec((d, tn), lambda j: (0, j)),
                  pl.BlockSpec((1, tn), lambda j: (0, j))],
        out_specs=pl.BlockSpec((m, tn), lambda j: (0, j)),
        out_shape=jax.ShapeDtypeStruct((m, n), F32),
        compiler_params=_cparams(("arbitrary",)),
        name="adaln",
    )(cc, w, b.reshape(1, n))


def _wlayout_kernel(wt_ref, o_ref, *, rkv_cols, lora_widths, hy_cols):
    tk = wt_ref.shape[1]

    def put(dst, rows):
        o_ref[:, dst:dst + rows.shape[0]] = jnp.transpose(rows).astype(o_ref.dtype)

    put(0, wt_ref[0:rkv_cols, :])
    src = rkv_cols
    pieces = []
    for width in lora_widths:
        padded = -(-width // LANE) * LANE
        pieces.append(wt_ref[src:src + width, :])
        if padded > width:
            pieces.append(jnp.zeros((padded - width, tk), F32))
        src += width
    put(rkv_cols, wt_ref[src:src + hy_cols, :])
    put(rkv_cols + hy_cols, jnp.concatenate(pieces, axis=0))


def _weight_layout(wt, *, rkv_cols, lora_widths, hy_cols, tk=256):
    n_src, k = wt.shape
    n_dst = rkv_cols + hy_cols + sum(-(-wd // LANE) * LANE for wd in lora_widths)
    return pl.pallas_call(
        functools.partial(_wlayout_kernel, rkv_cols=rkv_cols, lora_widths=lora_widths, hy_cols=hy_cols),
        grid=(k // tk,),
        in_specs=[pl.BlockSpec((n_src, tk), lambda i: (0, i))],
        out_specs=pl.BlockSpec((tk, n_dst), lambda i: (i, 0)),
        out_shape=jax.ShapeDtypeStruct((k, n_dst), BF16),
        compiler_params=_cparams(("arbitrary",)),
        name="weight_layout",
    )(wt)


HALO = 16


def _add_pos(xv, er_row, ec_rows):
    half = ec_rows.shape[1]
    return jnp.concatenate([xv[:, :half] + er_row, xv[:, half:] + ec_rows], axis=1)


def _inproj_kernel(x_ref, xp_ref, xn_ref, er_ref, ec_ref, sh_ref, sc_ref, w_ref, cw_ref, cb_ref, o_ref, a_scr,
                   *, add_pos):
    i = pl.program_id(1)
    n_i = pl.num_programs(1)
    tm = x_ref.shape[1]
    gr = tm // GRID_W

    @pl.when(pl.program_id(2) == 0)
    def _():
        scale = 1.0 + sc_ref[0]
        shift = sh_ref[0]

        def put(rows, xv, keep=None):
            a = xv * scale + shift
            if keep is not None:
                a = jnp.where(keep, a, 0.0)
            a_scr[rows, :] = a.astype(BF16)

        has_prev = i > 0
        has_next = i < n_i - 1
        lo_rows, hi_rows = slice(0, HALO), slice(HALO + tm, 2 * HALO + tm)
        if add_pos:
            n_er = er_ref.shape[0]
            for q in range(gr):
                rows = slice(q * GRID_W, (q + 1) * GRID_W)
                put(slice(HALO + q * GRID_W, HALO + (q + 1) * GRID_W),
                    _add_pos(x_ref[0, rows, :], er_ref[pl.ds(i * gr + q, 1), :], ec_ref[...]))
            put(lo_rows, _add_pos(xp_ref[0], er_ref[pl.ds(jnp.maximum(i * gr - 1, 0), 1), :],
                                  ec_ref[GRID_W - HALO:, :]), has_prev)
            put(hi_rows, _add_pos(xn_ref[0], er_ref[pl.ds(jnp.minimum((i + 1) * gr, n_er - 1), 1), :],
                                  ec_ref[:HALO, :]), has_next)
        else:
            put(lo_rows, xp_ref[0], has_prev)
            put(slice(HALO, HALO + tm), x_ref[0])
            put(hi_rows, xn_ref[0], has_next)

    n_rows = tm + 2 * HALO
    half = w_ref.shape[1] // 2
    for h in range(2):
        cols = slice(h * half, (h + 1) * half)
        res = jnp.dot(a_scr[...], w_ref[:, cols], preferred_element_type=F32)
        out = (pltpu.roll(res, 1, 0) * cw_ref[0:1, cols] + res * cw_ref[1:2, cols]
               + pltpu.roll(res, n_rows - 1, 0) * cw_ref[2:3, cols] + cb_ref[:, cols])
        o_ref[0, :, cols] = out[HALO:HALO + tm].astype(o_ref.dtype)


def _inproj(x, er, ec, mod3, w, cw, cb, *, tm, tn, add_pos, ctx_row, col_tiles=None):
    bsz, lx, d = x.shape
    if col_tiles is None:
        n = w.shape[1]
        col = lambda j: j
    else:
        n = len(col_tiles) * tn
        run = col_tiles[:-1]
        assert run == list(range(len(run)))
        col = lambda j: jnp.where(j < len(run), j, col_tiles[-1])
    hb = tm // HALO
    n_hb = lx // HALO
    row = (lambda b: ctx_row) if ctx_row is not None else (lambda b: b)
    return pl.pallas_call(
        functools.partial(_inproj_kernel, add_pos=add_pos),
        grid=(bsz, lx // tm, n // tn),
        in_specs=[pl.BlockSpec((1, tm, d), lambda b, i, j: (b, i, 0)),
                  pl.BlockSpec((1, HALO, d), lambda b, i, j: (b, jnp.maximum(i * hb - 1, 0), 0)),
                  pl.BlockSpec((1, HALO, d), lambda b, i, j: (b, jnp.minimum((i + 1) * hb, n_hb - 1), 0)),
                  pl.BlockSpec(er.shape, lambda b, i, j: (0, 0)),
                  pl.BlockSpec(ec.shape, lambda b, i, j: (0, 0)),
                  pl.BlockSpec((1, 1, d), lambda b, i, j: (row(b), 0, 0)),
                  pl.BlockSpec((1, 1, d), lambda b, i, j: (row(b), 0, 1)),
                  pl.BlockSpec((d, tn), lambda b, i, j: (0, col(j))),
                  pl.BlockSpec((3, tn), lambda b, i, j: (0, col(j))),
                  pl.BlockSpec((1, tn), lambda b, i, j: (0, col(j)))],
        out_specs=pl.BlockSpec((1, tm, tn), lambda b, i, j: (b, i, j)),
        out_shape=jax.ShapeDtypeStruct((bsz, lx, n), BF16),
        scratch_shapes=[pltpu.VMEM((tm + 2 * HALO, d), BF16)],
        compiler_params=_cparams(("arbitrary", "arbitrary", "arbitrary")),
        name="inproj_pos" if add_pos else "inproj_ctx",
    )(x, x, x, er, ec, mod3, mod3, w, cw, cb)


def _head_sum(x, ones_bd):
    rows, groups = x.shape[0], x.shape[1] // LANE
    xs = jnp.concatenate([x[:, j * LANE:(j + 1) * LANE] for j in range(groups)], axis=0)
    hi, lo = _split2(xs)
    s = _dot(jnp.concatenate([hi, lo], axis=0), ones_bd)
    s = s[:groups * rows] + s[groups * rows:]
    return jnp.concatenate([s[j * rows:(j + 1) * rows] for j in range(groups)], axis=1)


def _prep_kernel(p_rkv, p_lo, pc_rkv, pc_lo, wup, aup, gup, w0, a0, kk_w, ka_w, ones_ref,
                 r_out, v_out, kk_out, gate_out, lw_out, kh_out, b_out, *, n_lat):
    is_ctx = pl.program_id(1) == n_lat
    rw = kk_out.shape[2]
    ones_bd = ones_ref[...]
    wup_split = _split2(wup[...])
    aup_split = _split2(aup[...])
    gup_bf = gup[...].astype(BF16)

    def row_block(rows):
        def rkv(s):
            cols = slice(s * rw, (s + 1) * rw)
            return jnp.where(is_ctx, pc_rkv[0, rows, cols], p_rkv[0, rows, cols])

        r_out[0, rows, :] = rkv(0)
        v_out[0, rows, :] = rkv(2)
        k = rkv(1).astype(F32)
        kkr = k * kk_w[...]
        nrm = jnp.sqrt(_head_sum(kkr * kkr, ones_bd))
        kk = kkr / jnp.maximum(nrm, 1e-12)
        kk_out[0, rows, :] = kk.astype(kk_out.dtype)

        lo = jnp.where(is_ctx, pc_lo[0, rows, :], p_lo[0, rows, :]).astype(F32)
        tw = jnp.tanh(lo[:, 0:LANE])
        al = lo[:, LANE:2 * LANE]
        gl = _sigmoid(lo[:, 2 * LANE:])
        gate_out[0, rows, :] = _dot(gl, gup_bf).astype(gate_out.dtype)
        wx = _dot3(tw, None, wup_split) + w0[...]
        ax = _dot3(al, None, aup_split) + a0[...]
        for d in range(2):
            cols = slice(d * rw, (d + 1) * rw)
            lw_out[d, 0, rows, :] = -math.exp(-0.5) * _sigmoid(wx[:, cols])
            ia = _sigmoid(ax[:, cols])
            kh_out[d, 0, rows, :] = (k * (1.0 + (ia - 1.0) * ka_w[...])).astype(kh_out.dtype)
            b_out[d, 0, rows, :] = (kk * ia).astype(b_out.dtype)

    row_block(slice(0, kk_out.shape[1]))


def _rwkv_prep(p, pc, wup, aup, gup, w0, a0, k_k, k_a, *, rw, rkv_cols, lo_cols):
    bsz, lq, _ = p.shape
    ctx = pc.shape[1]
    tr = ctx
    n_lat = lq // tr
    t_all = lq + ctx
    lo_w = lo_cols[1] - lo_cols[0]
    lo_blk = lo_cols[0] // lo_w
    lo_blk_c = rkv_cols // lo_w
    lat = lambda i: jnp.minimum(i, n_lat - 1)
    ones_bd = (np.arange(LANE)[:, None] // HEAD_DIM == np.arange(LANE)[None, :] // HEAD_DIM)
    ones_bd = jnp.asarray(ones_bd, BF16)
    full = lambda a: pl.BlockSpec(a.shape, lambda b, i: (0,) * a.ndim)
    row = lambda a: a.reshape(1, -1)
    weights = [wup, aup, gup, row(w0), row(a0), row(k_k), row(k_a), ones_bd]
    shared = jax.ShapeDtypeStruct((bsz, t_all, rw), BF16)
    perdir = jax.ShapeDtypeStruct((2, bsz, t_all, rw), BF16)
    perdir_f32 = jax.ShapeDtypeStruct((2, bsz, t_all, rw), F32)
    o_shared = pl.BlockSpec((1, tr, rw), lambda b, i: (b, i, 0))
    o_dir = pl.BlockSpec((2, 1, tr, rw), lambda b, i: (0, b, i, 0))
    return pl.pallas_call(
        functools.partial(_prep_kernel, n_lat=n_lat),
        grid=(bsz, n_lat + 1),
        in_specs=[pl.BlockSpec((1, tr, rkv_cols), lambda b, i: (b, lat(i), 0)),
                  pl.BlockSpec((1, tr, lo_w), lambda b, i: (b, lat(i), lo_blk)),
                  pl.BlockSpec((1, tr, rkv_cols), lambda b, i: (b, 0, 0)),
                  pl.BlockSpec((1, tr, lo_w), lambda b, i: (b, 0, lo_blk_c))]
                 + [full(a) for a in weights],
        out_specs=[o_shared, o_shared, o_shared, o_shared, o_dir, o_dir, o_dir],
        out_shape=[shared, shared, shared, shared, perdir_f32, perdir, perdir],
        compiler_params=_cparams(("arbitrary", "arbitrary")),
        name="rwkv_prep",
    )(p, p, pc, pc, *weights)


def _scan_kernel(rf_ref, vf_ref, kkf_ref, lwf_ref, khf_ref, bf_ref, rb_ref, vb_ref, kkb_ref, lwb_ref, khb_ref,
                 bb_ref, rk_ref, gng_ref, gnb_ref, ones_ref, of_ref, ob_ref, s_scr):
    C = SCAN_CHUNK
    n_sub = rf_ref.shape[1] // C
    G = SCAN_GROUP
    W = G * HEAD_DIM
    n = G * C
    n_groups = rf_ref.shape[2] // W

    @pl.when(pl.program_id(1) == 0)
    def _():
        s_scr[...] = jnp.zeros_like(s_scr)

    bi = lax.broadcasted_iota(jnp.int32, (n, n), 0)
    bj = lax.broadcasted_iota(jnp.int32, (n, n), 1)
    same = (bi // C) == (bj // C)
    li = bi % C
    lj = bj % C
    eye = jnp.where(bi == bj, 1.0, 0.0)
    hrow = lax.broadcasted_iota(jnp.int32, (n, W), 0) // C
    hlane = lax.broadcasted_iota(jnp.int32, (n, W), 1) // HEAD_DIM
    hmask = hrow == hlane
    ti = lax.broadcasted_iota(jnp.int32, (C, C), 0)
    tj = lax.broadcasted_iota(jnp.int32, (C, C), 1)

    def rep(x):
        return jnp.where(hmask, jnp.concatenate([x] * G, axis=0), 0.0).astype(BF16)

    chains = []
    dirs = ((rf_ref, vf_ref, kkf_ref, lwf_ref, khf_ref, bf_ref, False),
            (rb_ref, vb_ref, kkb_ref, lwb_ref, khb_ref, bb_ref, True))
    per_chunk = {}
    for di, (r_ref, v_ref, kk_ref, lw_ref, kh_ref, b_ref, rev) in enumerate(dirs):
        before = (lj > li) if rev else (lj < li)
        strict = jnp.logical_and(same, before)
        incl = jnp.logical_and(same, jnp.logical_or(before, li == lj))
        tri = jnp.where((tj >= ti) if rev else (tj <= ti), 1.0, 0.0).astype(BF16)
        for t in range(n_sub):
            sub = n_sub - 1 - t if rev else t
            rows = slice(sub * C, (sub + 1) * C)
            r = r_ref[0, rows, :].astype(F32)
            v = v_ref[0, rows, :].astype(F32)
            kk = kk_ref[0, rows, :].astype(F32)
            lw = lw_ref[0, 0, rows, :]
            kh = kh_ref[0, 0, rows, :].astype(F32)
            bv = b_ref[0, 0, rows, :].astype(F32)
            cum = _dot_exact_lhs(tri, lw)
            tot = cum[0:1, :] if rev else cum[C - 1:C, :]
            e_neg = jnp.exp(-cum)
            e_end = jnp.exp(tot - cum)
            rt = r * jnp.exp(cum)
            at = -kk * jnp.exp(cum - lw)
            bt = bv * e_neg
            kt = kh * e_neg
            bp = bv * e_end
            kp = kh * e_end
            per_chunk[(di, t)] = (r, v, kh, jnp.exp(tot), rows)
            for g in range(n_groups):
                cols = slice(g * W, (g + 1) * W)
                chains.append(dict(
                    di=di, t=t, g=g, cols=cols, strict=strict, incl=incl,
                    v2=rep(v[:, cols]),
                    ar=jnp.concatenate([rep(at[:, cols]), rep(rt[:, cols])], axis=0),
                    bk=jnp.concatenate([rep(bt[:, cols]), rep(kt[:, cols])], axis=0),
                    bkp=jnp.concatenate([rep(bp[:, cols]), rep(kp[:, cols])], axis=0)))

    for ch in chains:
        ch["tt"] = _dot_nt(ch["ar"], ch["bk"])
    for ch in chains:
        ch["p"] = jnp.where(ch["strict"], ch["tt"][:n, :n], 0.0).astype(BF16)
        ch["tinv"] = eye + ch["p"].astype(F32)
    step = 2
    while step < C:
        for ch in chains:
            ch["p"] = _dot(ch["p"], ch["p"]).astype(BF16)
        for ch in chains:
            ch["tinv"] = ch["tinv"] + _dot(ch["p"], ch["tinv"])
        step *= 2
    state = {(di, g): s_scr[di, g] for di in range(2) for g in range(n_groups)}
    for t in range(n_sub):
        cur = [ch for ch in chains if ch["t"] == t]
        for ch in cur:
            ch["s0"] = state[(ch["di"], ch["g"])]
            ch["xs"] = _dot_nt(ch["ar"], ch["s0"])
        for ch in cur:
            a_ak = jnp.where(ch["strict"], ch["tt"][:n, n:], 0.0)
            ch["rhs"] = ch["xs"][:n] + _dot(a_ak, ch["v2"])
        for ch in cur:
            u = _dot(ch["tinv"], ch["rhs"])
            ch["uv"] = jnp.concatenate([u.astype(BF16), ch["v2"]], axis=0)
        for ch in cur:
            e_tot = per_chunk[(ch["di"], t)][3]
            state[(ch["di"], ch["g"])] = ch["s0"] * e_tot[:, ch["cols"]] + _dot_tn(ch["uv"], ch["bkp"])
        for ch in cur:
            a_r = jnp.concatenate([jnp.where(ch["incl"], ch["tt"][n:, :n], 0.0),
                                   jnp.where(ch["incl"], ch["tt"][n:, n:], 0.0)], axis=1)
            ch["y2"] = ch["xs"][n:] + _dot(a_r, ch["uv"])
    for (di, g), s_new in state.items():
        s_scr[di, g] = s_new

    ones_bd = ones_ref[...]
    inv = 1.0 / HEAD_DIM
    for di, o_ref in enumerate((of_ref, ob_ref)):
        for t in range(n_sub):
            r, v, kh, _, rows = per_chunk[(di, t)]
            ys = []
            for ch in chains:
                if ch["di"] == di and ch["t"] == t:
                    y = ch["y2"][0:C]
                    for h in range(1, G):
                        y = y + ch["y2"][h * C:(h + 1) * C]
                    ys.append(y)
            y = jnp.concatenate(ys, axis=1)
            mu = _head_sum(y, ones_bd) * inv
            yc = y - mu
            var = _head_sum(yc * yc, ones_bd) * inv
            gn = yc * lax.rsqrt(var + GN_EPS) * gng_ref[...] + gnb_ref[...]
            bonus = _head_sum(r * kh * rk_ref[...], ones_bd) * v
            o_ref[0, rows, :] = gn + bonus


def _rwkv_scan(r, v, kk, lw, kh, bvec, r_k, gn_g, gn_b, *, n_lat_tokens):
    bsz, t_all, rw = r.shape
    C = SCAN_BLOCK
    n_chunks = t_all // C
    n_ctx = (t_all - n_lat_tokens) // C
    n_lat = n_lat_tokens // C
    W = SCAN_GROUP * HEAD_DIM

    blk_f = lambda c: jnp.where(c < n_ctx, n_lat + c, c - n_ctx)
    blk_b = lambda c: n_chunks - 1 - c
    ones_bd = (np.arange(LANE)[:, None] // HEAD_DIM == np.arange(LANE)[None, :] // HEAD_DIM)
    ones_bd = jnp.asarray(ones_bd, BF16)
    shared = lambda blk: pl.BlockSpec((1, C, rw), lambda b, c: (b, blk(c), 0))
    perdir = lambda d, blk: pl.BlockSpec((1, 1, C, rw), lambda b, c: (d, b, blk(c), 0))
    vec = pl.BlockSpec((1, rw), lambda b, c: (0, 0))
    out = jax.ShapeDtypeStruct((bsz, t_all, rw), F32)
    return pl.pallas_call(
        _scan_kernel,
        grid=(bsz, n_chunks),
        in_specs=[shared(blk_f), shared(blk_f), shared(blk_f), perdir(0, blk_f), perdir(0, blk_f), perdir(0, blk_f),
                  shared(blk_b), shared(blk_b), shared(blk_b), perdir(1, blk_b), perdir(1, blk_b), perdir(1, blk_b),
                  vec, vec, vec, pl.BlockSpec((LANE, LANE), lambda b, c: (0, 0))],
        out_specs=[shared(blk_f), shared(blk_b)],
        out_shape=[out, out],
        scratch_shapes=[pltpu.VMEM((2, rw // W, W, W), F32)],
        compiler_params=_cparams(("arbitrary", "arbitrary")),
        name="rwkv_scan",
    )(r, v, kk, lw, kh, bvec, r, v, kk, lw, kh, bvec,
      r_k.reshape(1, rw), gn_g.reshape(1, rw), gn_b.reshape(1, rw), ones_bd)


def _fft_tables(L):
    n2 = FFT_N2
    n_all = 2 * L
    n1 = n_all // n2
    nh = n1 // 2
    two_pi = 2.0 * np.pi
    kh = -(-(nh + 1) // FFT_UNROLL) * FFT_UNROLL
    k1 = np.arange(kh)
    pair_w = np.where(k1 > nh, 0.0, np.where((k1 == 0) | (k1 == nh), 1.0, 2.0))
    th1 = two_pi * np.outer(k1, np.arange(nh)) / n1
    f1 = np.concatenate([np.cos(th1), -np.sin(th1)], 0)
    th2 = two_pi * np.outer(np.arange(n2), np.arange(n2)) / n2
    fr, fi = np.cos(th2), -np.sin(th2)
    a3 = np.block([[fr, -fi], [fi, fr]])
    b3 = np.block([[-fi, -fr], [fr, -fi]])
    tht = two_pi * np.outer(k1, np.arange(n2)) / n_all
    twr = np.tile(np.cos(tht), (1, 2))
    twi = np.tile(-np.sin(tht), (1, 2))
    gr, gi = np.cos(th2), np.sin(th2)
    m3i = np.block([[gr, -gi], [gi, gr]]) / n_all
    th4 = two_pi * np.outer(np.arange(nh), k1) / n1
    cos4, sin4 = np.cos(th4) * pair_w, np.sin(th4) * pair_w
    ca = np.concatenate([cos4, -sin4], 1)
    cb = np.concatenate([-sin4, -cos4], 1)
    th5 = two_pi * np.outer(np.arange(n2), k1) / n_all
    t2r, t2i = np.tile(np.cos(th5), (1, 2)), np.tile(np.sin(th5), (1, 2))
    f = lambda a: jnp.asarray(a, F32)
    return dict(f1=f(f1), a3=f(a3), b3=f(b3), twr=f(twr), twi=f(twi),
                m3i=f(m3i), ca=f(ca), cb=f(cb), t2r=f(t2r), t2i=f(t2i))


def _fft_stage1(src_ref, f1_ref, s_ref):
    kh2, nh = f1_ref.shape
    n2 = FFT_N2
    f1 = f1_ref[...].astype(BF16)

    def body(nlo, carry):
        xin = src_ref[pl.ds(nlo, nh, stride=n2), :]
        s_ref[pl.ds(pl.multiple_of(nlo * kh2, SUB), kh2), :] = _dot(f1, xin)
        return carry

    lax.fori_loop(0, n2, body, 0, unroll=FFT_UNROLL)


def _fft_mid_matrix(k1, a3_ref, b3_ref, twr_ref, twi_ref):
    tr = twr_ref[pl.ds(k1, 1), :]
    ti = twi_ref[pl.ds(k1, 1), :]
    return (a3_ref[...] * tr + b3_ref[...] * ti).astype(BF16)


def _fft_load_k1(s_ref, k1, kh2):
    n2 = FFT_N2
    re = s_ref[pl.ds(k1, n2, stride=kh2), :]
    im = s_ref[pl.ds(kh2 // 2 + k1, n2, stride=kh2), :]
    return jnp.concatenate([re, im], axis=0)


def _hyena_kernel(x0_ref, x1_ref, v_ref, bias_ref, kf_ref,
                  f1_ref, a3_ref, b3_ref, twr_ref, twi_ref, m3i_ref, ca_ref, cb_ref, t2r_ref, t2i_ref,
                  o_ref, z_scr, y_scr, s_scr, q_scr):
    n2 = FFT_N2
    kh2, nh = f1_ref.shape
    kh = kh2 // 2
    z_scr[...] = v_ref[0].astype(F32) * x1_ref[0].astype(F32)
    _fft_stage1(z_scr, f1_ref, s_scr)
    m3i = m3i_ref[...].astype(BF16)

    def mid(k1, carry):
        m = _fft_mid_matrix(k1, a3_ref, b3_ref, twr_ref, twi_ref)
        cc = _dot(m, _fft_load_k1(s_scr, k1, kh2))
        kf = kf_ref[0, k1]
        cr, ci = cc[:n2], cc[n2:]
        kr, ki = kf[:n2], kf[n2:]
        pp = jnp.concatenate([cr * kr - ci * ki, cr * ki + ci * kr], axis=0)
        q = _dot(m3i, pp)
        q_scr[pl.ds(k1, n2, stride=kh2), :] = q[:n2]
        q_scr[pl.ds(kh + k1, n2, stride=kh2), :] = q[n2:]
        return carry

    lax.fori_loop(0, kh, mid, 0, unroll=FFT_UNROLL)

    def last(m2, carry):
        f4 = ca_ref[...] * t2r_ref[pl.ds(m2, 1), :] + cb_ref[...] * t2i_ref[pl.ds(m2, 1), :]
        qm = q_scr[pl.ds(pl.multiple_of(m2 * kh2, SUB), kh2), :]
        y_scr[pl.ds(m2, nh, stride=n2), :] = _dot(f4, qm)
        return carry

    lax.fori_loop(0, n2, last, 0, unroll=FFT_UNROLL)
    z = z_scr[...]
    o_ref[0] = (y_scr[...] + z * bias_ref[...]) * x0_ref[0].astype(F32)


def _hyena_conv(p, hy_bias, kf, tabs, *, hy_col0, hy_dim, ct):
    bsz, L, _ = p.shape
    n_tiles = hy_dim // ct
    blk0 = hy_col0 // ct
    sec = hy_dim // ct
    xspec = lambda s: pl.BlockSpec((1, L, ct), lambda t, b: (b, 0, blk0 + s * sec + t))
    tab_list = [tabs[k] for k in ("f1", "a3", "b3", "twr", "twi", "m3i", "ca", "cb", "t2r", "t2i")]
    full = lambda a: pl.BlockSpec(a.shape, lambda t, b: (0,) * a.ndim)
    kh2 = tabs["f1"].shape[0]
    return pl.pallas_call(
        _hyena_kernel,
        grid=(n_tiles, bsz),
        in_specs=[xspec(0), xspec(1), xspec(2),
                  pl.BlockSpec((1, ct), lambda t, b: (0, t)),
                  pl.BlockSpec((1,) + kf.shape[1:], lambda t, b: (t, 0, 0, 0), pipeline_mode=pl.Buffered(1))]
                 + [full(a) for a in tab_list],
        out_specs=pl.BlockSpec((1, L, ct), lambda t, b: (b, 0, t)),
        out_shape=jax.ShapeDtypeStruct((bsz, L, hy_dim), F32),
        scratch_shapes=[pltpu.VMEM((L, ct), F32), pltpu.VMEM((L, ct), F32),
                        pltpu.VMEM((FFT_N2 * kh2, ct), F32), pltpu.VMEM((FFT_N2 * kh2, ct), F32)],
        compiler_params=_cparams(("arbitrary", "arbitrary")),
        name="hyena_conv",
    )(p, p, p, hy_bias.reshape(1, -1), kf, *tab_list)


def _filter_kernel(z_ref, t_ref, w1_ref, b1_ref, w2_ref, b2_ref, w3f_ref, w3b_ref, freq_ref, delta_ref,
                   f1_ref, a3_ref, b3_ref, twr_ref, twi_ref, kf_ref, h_scr, hf_scr, hb_scr, sf_scr, sb_scr):
    n2 = FFT_N2
    kh2 = f1_ref.shape[0]

    @pl.when(pl.program_id(0) == 0)
    def _():
        freq = freq_ref[...]
        h1 = jnp.sin(freq * (_dot3(z_ref[...], w1_ref[...]) + b1_ref[...]))
        h_scr[...] = jnp.sin(freq * (_dot3(h1, w2_ref[...]) + b2_ref[...]))

    h = h_scr[...]
    window = jnp.exp(-t_ref[...] * delta_ref[...])
    hf_scr[...] = _dot3(h, w3f_ref[...]) * window
    hb = _dot3(h, w3b_ref[...]) * window
    rows = lax.broadcasted_iota(jnp.int32, hb.shape, 0)
    hb_scr[...] = jnp.where(rows == 0, 0.0, hb)
    _fft_stage1(hf_scr, f1_ref, sf_scr)
    _fft_stage1(hb_scr, f1_ref, sb_scr)

    def mid(k1, carry):
        m = _fft_mid_matrix(k1, a3_ref, b3_ref, twr_ref, twi_ref)
        cf = _dot(m, _fft_load_k1(sf_scr, k1, kh2))
        cb = _dot(m, _fft_load_k1(sb_scr, k1, kh2))
        kf_ref[0, k1] = jnp.concatenate([cf[:n2] + cb[:n2], cf[n2:] - cb[n2:]], axis=0)
        return carry

    lax.fori_loop(0, kh2 // 2, mid, 0, unroll=FFT_UNROLL)


def _filter_spectrum(zpos, tcol, w1, b1, w2, b2, w3, freq, deltas, tabs, *, ct):
    L = zpos.shape[0]
    hy_dim = deltas.shape[1]
    n_tiles = hy_dim // ct
    kh2 = tabs["f1"].shape[0]
    kh = kh2 // 2
    tab_list = [tabs[k] for k in ("f1", "a3", "b3", "twr", "twi")]
    full = lambda a: pl.BlockSpec(a.shape, lambda t: (0,) * a.ndim)
    row = lambda a: a.reshape(1, -1)
    small = [zpos, tcol, w1, row(b1), w2, row(b2)]
    return pl.pallas_call(
        _filter_kernel,
        grid=(n_tiles,),
        in_specs=[full(a) for a in small]
                 + [pl.BlockSpec((w3.shape[0], ct), lambda t: (0, t)),
                    pl.BlockSpec((w3.shape[0], ct), lambda t: (0, n_tiles + t)),
                    full(row(freq)),
                    pl.BlockSpec((1, ct), lambda t: (0, t))]
                 + [full(a) for a in tab_list],
        out_specs=pl.BlockSpec((1, kh, 2 * FFT_N2, ct), lambda t: (t, 0, 0, 0)),
        out_shape=jax.ShapeDtypeStruct((n_tiles, kh, 2 * FFT_N2, ct), F32),
        scratch_shapes=[pltpu.VMEM((L, w2.shape[1]), F32), pltpu.VMEM((L, ct), F32), pltpu.VMEM((L, ct), F32),
                        pltpu.VMEM((FFT_N2 * kh2, ct), F32), pltpu.VMEM((FFT_N2 * kh2, ct), F32)],
        compiler_params=_cparams(("arbitrary",)),
        name="hyena_filter",
    )(*small, w3, w3, row(freq), deltas, *tab_list)


def _outproj_kernel(of_ref, ob_ref, gate_ref, ohy_ref, x_ref, er_ref, ec_ref, g1_ref, w_ref, lng_ref, lnb_ref,
                    o_ref):
    rw = of_ref.shape[2]
    a_rw = ((of_ref[0] + ob_ref[0]) * gate_ref[0].astype(F32)).astype(BF16)
    mix = jnp.dot(a_rw, w_ref[0:rw, :], preferred_element_type=F32)
    mix = mix + jnp.dot(ohy_ref[0].astype(BF16), w_ref[rw:, :], preferred_element_type=F32)
    gmix = g1_ref[0] * mix
    tm = x_ref.shape[1]
    for q in range(tm // GRID_W):
        rows = slice(q * GRID_W, (q + 1) * GRID_W)
        xv = _add_pos(x_ref[0, rows, :], er_ref[q:q + 1, :], ec_ref[...])
        o_ref[0, rows, :] = _layer_norm(ALPHA * xv + gmix[rows], lng_ref[...], lnb_ref[...]).astype(o_ref.dtype)


def _outproj(o_f, o_b, gate, o_hy, x, er, ec, mod3, w_out, ln_g, ln_b, *, tm):
    bsz, L, d = x.shape
    rw = gate.shape[2]
    hy = o_hy.shape[2]
    gr = tm // GRID_W
    return pl.pallas_call(
        _outproj_kernel,
        grid=(bsz, L // tm),
        in_specs=[pl.BlockSpec((1, tm, rw), lambda b, i: (b, i, 0)),
                  pl.BlockSpec((1, tm, rw), lambda b, i: (b, i, 0)),
                  pl.BlockSpec((1, tm, rw), lambda b, i: (b, i, 0)),
                  pl.BlockSpec((1, tm, hy), lambda b, i: (b, i, 0)),
                  pl.BlockSpec((1, tm, d), lambda b, i: (b, i, 0)),
                  pl.BlockSpec((gr, d // 2), lambda b, i: (i, 0)),
                  pl.BlockSpec((GRID_W, d // 2), lambda b, i: (0, 0)),
                  pl.BlockSpec((1, 1, d), lambda b, i: (b, 0, 2)),
                  pl.BlockSpec((d, d), lambda b, i: (0, 0)),
                  pl.BlockSpec((1, d), lambda b, i: (0, 0)),
                  pl.BlockSpec((1, d), lambda b, i: (0, 0))],
        out_specs=pl.BlockSpec((1, tm, d), lambda b, i: (b, i, 0)),
        out_shape=jax.ShapeDtypeStruct((bsz, L, d), BF16),
        compiler_params=_cparams(("arbitrary", "arbitrary")),
        name="outproj_ln",
    )(o_f, o_b, gate, o_hy, x, er, ec, mod3, w_out, ln_g.reshape(1, d), ln_b.reshape(1, d))


def _ffn_kernel(x_ref, sh_ref, sc_ref, g2_ref, w1_ref, w3_ref, w2_ref, lng_ref, lnb_ref, o_ref, a_scr):
    f = pl.program_id(2)
    tm = x_ref.shape[1]

    @pl.when(f == 0)
    def _():
        a_scr[...] = (x_ref[0].astype(F32) * (1.0 + sc_ref[0]) + sh_ref[0]).astype(BF16)
        o_ref[0] = jnp.zeros((tm, o_ref.shape[2]), F32)

    a = a_scr[...]
    h1 = jnp.dot(a, w1_ref[...], preferred_element_type=F32)
    h3 = jnp.dot(a, w3_ref[...], preferred_element_type=F32)
    h = (h1 * _sigmoid(h1) * h3).astype(BF16)
    o_ref[0] += jnp.dot(h, w2_ref[...], preferred_element_type=F32)

    @pl.when(f == pl.num_programs(2) - 1)
    def _():
        for r0 in range(0, tm, FFN_LN_ROWS):
            rows = slice(r0, r0 + FFN_LN_ROWS)
            res = ALPHA * x_ref[0, rows, :].astype(F32) + g2_ref[0] * o_ref[0, rows, :]
            o_ref[0, rows, :] = _layer_norm(res, lng_ref[...], lnb_ref[...])


def _ffn(x1, mod3, w1, w3, w2, ln_g, ln_b, *, tm, tf):
    bsz, L, d = x1.shape
    dff = w1.shape[1]
    return pl.pallas_call(
        _ffn_kernel,
        grid=(bsz, L // tm, dff // tf),
        in_specs=[pl.BlockSpec((1, tm, d), lambda b, i, f: (b, i, 0)),
                  pl.BlockSpec((1, 1, d), lambda b, i, f: (b, 0, 3)),
                  pl.BlockSpec((1, 1, d), lambda b, i, f: (b, 0, 4)),
                  pl.BlockSpec((1, 1, d), lambda b, i, f: (b, 0, 5)),
                  pl.BlockSpec((d, tf), lambda b, i, f: (0, f)),
                  pl.BlockSpec((d, tf), lambda b, i, f: (0, f)),
                  pl.BlockSpec((tf, d), lambda b, i, f: (f, 0)),
                  pl.BlockSpec((1, d), lambda b, i, f: (0, 0)),
                  pl.BlockSpec((1, d), lambda b, i, f: (0, 0))],
        out_specs=pl.BlockSpec((1, tm, d), lambda b, i, f: (b, i, 0)),
        out_shape=jax.ShapeDtypeStruct((bsz, L, d), F32),
        scratch_shapes=[pltpu.VMEM((tm, d), BF16)],
        compiler_params=_cparams(("arbitrary", "arbitrary", "arbitrary")),
        name="ffn_ln",
    )(x1, mod3, mod3, mod3, w1, w3, w2, ln_g.reshape(1, d), ln_b.reshape(1, d))


def _pos_tables(rows, d):
    quarter = d // 4
    omega = 1.0 / (10000.0 ** (jnp.arange(quarter, dtype=F32) / quarter))
    er = jnp.arange(rows, dtype=F32)[:, None] * omega
    ec = jnp.arange(GRID_W, dtype=F32)[:, None] * omega
    er = jnp.concatenate([jnp.sin(er), jnp.cos(er)], -1)
    ec = jnp.concatenate([jnp.sin(ec), jnp.cos(ec)], -1)
    return er, ec


def _filter_positions(L):
    pos = jnp.arange(L, dtype=F32)[:, None]
    t = jnp.linspace(0.0, 1.0, L, dtype=F32)[:, None]
    bands = jnp.linspace(1e-4, N_BANDS - 1, N_BANDS, dtype=F32)[None, :]
    ang = 2.0 * math.pi * bands * pos / L
    z = jnp.concatenate([t, jnp.cos(ang), -jnp.sin(ang)], -1)
    return z, t


def _pad_cols(a, width):
    return jnp.pad(a, ((0, 0), (0, width - a.shape[1])))


def _pad_rows(a, height):
    return jnp.pad(a, ((0, height - a.shape[0]), (0, 0)))


def kernel(x, c, ctx, c_ctx, w_ada, b_ada, w_in, conv_rw, conv_hy, conv_hy_b, w0_f, w_up_f, a0_f, a_up_f, w0_b, w_up_b, a0_b, a_up_b, k_k, k_a, r_k, g_up, gn_g, gn_b, filt_w1, filt_b1, filt_w2, filt_b2, filt_w3, sin_freq, hy_bias, w_out, ln1_g, ln1_b, ffn_w1, ffn_w3, ffn_w2, ln2_g, ln2_b):
    bsz, L, d = x.shape
    rw = k_k.shape[1]
    hy = hy_bias.shape[1]
    dl, il, gl = w_up_f.shape[1], a_up_f.shape[1], g_up.shape[1]
    lo_w = 4 * LANE
    rkv_cols = 3 * rw

    wi = w_in[0]
    lo0 = rkv_cols

    def lora_layout(a):
        return jnp.concatenate([_pad_cols(a[:, lo0:lo0 + dl], LANE),
                                _pad_cols(a[:, lo0 + dl:lo0 + dl + il], LANE),
                                _pad_cols(a[:, lo0 + dl + il:lo0 + dl + il + gl], 2 * LANE)], axis=1)

    w_lat = _weight_layout(jnp.transpose(wi), rkv_cols=rkv_cols, lora_widths=(dl, il, gl), hy_cols=3 * hy)
    cw_lat = jnp.concatenate([conv_rw[0][:, :rkv_cols], conv_hy[0], lora_layout(conv_rw[0])], axis=1)
    cb_lat = jnp.concatenate([jnp.zeros((rkv_cols,), F32), conv_hy_b[0], jnp.zeros((lo_w,), F32)])[None, :]
    wup = _pad_rows(jnp.concatenate([w_up_f[0], w_up_b[0]], axis=1), LANE)
    aup = _pad_rows(jnp.concatenate([a_up_f[0], a_up_b[0]], axis=1), LANE)
    gup = _pad_rows(g_up[0], 2 * LANE)
    w0 = jnp.concatenate([w0_f[0], w0_b[0]])
    a0 = jnp.concatenate([a0_f[0], a0_b[0]])

    cc = _pad_rows(jnp.concatenate([c, c_ctx[None, :]], axis=0), SUB)
    mod = _adaln(cc, w_ada[0], b_ada[0])
    mod3 = mod.reshape(SUB, 1, 6 * d)

    er, ec = _pos_tables(L // GRID_W, d)

    tn = lo_w
    p_lat = _inproj(x, er, ec, mod3, w_lat, cw_lat, cb_lat, tm=1024, tn=tn, add_pos=True, ctx_row=None)
    ctx_tiles = list(range(rkv_cols // tn)) + [(rkv_cols + 3 * hy) // tn]
    p_ctx = _inproj(ctx, er, ec, mod3, w_lat, cw_lat, cb_lat, tm=ctx.shape[1], tn=tn, add_pos=False,
                    ctx_row=bsz, col_tiles=ctx_tiles)

    lo_cols = (rkv_cols + 3 * hy, rkv_cols + 3 * hy + lo_w)
    r, v, kk, gate, lw, kh, bvec = _rwkv_prep(p_lat, p_ctx, wup, aup, gup, w0, a0,
                                              k_k[0], k_a[0], rw=rw, rkv_cols=rkv_cols, lo_cols=lo_cols)
    o_f, o_b = _rwkv_scan(r, v, kk, lw, kh, bvec, r_k[0], gn_g[0], gn_b[0], n_lat_tokens=L)

    tabs = _fft_tables(L)
    zpos, tcol = _filter_positions(L)
    max_decay = math.log(DECAY_TARGET) / FAST_DECAY_PCT
    min_decay = math.log(DECAY_TARGET) / SLOW_DECAY_PCT
    deltas = jnp.abs(jnp.linspace(min_decay, max_decay, hy, dtype=F32))[None, :]
    kf = _filter_spectrum(_pad_cols(zpos, LANE), tcol, _pad_rows(filt_w1[0], LANE), filt_b1[0], filt_w2[0],
                          filt_b2[0], filt_w3[0], sin_freq[0], deltas, tabs, ct=LANE)
    o_hy = _hyena_conv(p_lat, hy_bias[0], kf, tabs, hy_col0=rkv_cols, hy_dim=hy, ct=LANE)

    x1 = _outproj(o_f, o_b, gate, o_hy, x, er, ec, mod3, w_out[0].astype(BF16), ln1_g[0], ln1_b[0], tm=512)
    return _ffn(x1, mod3, ffn_w1[0].astype(BF16), ffn_w3[0].astype(BF16), ffn_w2[0].astype(BF16),
                ln2_g[0], ln2_b[0], tm=1024, tf=512)
```

```python
import functools
import math

import numpy as np
import jax
import jax.numpy as jnp
from jax import lax
from jax.experimental import pallas as pl
from jax.experimental.pallas import tpu as pltpu

F32 = jnp.float32
BF16 = jnp.bfloat16

GRID_W = 64
HEAD_DIM = 64
N_BANDS = 16
DECAY_TARGET = 1e-2
FAST_DECAY_PCT = 0.3
SLOW_DECAY_PCT = 1.5
LN_EPS = 1e-5
GN_EPS = 64e-5
ALPHA = 2.0 ** 0.25

LANE = 128
SUB = 8
VMEM_LIMIT = 56 * 1024 * 1024

SCAN_CHUNK = 64
SCAN_GROUP = 2
SCAN_BLOCK = 128
FFN_LN_ROWS = 256
FFT_N2 = 64
FFT_UNROLL = 8


def _cparams(sem):
    return pltpu.CompilerParams(dimension_semantics=sem, vmem_limit_bytes=VMEM_LIMIT)


def _dot(a, b, dims=(((1,), (0,)), ((), ()))):
    return lax.dot_general(a.astype(BF16), b.astype(BF16), dims, preferred_element_type=F32)


def _dot_nt(a, b):
    return _dot(a, b, (((1,), (1,)), ((), ())))


def _dot_tn(a, b):
    return _dot(a, b, (((0,), (0,)), ((), ())))


def _split2(a):
    hi = a.astype(BF16)
    lo = (a - hi.astype(F32)).astype(BF16)
    return hi, lo


def _dot3(a, b, b_split=None):
    ah, al = _split2(a)
    bh, bl = _split2(b) if b_split is None else b_split
    return _dot(ah, bh) + (_dot(al, bh) + _dot(ah, bl))


def _dot_exact_lhs(a_exact, b):
    b1, b2 = _split2(b)
    return _dot(a_exact, b1) + _dot(a_exact, b2)


def _sigmoid(x):
    return 0.5 * jnp.tanh(0.5 * x) + 0.5


def _layer_norm(h, g, b):
    mu = jnp.mean(h, -1, keepdims=True)
    hc = h - mu
    var = jnp.mean(hc * hc, -1, keepdims=True)
    return hc * lax.rsqrt(var + LN_EPS) * g + b


def _adaln_kernel(c_ref, w_ref, b_ref, o_ref):
    cv = c_ref[...]
    a = cv * _sigmoid(cv)
    o_ref[...] = _dot3(a, w_ref[...]) + b_ref[...]


def _adaln(cc, w, b, tn=1024):
    m, d = cc.shape
    n = w.shape[1]
    return pl.pallas_call(
        _adaln_kernel,
        grid=(n // tn,),
        in_specs=[pl.BlockSpec((m, d), lambda j: (0, 0)),
                  pl.BlockSpec((d, tn), lambda j: (0, j)),
                  pl.BlockSpec((1, tn), lambda j: (0, j))],
        out_specs=pl.BlockSpec((m, tn), lambda j: (0, j)),
        out_shape=jax.ShapeDtypeStruct((m, n), F32),
        compiler_params=_cparams(("arbitrary",)),
        name="adaln",
    )(cc, w, b.reshape(1, n))


def _wlayout_kernel(wt_ref, o_ref, *, rkv_cols, lora_widths, hy_cols):
    tk = wt_ref.shape[1]

    def put(dst, rows):
        o_ref[:, dst:dst + rows.shape[0]] = jnp.transpose(rows).astype(o_ref.dtype)

    put(0, wt_ref[0:rkv_cols, :])
    src = rkv_cols
    pieces = []
    for width in lora_widths:
        padded = -(-width // LANE) * LANE
        pieces.append(wt_ref[src:src + width, :])
        if padded > width:
            pieces.append(jnp.zeros((padded - width, tk), F32))
        src += width
    put(rkv_cols, wt_ref[src:src + hy_cols, :])
    put(rkv_cols + hy_cols, jnp.concatenate(pieces, axis=0))


def _weight_layout(wt, *, rkv_cols, lora_widths, hy_cols, tk=256):
    n_src, k = wt.shape
    n_dst = rkv_cols + hy_cols + sum(-(-wd // LANE) * LANE for wd in lora_widths)
    return pl.pallas_call(
        functools.partial(_wlayout_kernel, rkv_cols=rkv_cols, lora_widths=lora_widths, hy_cols=hy_cols),
        grid=(k // tk,),
        in_specs=[pl.BlockSpec((n_src, tk), lambda i: (0, i))],
        out_specs=pl.BlockSpec((tk, n_dst), lambda i: (i, 0)),
        out_shape=jax.ShapeDtypeStruct((k, n_dst), BF16),
        compiler_params=_cparams(("arbitrary",)),
        name="weight_layout",
    )(wt)


HALO = 16


def _add_pos(xv, er_row, ec_rows):
    half = ec_rows.shape[1]
    return jnp.concatenate([xv[:, :half] + er_row, xv[:, half:] + ec_rows], axis=1)


def _inproj_kernel(x_ref, xp_ref, xn_ref, er_ref, ec_ref, sh_ref, sc_ref, w_ref, cw_ref, cb_ref, o_ref, a_scr,
                   *, add_pos):
    i = pl.program_id(1)
    n_i = pl.num_programs(1)
    tm = x_ref.shape[1]
    gr = tm // GRID_W

    @pl.when(pl.program_id(2) == 0)
    def _():
        scale = 1.0 + sc_ref[0]
        shift = sh_ref[0]

        def put(rows, xv, keep=None):
            a = xv * scale + shift
            if keep is not None:
                a = jnp.where(keep, a, 0.0)
            a_scr[rows, :] = a.astype(BF16)

        has_prev = i > 0
        has_next = i < n_i - 1
        lo_rows, hi_rows = slice(0, HALO), slice(HALO + tm, 2 * HALO + tm)
        if add_pos:
            n_er = er_ref.shape[0]
            for q in range(gr):
                rows = slice(q * GRID_W, (q + 1) * GRID_W)
                put(slice(HALO + q * GRID_W, HALO + (q + 1) * GRID_W),
                    _add_pos(x_ref[0, rows, :], er_ref[pl.ds(i * gr + q, 1), :], ec_ref[...]))
            put(lo_rows, _add_pos(xp_ref[0], er_ref[pl.ds(jnp.maximum(i * gr - 1, 0), 1), :],
                                  ec_ref[GRID_W - HALO:, :]), has_prev)
            put(hi_rows, _add_pos(xn_ref[0], er_ref[pl.ds(jnp.minimum((i + 1) * gr, n_er - 1), 1), :],
                                  ec_ref[:HALO, :]), has_next)
        else:
            put(lo_rows, xp_ref[0], has_prev)
            put(slice(HALO, HALO + tm), x_ref[0])
            put(hi_rows, xn_ref[0], has_next)

    n_rows = tm + 2 * HALO
    half = w_ref.shape[1] // 2
    for h in range(2):
        cols = slice(h * half, (h + 1) * half)
        res = jnp.dot(a_scr[...], w_ref[:, cols], preferred_element_type=F32)
        out = (pltpu.roll(res, 1, 0) * cw_ref[0:1, cols] + res * cw_ref[1:2, cols]
               + pltpu.roll(res, n_rows - 1, 0) * cw_ref[2:3, cols] + cb_ref[:, cols])
        o_ref[0, :, cols] = out[HALO:HALO + tm].astype(o_ref.dtype)


def _inproj(x, er, ec, mod3, w, cw, cb, *, tm, tn, add_pos, ctx_row, col_tiles=None):
    bsz, lx, d = x.shape
    if col_tiles is None:
        n = w.shape[1]
        col = lambda j: j
    else:
        n = len(col_tiles) * tn
        run = col_tiles[:-1]
        assert run == list(range(len(run)))
        col = lambda j: jnp.where(j < len(run), j, col_tiles[-1])
    hb = tm // HALO
    n_hb = lx // HALO
    row = (lambda b: ctx_row) if ctx_row is not None else (lambda b: b)
    return pl.pallas_call(
        functools.partial(_inproj_kernel, add_pos=add_pos),
        grid=(bsz, lx // tm, n // tn),
        in_specs=[pl.BlockSpec((1, tm, d), lambda b, i, j: (b, i, 0)),
                  pl.BlockSpec((1, HALO, d), lambda b, i, j: (b, jnp.maximum(i * hb - 1, 0), 0)),
                  pl.BlockSpec((1, HALO, d), lambda b, i, j: (b, jnp.minimum((i + 1) * hb, n_hb - 1), 0)),
                  pl.BlockSpec(er.shape, lambda b, i, j: (0, 0)),
                  pl.BlockSpec(ec.shape, lambda b, i, j: (0, 0)),
                  pl.BlockSpec((1, 1, d), lambda b, i, j: (row(b), 0, 0)),
                  pl.BlockSpec((1, 1, d), lambda b, i, j: (row(b), 0, 1)),
                  pl.BlockSpec((d, tn), lambda b, i, j: (0, col(j))),
                  pl.BlockSpec((3, tn), lambda b, i, j: (0, col(j))),
                  pl.BlockSpec((1, tn), lambda b, i, j: (0, col(j)))],
        out_specs=pl.BlockSpec((1, tm, tn), lambda b, i, j: (b, i, j)),
        out_shape=jax.ShapeDtypeStruct((bsz, lx, n), BF16),
        scratch_shapes=[pltpu.VMEM((tm + 2 * HALO, d), BF16)],
        compiler_params=_cparams(("arbitrary", "arbitrary", "arbitrary")),
        name="inproj_pos" if add_pos else "inproj_ctx",
    )(x, x, x, er, ec, mod3, mod3, w, cw, cb)


def _head_sum(x, ones_bd):
    rows = x.shape[0]
    outs = []
    for j in range(x.shape[1] // LANE):
        hi, lo = _split2(x[:, j * LANE:(j + 1) * LANE])
        s = _dot(jnp.concatenate([hi, lo], axis=0), ones_bd)
        outs.append(s[:rows] + s[rows:])
    return jnp.concatenate(outs, axis=1)


def _prep_kernel(p_rkv, p_lo, pc_rkv, pc_lo, wup, aup, gup, w0, a0, kk_w, ka_w, ones_ref,
                 r_out, v_out, kk_out, gate_out, lw_out, kh_out, b_out, *, n_lat):
    is_ctx = pl.program_id(1) == n_lat
    rw = kk_out.shape[2]
    ones_bd = ones_ref[...]
    wup_split = _split2(wup[...])
    aup_split = _split2(aup[...])
    gup_bf = gup[...].astype(BF16)

    def row_block(rows):
        def rkv(s):
            cols = slice(s * rw, (s + 1) * rw)
            return jnp.where(is_ctx, pc_rkv[0, rows, cols], p_rkv[0, rows, cols])

        r_out[0, rows, :] = rkv(0)
        v_out[0, rows, :] = rkv(2)
        k = rkv(1).astype(F32)
        kkr = k * kk_w[...]
        nrm = jnp.sqrt(_head_sum(kkr * kkr, ones_bd))
        kk = kkr / jnp.maximum(nrm, 1e-12)
        kk_out[0, rows, :] = kk.astype(kk_out.dtype)

        lo = jnp.where(is_ctx, pc_lo[0, rows, :], p_lo[0, rows, :]).astype(F32)
        tw = jnp.tanh(lo[:, 0:LANE])
        al = lo[:, LANE:2 * LANE]
        gl = _sigmoid(lo[:, 2 * LANE:])
        gate_out[0, rows, :] = _dot(gl, gup_bf).astype(gate_out.dtype)
        wx = _dot3(tw, None, wup_split) + w0[...]
        ax = _dot3(al, None, aup_split) + a0[...]
        for d in range(2):
            cols = slice(d * rw, (d + 1) * rw)
            lw_out[d, 0, rows, :] = -math.exp(-0.5) * _sigmoid(wx[:, cols])
            ia = _sigmoid(ax[:, cols])
            kh_out[d, 0, rows, :] = (k * (1.0 + (ia - 1.0) * ka_w[...])).astype(kh_out.dtype)
            b_out[d, 0, rows, :] = (kk * ia).astype(b_out.dtype)

    row_block(slice(0, kk_out.shape[1]))


def _rwkv_prep(p, pc, wup, aup, gup, w0, a0, k_k, k_a, *, rw, rkv_cols, lo_cols):
    bsz, lq, _ = p.shape
    ctx = pc.shape[1]
    tr = ctx
    n_lat = lq // tr
    t_all = lq + ctx
    lo_w = lo_cols[1] - lo_cols[0]
    lo_blk = lo_cols[0] // lo_w
    lo_blk_c = rkv_cols // lo_w
    lat = lambda i: jnp.minimum(i, n_lat - 1)
    ones_bd = (np.arange(LANE)[:, None] // HEAD_DIM == np.arange(LANE)[None, :] // HEAD_DIM)
    ones_bd = jnp.asarray(ones_bd, BF16)
    full = lambda a: pl.BlockSpec(a.shape, lambda b, i: (0,) * a.ndim)
    row = lambda a: a.reshape(1, -1)
    weights = [wup, aup, gup, row(w0), row(a0), row(k_k), row(k_a), ones_bd]
    shared = jax.ShapeDtypeStruct((bsz, t_all, rw), BF16)
    perdir = jax.ShapeDtypeStruct((2, bsz, t_all, rw), BF16)
    perdir_f32 = jax.ShapeDtypeStruct((2, bsz, t_all, rw), F32)
    o_shared = pl.BlockSpec((1, tr, rw), lambda b, i: (b, i, 0))
    o_dir = pl.BlockSpec((2, 1, tr, rw), lambda b, i: (0, b, i, 0))
    return pl.pallas_call(
        functools.partial(_prep_kernel, n_lat=n_lat),
        grid=(bsz, n_lat + 1),
        in_specs=[pl.BlockSpec((1, tr, rkv_cols), lambda b, i: (b, lat(i), 0)),
                  pl.BlockSpec((1, tr, lo_w), lambda b, i: (b, lat(i), lo_blk)),
                  pl.BlockSpec((1, tr, rkv_cols), lambda b, i: (b, 0, 0)),
                  pl.BlockSpec((1, tr, lo_w), lambda b, i: (b, 0, lo_blk_c))]
                 + [full(a) for a in weights],
        out_specs=[o_shared, o_shared, o_shared, o_shared, o_dir, o_dir, o_dir],
        out_shape=[shared, shared, shared, shared, perdir_f32, perdir, perdir],
        compiler_params=_cparams(("arbitrary", "arbitrary")),
        name="rwkv_prep",
    )(p, p, pc, pc, *weights)


def _scan_kernel(rf_ref, vf_ref, kkf_ref, lwf_ref, khf_ref, bf_ref, rb_ref, vb_ref, kkb_ref, lwb_ref, khb_ref,
                 bb_ref, rk_ref, gng_ref, gnb_ref, ones_ref, of_ref, ob_ref, s_scr):
    C = SCAN_CHUNK
    n_sub = rf_ref.shape[1] // C
    G = SCAN_GROUP
    W = G * HEAD_DIM
    n = G * C
    n_groups = rf_ref.shape[2] // W

    @pl.when(pl.program_id(1) == 0)
    def _():
        s_scr[...] = jnp.zeros_like(s_scr)

    bi = lax.broadcasted_iota(jnp.int32, (n, n), 0)
    bj = lax.broadcasted_iota(jnp.int32, (n, n), 1)
    same = (bi // C) == (bj // C)
    li = bi % C
    lj = bj % C
    eye = jnp.where(bi == bj, 1.0, 0.0)
    hrow = lax.broadcasted_iota(jnp.int32, (n, W), 0) // C
    hlane = lax.broadcasted_iota(jnp.int32, (n, W), 1) // HEAD_DIM
    hmask = hrow == hlane
    ti = lax.broadcasted_iota(jnp.int32, (C, C), 0)
    tj = lax.broadcasted_iota(jnp.int32, (C, C), 1)

    def rep(x):
        return jnp.where(hmask, jnp.concatenate([x] * G, axis=0), 0.0).astype(BF16)

    chains = []
    dirs = ((rf_ref, vf_ref, kkf_ref, lwf_ref, khf_ref, bf_ref, False),
            (rb_ref, vb_ref, kkb_ref, lwb_ref, khb_ref, bb_ref, True))
    per_chunk = {}
    for di, (r_ref, v_ref, kk_ref, lw_ref, kh_ref, b_ref, rev) in enumerate(dirs):
        before = (lj > li) if rev else (lj < li)
        strict = jnp.logical_and(same, before)
        incl = jnp.logical_and(same, jnp.logical_or(before, li == lj))
        tri = jnp.where((tj >= ti) if rev else (tj <= ti), 1.0, 0.0).astype(BF16)
        for t in range(n_sub):
            sub = n_sub - 1 - t if rev else t
            rows = slice(sub * C, (sub + 1) * C)
            r = r_ref[0, rows, :].astype(F32)
            v = v_ref[0, rows, :].astype(F32)
            kk = kk_ref[0, rows, :].astype(F32)
            lw = lw_ref[0, 0, rows, :]
            kh = kh_ref[0, 0, rows, :].astype(F32)
            bv = b_ref[0, 0, rows, :].astype(F32)
            cum = _dot_exact_lhs(tri, lw)
            tot = cum[0:1, :] if rev else cum[C - 1:C, :]
            e_neg = jnp.exp(-cum)
            e_end = jnp.exp(tot - cum)
            rt = r * jnp.exp(cum)
            at = -kk * jnp.exp(cum - lw)
            bt = bv * e_neg
            kt = kh * e_neg
            bp = bv * e_end
            kp = kh * e_end
            per_chunk[(di, t)] = (r, v, kh, jnp.exp(tot), rows)
            for g in range(n_groups):
                cols = slice(g * W, (g + 1) * W)
                chains.append(dict(
                    di=di, t=t, g=g, cols=cols, strict=strict, incl=incl,
                    v2=rep(v[:, cols]),
                    ar=jnp.concatenate([rep(at[:, cols]), rep(rt[:, cols])], axis=0),
                    bk=jnp.concatenate([rep(bt[:, cols]), rep(kt[:, cols])], axis=0),
                    bkp=jnp.concatenate([rep(bp[:, cols]), rep(kp[:, cols])], axis=0)))

    for ch in chains:
        ch["tt"] = _dot_nt(ch["ar"], ch["bk"])
    for ch in chains:
        ch["p"] = jnp.where(ch["strict"], ch["tt"][:n, :n], 0.0).astype(BF16)
        ch["tinv"] = eye + ch["p"].astype(F32)
    step = 2
    while step < C:
        for ch in chains:
            ch["p"] = _dot(ch["p"], ch["p"]).astype(BF16)
        for ch in chains:
            ch["tinv"] = ch["tinv"] + _dot(ch["p"], ch["tinv"])
        step *= 2
    state = {(di, g): s_scr[di, g] for di in range(2) for g in range(n_groups)}
    for t in range(n_sub):
        cur = [ch for ch in chains if ch["t"] == t]
        for ch in cur:
            ch["s0"] = state[(ch["di"], ch["g"])]
            ch["xs"] = _dot_nt(ch["ar"], ch["s0"])
        for ch in cur:
            a_ak = jnp.where(ch["strict"], ch["tt"][:n, n:], 0.0)
            ch["rhs"] = ch["xs"][:n] + _dot(a_ak, ch["v2"])
        for ch in cur:
            u = _dot(ch["tinv"], ch["rhs"])
            ch["uv"] = jnp.concatenate([u.astype(BF16), ch["v2"]], axis=0)
        for ch in cur:
            e_tot = per_chunk[(ch["di"], t)][3]
            state[(ch["di"], ch["g"])] = ch["s0"] * e_tot[:, ch["cols"]] + _dot_tn(ch["uv"], ch["bkp"])
        for ch in cur:
            a_r = jnp.concatenate([jnp.where(ch["incl"], ch["tt"][n:, :n], 0.0),
                                   jnp.where(ch["incl"], ch["tt"][n:, n:], 0.0)], axis=1)
            ch["y2"] = ch["xs"][n:] + _dot(a_r, ch["uv"])
    for (di, g), s_new in state.items():
        s_scr[di, g] = s_new

    ones_bd = ones_ref[...]
    inv = 1.0 / HEAD_DIM
    for di, o_ref in enumerate((of_ref, ob_ref)):
        for t in range(n_sub):
            r, v, kh, _, rows = per_chunk[(di, t)]
            ys = []
            for ch in chains:
                if ch["di"] == di and ch["t"] == t:
                    y = ch["y2"][0:C]
                    for h in range(1, G):
                        y = y + ch["y2"][h * C:(h + 1) * C]
                    ys.append(y)
            y = jnp.concatenate(ys, axis=1)
            mu = _head_sum(y, ones_bd) * inv
            yc = y - mu
            var = _head_sum(yc * yc, ones_bd) * inv
            gn = yc * lax.rsqrt(var + GN_EPS) * gng_ref[...] + gnb_ref[...]
            bonus = _head_sum(r * kh * rk_ref[...], ones_bd) * v
            o_ref[0, rows, :] = gn + bonus


def _rwkv_scan(r, v, kk, lw, kh, bvec, r_k, gn_g, gn_b, *, n_lat_tokens):
    bsz, t_all, rw = r.shape
    C = SCAN_BLOCK
    n_chunks = t_all // C
    n_ctx = (t_all - n_lat_tokens) // C
    n_lat = n_lat_tokens // C
    W = SCAN_GROUP * HEAD_DIM

    blk_f = lambda c: jnp.where(c < n_ctx, n_lat + c, c - n_ctx)
    blk_b = lambda c: n_chunks - 1 - c
    ones_bd = (np.arange(LANE)[:, None] // HEAD_DIM == np.arange(LANE)[None, :] // HEAD_DIM)
    ones_bd = jnp.asarray(ones_bd, BF16)
    shared = lambda blk: pl.BlockSpec((1, C, rw), lambda b, c: (b, blk(c), 0))
    perdir = lambda d, blk: pl.BlockSpec((1, 1, C, rw), lambda b, c: (d, b, blk(c), 0))
    vec = pl.BlockSpec((1, rw), lambda b, c: (0, 0))
    out = jax.ShapeDtypeStruct((bsz, t_all, rw), F32)
    return pl.pallas_call(
        _scan_kernel,
        grid=(bsz, n_chunks),
        in_specs=[shared(blk_f), shared(blk_f), shared(blk_f), perdir(0, blk_f), perdir(0, blk_f), perdir(0, blk_f),
                  shared(blk_b), shared(blk_b), shared(blk_b), perdir(1, blk_b), perdir(1, blk_b), perdir(1, blk_b),
                  vec, vec, vec, pl.BlockSpec((LANE, LANE), lambda b, c: (0, 0))],
        out_specs=[shared(blk_f), shared(blk_b)],
        out_shape=[out, out],
        scratch_shapes=[pltpu.VMEM((2, rw // W, W, W), F32)],
        compiler_params=_cparams(("arbitrary", "arbitrary")),
        name="rwkv_scan",
    )(r, v, kk, lw, kh, bvec, r, v, kk, lw, kh, bvec,
      r_k.reshape(1, rw), gn_g.reshape(1, rw), gn_b.reshape(1, rw), ones_bd)


def _fft_tables(L):
    n2 = FFT_N2
    n_all = 2 * L
    n1 = n_all // n2
    nh = n1 // 2
    two_pi = 2.0 * np.pi
    kh = -(-(nh + 1) // FFT_UNROLL) * FFT_UNROLL
    k1 = np.arange(kh)
    pair_w = np.where(k1 > nh, 0.0, np.where((k1 == 0) | (k1 == nh), 1.0, 2.0))
    th1 = two_pi * np.outer(k1, np.arange(nh)) / n1
    f1 = np.concatenate([np.cos(th1), -np.sin(th1)], 0)
    th2 = two_pi * np.outer(np.arange(n2), np.arange(n2)) / n2
    fr, fi = np.cos(th2), -np.sin(th2)
    a3 = np.block([[fr, -fi], [fi, fr]])
    b3 = np.block([[-fi, -fr], [fr, -fi]])
    tht = two_pi * np.outer(k1, np.arange(n2)) / n_all
    twr = np.tile(np.cos(tht), (1, 2))
    twi = np.tile(-np.sin(tht), (1, 2))
    gr, gi = np.cos(th2), np.sin(th2)
    m3i = np.block([[gr, -gi], [gi, gr]]) / n_all
    th4 = two_pi * np.outer(np.arange(nh), k1) / n1
    cos4, sin4 = np.cos(th4) * pair_w, np.sin(th4) * pair_w
    ca = np.concatenate([cos4, -sin4], 1)
    cb = np.concatenate([-sin4, -cos4], 1)
    th5 = two_pi * np.outer(np.arange(n2), k1) / n_all
    t2r, t2i = np.tile(np.cos(th5), (1, 2)), np.tile(np.sin(th5), (1, 2))
    f = lambda a: jnp.asarray(a, F32)
    return dict(f1=f(f1), a3=f(a3), b3=f(b3), twr=f(twr), twi=f(twi),
                m3i=f(m3i), ca=f(ca), cb=f(cb), t2r=f(t2r), t2i=f(t2i))


def _fft_stage1(src_ref, f1_ref, s_ref):
    kh2, nh = f1_ref.shape
    n2 = FFT_N2
    f1 = f1_ref[...].astype(BF16)

    def body(nlo, carry):
        xin = src_ref[pl.ds(nlo, nh, stride=n2), :]
        s_ref[pl.ds(pl.multiple_of(nlo * kh2, SUB), kh2), :] = _dot(f1, xin)
        return carry

    lax.fori_loop(0, n2, body, 0, unroll=FFT_UNROLL)


def _fft_mid_matrix(k1, a3_ref, b3_ref, twr_ref, twi_ref):
    tr = twr_ref[pl.ds(k1, 1), :]
    ti = twi_ref[pl.ds(k1, 1), :]
    return (a3_ref[...] * tr + b3_ref[...] * ti).astype(BF16)


def _fft_load_k1(s_ref, k1, kh2):
    n2 = FFT_N2
    re = s_ref[pl.ds(k1, n2, stride=kh2), :]
    im = s_ref[pl.ds(kh2 // 2 + k1, n2, stride=kh2), :]
    return jnp.concatenate([re, im], axis=0)


def _hyena_kernel(x0_ref, x1_ref, v_ref, bias_ref, kf_ref,
                  f1_ref, a3_ref, b3_ref, twr_ref, twi_ref, m3i_ref, ca_ref, cb_ref, t2r_ref, t2i_ref,
                  o_ref, z_scr, y_scr, s_scr, q_scr):
    n2 = FFT_N2
    kh2, nh = f1_ref.shape
    kh = kh2 // 2
    z_scr[...] = v_ref[0].astype(F32) * x1_ref[0].astype(F32)
    _fft_stage1(z_scr, f1_ref, s_scr)
    m3i = m3i_ref[...].astype(BF16)

    def mid(k1, carry):
        m = _fft_mid_matrix(k1, a3_ref, b3_ref, twr_ref, twi_ref)
        cc = _dot(m, _fft_load_k1(s_scr, k1, kh2))
        kf = kf_ref[0, k1]
        cr, ci = cc[:n2], cc[n2:]
        kr, ki = kf[:n2], kf[n2:]
        pp = jnp.concatenate([cr * kr - ci * ki, cr * ki + ci * kr], axis=0)
        q = _dot(m3i, pp)
        q_scr[pl.ds(k1, n2, stride=kh2), :] = q[:n2]
        q_scr[pl.ds(kh + k1, n2, stride=kh2), :] = q[n2:]
        return carry

    lax.fori_loop(0, kh, mid, 0, unroll=FFT_UNROLL)

    def last(m2, carry):
        f4 = ca_ref[...] * t2r_ref[pl.ds(m2, 1), :] + cb_ref[...] * t2i_ref[pl.ds(m2, 1), :]
        qm = q_scr[pl.ds(pl.multiple_of(m2 * kh2, SUB), kh2), :]
        y_scr[pl.ds(m2, nh, stride=n2), :] = _dot(f4, qm)
        return carry

    lax.fori_loop(0, n2, last, 0, unroll=FFT_UNROLL)
    z = z_scr[...]
    o_ref[0] = (y_scr[...] + z * bias_ref[...]) * x0_ref[0].astype(F32)


def _hyena_conv(p, hy_bias, kf, tabs, *, hy_col0, hy_dim, ct):
    bsz, L, _ = p.shape
    n_tiles = hy_dim // ct
    blk0 = hy_col0 // ct
    sec = hy_dim // ct
    xspec = lambda s: pl.BlockSpec((1, L, ct), lambda t, b: (b, 0, blk0 + s * sec + t))
    tab_list = [tabs[k] for k in ("f1", "a3", "b3", "twr", "twi", "m3i", "ca", "cb", "t2r", "t2i")]
    full = lambda a: pl.BlockSpec(a.shape, lambda t, b: (0,) * a.ndim)
    kh2 = tabs["f1"].shape[0]
    return pl.pallas_call(
        _hyena_kernel,
        grid=(n_tiles, bsz),
        in_specs=[xspec(0), xspec(1), xspec(2),
                  pl.BlockSpec((1, ct), lambda t, b: (0, t)),
                  pl.BlockSpec((1,) + kf.shape[1:], lambda t, b: (t, 0, 0, 0), pipeline_mode=pl.Buffered(1))]
                 + [full(a) for a in tab_list],
        out_specs=pl.BlockSpec((1, L, ct), lambda t, b: (b, 0, t)),
        out_shape=jax.ShapeDtypeStruct((bsz, L, hy_dim), F32),
        scratch_shapes=[pltpu.VMEM((L, ct), F32), pltpu.VMEM((L, ct), F32),
                        pltpu.VMEM((FFT_N2 * kh2, ct), F32), pltpu.VMEM((FFT_N2 * kh2, ct), F32)],
        compiler_params=_cparams(("arbitrary", "arbitrary")),
        name="hyena_conv",
    )(p, p, p, hy_bias.reshape(1, -1), kf, *tab_list)


def _filter_kernel(z_ref, t_ref, w1_ref, b1_ref, w2_ref, b2_ref, w3f_ref, w3b_ref, freq_ref, delta_ref,
                   f1_ref, a3_ref, b3_ref, twr_ref, twi_ref, kf_ref, h_scr, hf_scr, hb_scr, sf_scr, sb_scr):
    n2 = FFT_N2
    kh2 = f1_ref.shape[0]

    @pl.when(pl.program_id(0) == 0)
    def _():
        freq = freq_ref[...]
        h1 = jnp.sin(freq * (_dot3(z_ref[...], w1_ref[...]) + b1_ref[...]))
        h_scr[...] = jnp.sin(freq * (_dot3(h1, w2_ref[...]) + b2_ref[...]))

    h = h_scr[...]
    window = jnp.exp(-t_ref[...] * delta_ref[...])
    hf_scr[...] = _dot3(h, w3f_ref[...]) * window
    hb = _dot3(h, w3b_ref[...]) * window
    rows = lax.broadcasted_iota(jnp.int32, hb.shape, 0)
    hb_scr[...] = jnp.where(rows == 0, 0.0, hb)
    _fft_stage1(hf_scr, f1_ref, sf_scr)
    _fft_stage1(hb_scr, f1_ref, sb_scr)

    def mid(k1, carry):
        m = _fft_mid_matrix(k1, a3_ref, b3_ref, twr_ref, twi_ref)
        cf = _dot(m, _fft_load_k1(sf_scr, k1, kh2))
        cb = _dot(m, _fft_load_k1(sb_scr, k1, kh2))
        kf_ref[0, k1] = jnp.concatenate([cf[:n2] + cb[:n2], cf[n2:] - cb[n2:]], axis=0)
        return carry

    lax.fori_loop(0, kh2 // 2, mid, 0, unroll=FFT_UNROLL)


def _filter_spectrum(zpos, tcol, w1, b1, w2, b2, w3, freq, deltas, tabs, *, ct):
    L = zpos.shape[0]
    hy_dim = deltas.shape[1]
    n_tiles = hy_dim // ct
    kh2 = tabs["f1"].shape[0]
    kh = kh2 // 2
    tab_list = [tabs[k] for k in ("f1", "a3", "b3", "twr", "twi")]
    full = lambda a: pl.BlockSpec(a.shape, lambda t: (0,) * a.ndim)
    row = lambda a: a.reshape(1, -1)
    small = [zpos, tcol, w1, row(b1), w2, row(b2)]
    return pl.pallas_call(
        _filter_kernel,
        grid=(n_tiles,),
        in_specs=[full(a) for a in small]
                 + [pl.BlockSpec((w3.shape[0], ct), lambda t: (0, t)),
                    pl.BlockSpec((w3.shape[0], ct), lambda t: (0, n_tiles + t)),
                    full(row(freq)),
                    pl.BlockSpec((1, ct), lambda t: (0, t))]
                 + [full(a) for a in tab_list],
        out_specs=pl.BlockSpec((1, kh, 2 * FFT_N2, ct), lambda t: (t, 0, 0, 0)),
        out_shape=jax.ShapeDtypeStruct((n_tiles, kh, 2 * FFT_N2, ct), F32),
        scratch_shapes=[pltpu.VMEM((L, w2.shape[1]), F32), pltpu.VMEM((L, ct), F32), pltpu.VMEM((L, ct), F32),
                        pltpu.VMEM((FFT_N2 * kh2, ct), F32), pltpu.VMEM((FFT_N2 * kh2, ct), F32)],
        compiler_params=_cparams(("arbitrary",)),
        name="hyena_filter",
    )(*small, w3, w3, row(freq), deltas, *tab_list)


def _outproj_kernel(of_ref, ob_ref, gate_ref, ohy_ref, x_ref, er_ref, ec_ref, g1_ref, w_ref, lng_ref, lnb_ref,
                    o_ref):
    rw = of_ref.shape[2]
    a_rw = ((of_ref[0] + ob_ref[0]) * gate_ref[0].astype(F32)).astype(BF16)
    mix = jnp.dot(a_rw, w_ref[0:rw, :], preferred_element_type=F32)
    mix = mix + jnp.dot(ohy_ref[0].astype(BF16), w_ref[rw:, :], preferred_element_type=F32)
    gmix = g1_ref[0] * mix
    tm = x_ref.shape[1]
    for q in range(tm // GRID_W):
        rows = slice(q * GRID_W, (q + 1) * GRID_W)
        xv = _add_pos(x_ref[0, rows, :], er_ref[q:q + 1, :], ec_ref[...])
        o_ref[0, rows, :] = _layer_norm(ALPHA * xv + gmix[rows], lng_ref[...], lnb_ref[...]).astype(o_ref.dtype)


def _outproj(o_f, o_b, gate, o_hy, x, er, ec, mod3, w_out, ln_g, ln_b, *, tm):
    bsz, L, d = x.shape
    rw = gate.shape[2]
    hy = o_hy.shape[2]
    gr = tm // GRID_W
    return pl.pallas_call(
        _outproj_kernel,
        grid=(bsz, L // tm),
        in_specs=[pl.BlockSpec((1, tm, rw), lambda b, i: (b, i, 0)),
                  pl.BlockSpec((1, tm, rw), lambda b, i: (b, i, 0)),
                  pl.BlockSpec((1, tm, rw), lambda b, i: (b, i, 0)),
                  pl.BlockSpec((1, tm, hy), lambda b, i: (b, i, 0)),
                  pl.BlockSpec((1, tm, d), lambda b, i: (b, i, 0)),
                  pl.BlockSpec((gr, d // 2), lambda b, i: (i, 0)),
                  pl.BlockSpec((GRID_W, d // 2), lambda b, i: (0, 0)),
                  pl.BlockSpec((1, 1, d), lambda b, i: (b, 0, 2)),
                  pl.BlockSpec((d, d), lambda b, i: (0, 0)),
                  pl.BlockSpec((1, d), lambda b, i: (0, 0)),
                  pl.BlockSpec((1, d), lambda b, i: (0, 0))],
        out_specs=pl.BlockSpec((1, tm, d), lambda b, i: (b, i, 0)),
        out_shape=jax.ShapeDtypeStruct((bsz, L, d), BF16),
        compiler_params=_cparams(("arbitrary", "arbitrary")),
        name="outproj_ln",
    )(o_f, o_b, gate, o_hy, x, er, ec, mod3, w_out, ln_g.reshape(1, d), ln_b.reshape(1, d))


def _ffn_kernel(x_ref, sh_ref, sc_ref, g2_ref, w1_ref, w3_ref, w2_ref, lng_ref, lnb_ref, o_ref, a_scr):
    f = pl.program_id(2)
    tm = x_ref.shape[1]

    @pl.when(f == 0)
    def _():
        a_scr[...] = (x_ref[0].astype(F32) * (1.0 + sc_ref[0]) + sh_ref[0]).astype(BF16)
        o_ref[0] = jnp.zeros((tm, o_ref.shape[2]), F32)

    a = a_scr[...]
    h1 = jnp.dot(a, w1_ref[...], preferred_element_type=F32)
    h3 = jnp.dot(a, w3_ref[...], preferred_element_type=F32)
    h = (h1 * _sigmoid(h1) * h3).astype(BF16)
    o_ref[0] += jnp.dot(h, w2_ref[...], preferred_element_type=F32)

    @pl.when(f == pl.num_programs(2) - 1)
    def _():
        for r0 in range(0, tm, FFN_LN_ROWS):
            rows = slice(r0, r0 + FFN_LN_ROWS)
            res = ALPHA * x_ref[0, rows, :].astype(F32) + g2_ref[0] * o_ref[0, rows, :]
            o_ref[0, rows, :] = _layer_norm(res, lng_ref[...], lnb_ref[...])


def _ffn(x1, mod3, w1, w3, w2, ln_g, ln_b, *, tm, tf):
    bsz, L, d = x1.shape
    dff = w1.shape[1]
    return pl.pallas_call(
        _ffn_kernel,
        grid=(bsz, L // tm, dff // tf),
        in_specs=[pl.BlockSpec((1, tm, d), lambda b, i, f: (b, i, 0)),
                  pl.BlockSpec((1, 1, d), lambda b, i, f: (b, 0, 3)),
                  pl.BlockSpec((1, 1, d), lambda b, i, f: (b, 0, 4)),
                  pl.BlockSpec((1, 1, d), lambda b, i, f: (b, 0, 5)),
                  pl.BlockSpec((d, tf), lambda b, i, f: (0, f)),
                  pl.BlockSpec((d, tf), lambda b, i, f: (0, f)),
                  pl.BlockSpec((tf, d), lambda b, i, f: (f, 0)),
                  pl.BlockSpec((1, d), lambda b, i, f: (0, 0)),
                  pl.BlockSpec((1, d), lambda b, i, f: (0, 0))],
        out_specs=pl.BlockSpec((1, tm, d), lambda b, i, f: (b, i, 0)),
        out_shape=jax.ShapeDtypeStruct((bsz, L, d), F32),
        scratch_shapes=[pltpu.VMEM((tm, d), BF16)],
        compiler_params=_cparams(("arbitrary", "arbitrary", "arbitrary")),
        name="ffn_ln",
    )(x1, mod3, mod3, mod3, w1, w3, w2, ln_g.reshape(1, d), ln_b.reshape(1, d))


def _pos_tables(rows, d):
    quarter = d // 4
    omega = 1.0 / (10000.0 ** (jnp.arange(quarter, dtype=F32) / quarter))
    er = jnp.arange(rows, dtype=F32)[:, None] * omega
    ec = jnp.arange(GRID_W, dtype=F32)[:, None] * omega
    er = jnp.concatenate([jnp.sin(er), jnp.cos(er)], -1)
    ec = jnp.concatenate([jnp.sin(ec), jnp.cos(ec)], -1)
    return er, ec


def _filter_positions(L):
    pos = jnp.arange(L, dtype=F32)[:, None]
    t = jnp.linspace(0.0, 1.0, L, dtype=F32)[:, None]
    bands = jnp.linspace(1e-4, N_BANDS - 1, N_BANDS, dtype=F32)[None, :]
    ang = 2.0 * math.pi * bands * pos / L
    z = jnp.concatenate([t, jnp.cos(ang), -jnp.sin(ang)], -1)
    return z, t


def _pad_cols(a, width):
    return jnp.pad(a, ((0, 0), (0, width - a.shape[1])))


def _pad_rows(a, height):
    return jnp.pad(a, ((0, height - a.shape[0]), (0, 0)))


def kernel(x, c, ctx, c_ctx, w_ada, b_ada, w_in, conv_rw, conv_hy, conv_hy_b, w0_f, w_up_f, a0_f, a_up_f, w0_b, w_up_b, a0_b, a_up_b, k_k, k_a, r_k, g_up, gn_g, gn_b, filt_w1, filt_b1, filt_w2, filt_b2, filt_w3, sin_freq, hy_bias, w_out, ln1_g, ln1_b, ffn_w1, ffn_w3, ffn_w2, ln2_g, ln2_b):
    bsz, L, d = x.shape
    rw = k_k.shape[1]
    hy = hy_bias.shape[1]
    dl, il, gl = w_up_f.shape[1], a_up_f.shape[1], g_up.shape[1]
    lo_w = 4 * LANE
    rkv_cols = 3 * rw

    wi = w_in[0]
    lo0 = rkv_cols

    def lora_layout(a):
        return jnp.concatenate([_pad_cols(a[:, lo0:lo0 + dl], LANE),
                                _pad_cols(a[:, lo0 + dl:lo0 + dl + il], LANE),
                                _pad_cols(a[:, lo0 + dl + il:lo0 + dl + il + gl], 2 * LANE)], axis=1)

    w_lat = _weight_layout(jnp.transpose(wi), rkv_cols=rkv_cols, lora_widths=(dl, il, gl), hy_cols=3 * hy)
    cw_lat = jnp.concatenate([conv_rw[0][:, :rkv_cols], conv_hy[0], lora_layout(conv_rw[0])], axis=1)
    cb_lat = jnp.concatenate([jnp.zeros((rkv_cols,), F32), conv_hy_b[0], jnp.zeros((lo_w,), F32)])[None, :]
    wup = _pad_rows(jnp.concatenate([w_up_f[0], w_up_b[0]], axis=1), LANE)
    aup = _pad_rows(jnp.concatenate([a_up_f[0], a_up_b[0]], axis=1), LANE)
    gup = _pad_rows(g_up[0], 2 * LANE)
    w0 = jnp.concatenate([w0_f[0], w0_b[0]])
    a0 = jnp.concatenate([a0_f[0], a0_b[0]])

    cc = _pad_rows(jnp.concatenate([c, c_ctx[None, :]], axis=0), SUB)
    mod = _adaln(cc, w_ada[0], b_ada[0])
    mod3 = mod.reshape(SUB, 1, 6 * d)

    er, ec = _pos_tables(L // GRID_W, d)

    tn = lo_w
    p_lat = _inproj(x, er, ec, mod3, w_lat, cw_lat, cb_lat, tm=1024, tn=tn, add_pos=True, ctx_row=None)
    ctx_tiles = list(range(rkv_cols // tn)) + [(rkv_cols + 3 * hy) // tn]
    p_ctx = _inproj(ctx, er, ec, mod3, w_lat, cw_lat, cb_lat, tm=ctx.shape[1], tn=tn, add_pos=False,
                    ctx_row=bsz, col_tiles=ctx_tiles)

    lo_cols = (rkv_cols + 3 * hy, rkv_cols + 3 * hy + lo_w)
    r, v, kk, gate, lw, kh, bvec = _rwkv_prep(p_lat, p_ctx, wup, aup, gup, w0, a0,
                                              k_k[0], k_a[0], rw=rw, rkv_cols=rkv_cols, lo_cols=lo_cols)
    o_f, o_b = _rwkv_scan(r, v, kk, lw, kh, bvec, r_k[0], gn_g[0], gn_b[0], n_lat_tokens=L)

    tabs = _fft_tables(L)
    zpos, tcol = _filter_positions(L)
    max_decay = math.log(DECAY_TARGET) / FAST_DECAY_PCT
    min_decay = math.log(DECAY_TARGET) / SLOW_DECAY_PCT
    deltas = jnp.abs(jnp.linspace(min_decay, max_decay, hy, dtype=F32))[None, :]
    kf = _filter_spectrum(_pad_cols(zpos, LANE), tcol, _pad_rows(filt_w1[0], LANE), filt_b1[0], filt_w2[0],
                          filt_b2[0], filt_w3[0], sin_freq[0], deltas, tabs, ct=LANE)
    o_hy = _hyena_conv(p_lat, hy_bias[0], kf, tabs, hy_col0=rkv_cols, hy_dim=hy, ct=LANE)

    x1 = _outproj(o_f, o_b, gate, o_hy, x, er, ec, mod3, w_out[0].astype(BF16), ln1_g[0], ln1_b[0], tm=512)
    return _ffn(x1, mod3, ffn_w1[0].astype(BF16), ffn_w3[0].astype(BF16), ffn_w2[0].astype(BF16),
                ln2_g[0], ln2_b[0], tm=1024, tf=512)
```

```python
import functools
import math

import numpy as np
import jax
import jax.numpy as jnp
from jax import lax
from jax.experimental import pallas as pl
from jax.experimental.pallas import tpu as pltpu

F32 = jnp.float32
BF16 = jnp.bfloat16

GRID_W = 64
HEAD_DIM = 64
N_BANDS = 16
DECAY_TARGET = 1e-2
FAST_DECAY_PCT = 0.3
SLOW_DECAY_PCT = 1.5
LN_EPS = 1e-5
GN_EPS = 64e-5
ALPHA = 2.0 ** 0.25

LANE = 128
SUB = 8
VMEM_LIMIT = 56 * 1024 * 1024

SCAN_CHUNK = 64
SCAN_GROUP = 2
SCAN_BLOCK = 128
FFN_LN_ROWS = 256
FFT_N2 = 64
FFT_UNROLL = 8


def _cparams(sem):
    return pltpu.CompilerParams(dimension_semantics=sem, vmem_limit_bytes=VMEM_LIMIT)


def _dot(a, b, dims=(((1,), (0,)), ((), ()))):
    return lax.dot_general(a.astype(BF16), b.astype(BF16), dims, preferred_element_type=F32)


def _dot_nt(a, b):
    return _dot(a, b, (((1,), (1,)), ((), ())))


def _dot_tn(a, b):
    return _dot(a, b, (((0,), (0,)), ((), ())))


def _split2(a):
    hi = a.astype(BF16)
    lo = (a - hi.astype(F32)).astype(BF16)
    return hi, lo


def _dot3(a, b, b_split=None):
    ah, al = _split2(a)
    bh, bl = _split2(b) if b_split is None else b_split
    return _dot(ah, bh) + (_dot(al, bh) + _dot(ah, bl))


def _dot_exact_lhs(a_exact, b):
    b1, b2 = _split2(b)
    return _dot(a_exact, b1) + _dot(a_exact, b2)


def _sigmoid(x):
    return 0.5 * jnp.tanh(0.5 * x) + 0.5


def _layer_norm(h, g, b):
    mu = jnp.mean(h, -1, keepdims=True)
    hc = h - mu
    var = jnp.mean(hc * hc, -1, keepdims=True)
    return hc * lax.rsqrt(var + LN_EPS) * g + b


def _adaln_kernel(c_ref, w_ref, b_ref, o_ref):
    cv = c_ref[...]
    a = cv * _sigmoid(cv)
    o_ref[...] = _dot3(a, w_ref[...]) + b_ref[...]


def _adaln(cc, w, b, tn=1024):
    m, d = cc.shape
    n = w.shape[1]
    return pl.pallas_call(
        _adaln_kernel,
        grid=(n // tn,),
        in_specs=[pl.BlockSpec((m, d), lambda j: (0, 0)),
                  pl.BlockSpec((d, tn), lambda j: (0, j)),
                  pl.BlockSpec((1, tn), lambda j: (0, j))],
        out_specs=pl.BlockSpec((m, tn), lambda j: (0, j)),
        out_shape=jax.ShapeDtypeStruct((m, n), F32),
        compiler_params=_cparams(("arbitrary",)),
        name="adaln",
    )(cc, w, b.reshape(1, n))


def _wlayout_kernel(wt_ref, o_ref, *, rkv_cols, lora_widths, hy_cols):
    tk = wt_ref.shape[1]

    def put(dst, rows):
        o_ref[:, dst:dst + rows.shape[0]] = jnp.transpose(rows).astype(o_ref.dtype)

    put(0, wt_ref[0:rkv_cols, :])
    src = rkv_cols
    pieces = []
    for width in lora_widths:
        padded = -(-width // LANE) * LANE
        pieces.append(wt_ref[src:src + width, :])
        if padded > width:
            pieces.append(jnp.zeros((padded - width, tk), F32))
        src += width
    put(rkv_cols, wt_ref[src:src + hy_cols, :])
    put(rkv_cols + hy_cols, jnp.concatenate(pieces, axis=0))


def _weight_layout(wt, *, rkv_cols, lora_widths, hy_cols, tk=256):
    n_src, k = wt.shape
    n_dst = rkv_cols + hy_cols + sum(-(-wd // LANE) * LANE for wd in lora_widths)
    return pl.pallas_call(
        functools.partial(_wlayout_kernel, rkv_cols=rkv_cols, lora_widths=lora_widths, hy_cols=hy_cols),
        grid=(k // tk,),
        in_specs=[pl.BlockSpec((n_src, tk), lambda i: (0, i))],
        out_specs=pl.BlockSpec((tk, n_dst), lambda i: (i, 0)),
        out_shape=jax.ShapeDtypeStruct((k, n_dst), BF16),
        compiler_params=_cparams(("arbitrary",)),
        name="weight_layout",
    )(wt)


HALO = 16


def _add_pos(xv, er_row, ec_rows):
    half = ec_rows.shape[1]
    return jnp.concatenate([xv[:, :half] + er_row, xv[:, half:] + ec_rows], axis=1)


def _inproj_kernel(x_ref, xp_ref, xn_ref, er_ref, ec_ref, sh_ref, sc_ref, w_ref, cw_ref, cb_ref, o_ref, a_scr,
                   *, add_pos):
    i = pl.program_id(1)
    n_i = pl.num_programs(1)
    tm = x_ref.shape[1]
    gr = tm // GRID_W

    @pl.when(pl.program_id(2) == 0)
    def _():
        scale = 1.0 + sc_ref[0]
        shift = sh_ref[0]

        def put(rows, xv, keep=None):
            a = xv * scale + shift
            if keep is not None:
                a = jnp.where(keep, a, 0.0)
            a_scr[rows, :] = a.astype(BF16)

        has_prev = i > 0
        has_next = i < n_i - 1
        lo_rows, hi_rows = slice(0, HALO), slice(HALO + tm, 2 * HALO + tm)
        if add_pos:
            n_er = er_ref.shape[0]
            for q in range(gr):
                rows = slice(q * GRID_W, (q + 1) * GRID_W)
                put(slice(HALO + q * GRID_W, HALO + (q + 1) * GRID_W),
                    _add_pos(x_ref[0, rows, :], er_ref[pl.ds(i * gr + q, 1), :], ec_ref[...]))
            put(lo_rows, _add_pos(xp_ref[0], er_ref[pl.ds(jnp.maximum(i * gr - 1, 0), 1), :],
                                  ec_ref[GRID_W - HALO:, :]), has_prev)
            put(hi_rows, _add_pos(xn_ref[0], er_ref[pl.ds(jnp.minimum((i + 1) * gr, n_er - 1), 1), :],
                                  ec_ref[:HALO, :]), has_next)
        else:
            put(lo_rows, xp_ref[0], has_prev)
            put(slice(HALO, HALO + tm), x_ref[0])
            put(hi_rows, xn_ref[0], has_next)

    n_rows = tm + 2 * HALO
    half = w_ref.shape[1] // 2
    for h in range(2):
        cols = slice(h * half, (h + 1) * half)
        res = jnp.dot(a_scr[...], w_ref[:, cols], preferred_element_type=F32)
        out = (pltpu.roll(res, 1, 0) * cw_ref[0:1, cols] + res * cw_ref[1:2, cols]
               + pltpu.roll(res, n_rows - 1, 0) * cw_ref[2:3, cols] + cb_ref[:, cols])
        o_ref[0, :, cols] = out[HALO:HALO + tm].astype(o_ref.dtype)


def _inproj(x, er, ec, mod3, w, cw, cb, *, tm, tn, add_pos, ctx_row, col_tiles=None):
    bsz, lx, d = x.shape
    if col_tiles is None:
        n = w.shape[1]
        col = lambda j: j
    else:
        n = len(col_tiles) * tn
        run = col_tiles[:-1]
        assert run == list(range(len(run)))
        col = lambda j: jnp.where(j < len(run), j, col_tiles[-1])
    hb = tm // HALO
    n_hb = lx // HALO
    row = (lambda b: ctx_row) if ctx_row is not None else (lambda b: b)
    return pl.pallas_call(
        functools.partial(_inproj_kernel, add_pos=add_pos),
        grid=(bsz, lx // tm, n // tn),
        in_specs=[pl.BlockSpec((1, tm, d), lambda b, i, j: (b, i, 0)),
                  pl.BlockSpec((1, HALO, d), lambda b, i, j: (b, jnp.maximum(i * hb - 1, 0), 0)),
                  pl.BlockSpec((1, HALO, d), lambda b, i, j: (b, jnp.minimum((i + 1) * hb, n_hb - 1), 0)),
                  pl.BlockSpec(er.shape, lambda b, i, j: (0, 0)),
                  pl.BlockSpec(ec.shape, lambda b, i, j: (0, 0)),
                  pl.BlockSpec((1, 1, d), lambda b, i, j: (row(b), 0, 0)),
                  pl.BlockSpec((1, 1, d), lambda b, i, j: (row(b), 0, 1)),
                  pl.BlockSpec((d, tn), lambda b, i, j: (0, col(j))),
                  pl.BlockSpec((3, tn), lambda b, i, j: (0, col(j))),
                  pl.BlockSpec((1, tn), lambda b, i, j: (0, col(j)))],
        out_specs=pl.BlockSpec((1, tm, tn), lambda b, i, j: (b, i, j)),
        out_shape=jax.ShapeDtypeStruct((bsz, lx, n), BF16),
        scratch_shapes=[pltpu.VMEM((tm + 2 * HALO, d), BF16)],
        compiler_params=_cparams(("arbitrary", "arbitrary", "arbitrary")),
        name="inproj_pos" if add_pos else "inproj_ctx",
    )(x, x, x, er, ec, mod3, mod3, w, cw, cb)


def _head_sum(x, ones_bd):
    rows = x.shape[0]
    outs = []
    for j in range(x.shape[1] // LANE):
        hi, lo = _split2(x[:, j * LANE:(j + 1) * LANE])
        s = _dot(jnp.concatenate([hi, lo], axis=0), ones_bd)
        outs.append(s[:rows] + s[rows:])
    return jnp.concatenate(outs, axis=1)


def _prep_kernel(p_rkv, p_lo, pc_rkv, pc_lo, wup, aup, gup, w0, a0, kk_w, ka_w, ones_ref,
                 r_out, v_out, kk_out, gate_out, lw_out, kh_out, b_out, *, n_lat):
    is_ctx = pl.program_id(1) == n_lat
    rw = kk_out.shape[2]
    ones_bd = ones_ref[...]
    wup_split = _split2(wup[...])
    aup_split = _split2(aup[...])
    gup_bf = gup[...].astype(BF16)

    def row_block(rows):
        def rkv(s):
            cols = slice(s * rw, (s + 1) * rw)
            return jnp.where(is_ctx, pc_rkv[0, rows, cols], p_rkv[0, rows, cols])

        r_out[0, rows, :] = rkv(0)
        v_out[0, rows, :] = rkv(2)
        k = rkv(1).astype(F32)
        kkr = k * kk_w[...]
        nrm = jnp.sqrt(_head_sum(kkr * kkr, ones_bd))
        kk = kkr / jnp.maximum(nrm, 1e-12)
        kk_out[0, rows, :] = kk.astype(kk_out.dtype)

        lo = jnp.where(is_ctx, pc_lo[0, rows, :], p_lo[0, rows, :]).astype(F32)
        tw = jnp.tanh(lo[:, 0:LANE])
        al = lo[:, LANE:2 * LANE]
        gl = _sigmoid(lo[:, 2 * LANE:])
        gate_out[0, rows, :] = _dot(gl, gup_bf).astype(gate_out.dtype)
        wx = _dot3(tw, None, wup_split) + w0[...]
        ax = _dot3(al, None, aup_split) + a0[...]
        for d in range(2):
            cols = slice(d * rw, (d + 1) * rw)
            lw_out[d, 0, rows, :] = -math.exp(-0.5) * _sigmoid(wx[:, cols])
            ia = _sigmoid(ax[:, cols])
            kh_out[d, 0, rows, :] = (k * (1.0 + (ia - 1.0) * ka_w[...])).astype(kh_out.dtype)
            b_out[d, 0, rows, :] = (kk * ia).astype(b_out.dtype)

    row_block(slice(0, kk_out.shape[1]))


def _rwkv_prep(p, pc, wup, aup, gup, w0, a0, k_k, k_a, *, rw, rkv_cols, lo_cols):
    bsz, lq, _ = p.shape
    ctx = pc.shape[1]
    tr = ctx
    n_lat = lq // tr
    t_all = lq + ctx
    lo_w = lo_cols[1] - lo_cols[0]
    lo_blk = lo_cols[0] // lo_w
    lo_blk_c = rkv_cols // lo_w
    lat = lambda i: jnp.minimum(i, n_lat - 1)
    ones_bd = (np.arange(LANE)[:, None] // HEAD_DIM == np.arange(LANE)[None, :] // HEAD_DIM)
    ones_bd = jnp.asarray(ones_bd, BF16)
    full = lambda a: pl.BlockSpec(a.shape, lambda b, i: (0,) * a.ndim)
    row = lambda a: a.reshape(1, -1)
    weights = [wup, aup, gup, row(w0), row(a0), row(k_k), row(k_a), ones_bd]
    shared = jax.ShapeDtypeStruct((bsz, t_all, rw), BF16)
    perdir = jax.ShapeDtypeStruct((2, bsz, t_all, rw), BF16)
    perdir_f32 = jax.ShapeDtypeStruct((2, bsz, t_all, rw), F32)
    o_shared = pl.BlockSpec((1, tr, rw), lambda b, i: (b, i, 0))
    o_dir = pl.BlockSpec((2, 1, tr, rw), lambda b, i: (0, b, i, 0))
    return pl.pallas_call(
        functools.partial(_prep_kernel, n_lat=n_lat),
        grid=(bsz, n_lat + 1),
        in_specs=[pl.BlockSpec((1, tr, rkv_cols), lambda b, i: (b, lat(i), 0)),
                  pl.BlockSpec((1, tr, lo_w), lambda b, i: (b, lat(i), lo_blk)),
                  pl.BlockSpec((1, tr, rkv_cols), lambda b, i: (b, 0, 0)),
                  pl.BlockSpec((1, tr, lo_w), lambda b, i: (b, 0, lo_blk_c))]
                 + [full(a) for a in weights],
        out_specs=[o_shared, o_shared, o_shared, o_shared, o_dir, o_dir, o_dir],
        out_shape=[shared, shared, shared, shared, perdir_f32, perdir, perdir],
        compiler_params=_cparams(("arbitrary", "arbitrary")),
        name="rwkv_prep",
    )(p, p, pc, pc, *weights)


def _scan_kernel(rf_ref, vf_ref, kkf_ref, lwf_ref, khf_ref, bf_ref, rb_ref, vb_ref, kkb_ref, lwb_ref, khb_ref,
                 bb_ref, rk_ref, gng_ref, gnb_ref, ones_ref, of_ref, ob_ref, s_scr):
    C = SCAN_CHUNK
    n_sub = rf_ref.shape[1] // C
    G = SCAN_GROUP
    W = G * HEAD_DIM
    n = G * C
    n_groups = rf_ref.shape[2] // W

    @pl.when(pl.program_id(1) == 0)
    def _():
        s_scr[...] = jnp.zeros_like(s_scr)

    bi = lax.broadcasted_iota(jnp.int32, (n, n), 0)
    bj = lax.broadcasted_iota(jnp.int32, (n, n), 1)
    same = (bi // C) == (bj // C)
    li = bi % C
    lj = bj % C
    eye = jnp.where(bi == bj, 1.0, 0.0)
    hrow = lax.broadcasted_iota(jnp.int32, (n, W), 0) // C
    hlane = lax.broadcasted_iota(jnp.int32, (n, W), 1) // HEAD_DIM
    hmask = hrow == hlane
    ti = lax.broadcasted_iota(jnp.int32, (C, C), 0)
    tj = lax.broadcasted_iota(jnp.int32, (C, C), 1)

    def rep(x):
        return jnp.where(hmask, jnp.concatenate([x] * G, axis=0), 0.0).astype(BF16)

    chains = []
    dirs = ((rf_ref, vf_ref, kkf_ref, lwf_ref, khf_ref, bf_ref, False),
            (rb_ref, vb_ref, kkb_ref, lwb_ref, khb_ref, bb_ref, True))
    per_chunk = {}
    for di, (r_ref, v_ref, kk_ref, lw_ref, kh_ref, b_ref, rev) in enumerate(dirs):
        before = (lj > li) if rev else (lj < li)
        strict = jnp.logical_and(same, before)
        incl = jnp.logical_and(same, jnp.logical_or(before, li == lj))
        levels = []
        s = 1
        while s < C:
            later, earlier = (lj, li) if rev else (li, lj)
            off = jnp.logical_and(jnp.logical_and((later // s) % 2 == 1, (earlier // s) % 2 == 0),
                                  (li // (2 * s)) == (lj // (2 * s)))
            levels.append(jnp.logical_and(same, off))
            s *= 2
        tri = jnp.where((tj >= ti) if rev else (tj <= ti), 1.0, 0.0).astype(BF16)
        for t in range(n_sub):
            sub = n_sub - 1 - t if rev else t
            rows = slice(sub * C, (sub + 1) * C)
            r = r_ref[0, rows, :].astype(F32)
            v = v_ref[0, rows, :].astype(F32)
            kk = kk_ref[0, rows, :].astype(F32)
            lw = lw_ref[0, 0, rows, :]
            kh = kh_ref[0, 0, rows, :].astype(F32)
            bv = b_ref[0, 0, rows, :].astype(F32)
            cum = _dot_exact_lhs(tri, lw)
            tot = cum[0:1, :] if rev else cum[C - 1:C, :]
            e_neg = jnp.exp(-cum)
            e_end = jnp.exp(tot - cum)
            rt = r * jnp.exp(cum)
            at = -kk * jnp.exp(cum - lw)
            bt = bv * e_neg
            kt = kh * e_neg
            bp = bv * e_end
            kp = kh * e_end
            per_chunk[(di, t)] = (r, v, kh, jnp.exp(tot), rows)
            for g in range(n_groups):
                cols = slice(g * W, (g + 1) * W)
                chains.append(dict(
                    di=di, t=t, g=g, cols=cols, strict=strict, incl=incl, levels=levels,
                    v2=rep(v[:, cols]),
                    ar=jnp.concatenate([rep(at[:, cols]), rep(rt[:, cols])], axis=0),
                    bk=jnp.concatenate([rep(bt[:, cols]), rep(kt[:, cols])], axis=0),
                    bkp=jnp.concatenate([rep(bp[:, cols]), rep(kp[:, cols])], axis=0)))

    for ch in chains:
        ch["tt"] = _dot_nt(ch["ar"], ch["bk"])
    for ch in chains:
        ch["p"] = jnp.where(ch["strict"], ch["tt"][:n, :n], 0.0)
        ch["tinv"] = eye + jnp.where(ch["levels"][0], ch["p"], 0.0)
    for lvl in range(1, len(chains[0]["levels"])):
        for ch in chains:
            ch["e"] = _dot(jnp.where(ch["levels"][lvl], ch["p"], 0.0), ch["tinv"])
        for ch in chains:
            ch["tinv"] = ch["tinv"] + _dot(ch["tinv"], ch["e"])
    state = {(di, g): s_scr[di, g] for di in range(2) for g in range(n_groups)}
    for t in range(n_sub):
        cur = [ch for ch in chains if ch["t"] == t]
        for ch in cur:
            ch["s0"] = state[(ch["di"], ch["g"])]
            ch["xs"] = _dot_nt(ch["ar"], ch["s0"])
        for ch in cur:
            a_ak = jnp.where(ch["strict"], ch["tt"][:n, n:], 0.0)
            ch["rhs"] = ch["xs"][:n] + _dot(a_ak, ch["v2"])
        for ch in cur:
            u = _dot(ch["tinv"], ch["rhs"])
            ch["uv"] = jnp.concatenate([u.astype(BF16), ch["v2"]], axis=0)
        for ch in cur:
            e_tot = per_chunk[(ch["di"], t)][3]
            state[(ch["di"], ch["g"])] = ch["s0"] * e_tot[:, ch["cols"]] + _dot_tn(ch["uv"], ch["bkp"])
        for ch in cur:
            a_r = jnp.concatenate([jnp.where(ch["incl"], ch["tt"][n:, :n], 0.0),
                                   jnp.where(ch["incl"], ch["tt"][n:, n:], 0.0)], axis=1)
            ch["y2"] = ch["xs"][n:] + _dot(a_r, ch["uv"])
    for (di, g), s_new in state.items():
        s_scr[di, g] = s_new

    ones_bd = ones_ref[...]
    inv = 1.0 / HEAD_DIM
    for di, o_ref in enumerate((of_ref, ob_ref)):
        for t in range(n_sub):
            r, v, kh, _, rows = per_chunk[(di, t)]
            ys = []
            for ch in chains:
                if ch["di"] == di and ch["t"] == t:
                    y = ch["y2"][0:C]
                    for h in range(1, G):
                        y = y + ch["y2"][h * C:(h + 1) * C]
                    ys.append(y)
            y = jnp.concatenate(ys, axis=1)
            mu = _head_sum(y, ones_bd) * inv
            yc = y - mu
            var = _head_sum(yc * yc, ones_bd) * inv
            gn = yc * lax.rsqrt(var + GN_EPS) * gng_ref[...] + gnb_ref[...]
            bonus = _head_sum(r * kh * rk_ref[...], ones_bd) * v
            o_ref[0, rows, :] = gn + bonus


def _rwkv_scan(r, v, kk, lw, kh, bvec, r_k, gn_g, gn_b, *, n_lat_tokens):
    bsz, t_all, rw = r.shape
    C = SCAN_BLOCK
    n_chunks = t_all // C
    n_ctx = (t_all - n_lat_tokens) // C
    n_lat = n_lat_tokens // C
    W = SCAN_GROUP * HEAD_DIM

    blk_f = lambda c: jnp.where(c < n_ctx, n_lat + c, c - n_ctx)
    blk_b = lambda c: n_chunks - 1 - c
    ones_bd = (np.arange(LANE)[:, None] // HEAD_DIM == np.arange(LANE)[None, :] // HEAD_DIM)
    ones_bd = jnp.asarray(ones_bd, BF16)
    shared = lambda blk: pl.BlockSpec((1, C, rw), lambda b, c: (b, blk(c), 0))
    perdir = lambda d, blk: pl.BlockSpec((1, 1, C, rw), lambda b, c: (d, b, blk(c), 0))
    vec = pl.BlockSpec((1, rw), lambda b, c: (0, 0))
    out = jax.ShapeDtypeStruct((bsz, t_all, rw), F32)
    return pl.pallas_call(
        _scan_kernel,
        grid=(bsz, n_chunks),
        in_specs=[shared(blk_f), shared(blk_f), shared(blk_f), perdir(0, blk_f), perdir(0, blk_f), perdir(0, blk_f),
                  shared(blk_b), shared(blk_b), shared(blk_b), perdir(1, blk_b), perdir(1, blk_b), perdir(1, blk_b),
                  vec, vec, vec, pl.BlockSpec((LANE, LANE), lambda b, c: (0, 0))],
        out_specs=[shared(blk_f), shared(blk_b)],
        out_shape=[out, out],
        scratch_shapes=[pltpu.VMEM((2, rw // W, W, W), F32)],
        compiler_params=_cparams(("arbitrary", "arbitrary")),
        name="rwkv_scan",
    )(r, v, kk, lw, kh, bvec, r, v, kk, lw, kh, bvec,
      r_k.reshape(1, rw), gn_g.reshape(1, rw), gn_b.reshape(1, rw), ones_bd)


def _fft_tables(L):
    n2 = FFT_N2
    n_all = 2 * L
    n1 = n_all // n2
    nh = n1 // 2
    two_pi = 2.0 * np.pi
    kh = -(-(nh + 1) // FFT_UNROLL) * FFT_UNROLL
    k1 = np.arange(kh)
    pair_w = np.where(k1 > nh, 0.0, np.where((k1 == 0) | (k1 == nh), 1.0, 2.0))
    th1 = two_pi * np.outer(k1, np.arange(nh)) / n1
    f1 = np.concatenate([np.cos(th1), -np.sin(th1)], 0)
    th2 = two_pi * np.outer(np.arange(n2), np.arange(n2)) / n2
    fr, fi = np.cos(th2), -np.sin(th2)
    a3 = np.block([[fr, -fi], [fi, fr]])
    b3 = np.block([[-fi, -fr], [fr, -fi]])
    tht = two_pi * np.outer(k1, np.arange(n2)) / n_all
    twr = np.tile(np.cos(tht), (1, 2))
    twi = np.tile(-np.sin(tht), (1, 2))
    gr, gi = np.cos(th2), np.sin(th2)
    m3i = np.block([[gr, -gi], [gi, gr]]) / n_all
    th4 = two_pi * np.outer(np.arange(nh), k1) / n1
    cos4, sin4 = np.cos(th4) * pair_w, np.sin(th4) * pair_w
    ca = np.concatenate([cos4, -sin4], 1)
    cb = np.concatenate([-sin4, -cos4], 1)
    th5 = two_pi * np.outer(np.arange(n2), k1) / n_all
    t2r, t2i = np.tile(np.cos(th5), (1, 2)), np.tile(np.sin(th5), (1, 2))
    f = lambda a: jnp.asarray(a, F32)
    return dict(f1=f(f1), a3=f(a3), b3=f(b3), twr=f(twr), twi=f(twi),
                m3i=f(m3i), ca=f(ca), cb=f(cb), t2r=f(t2r), t2i=f(t2i))


def _fft_stage1(src_ref, f1_ref, s_ref):
    kh2, nh = f1_ref.shape
    n2 = FFT_N2
    f1 = f1_ref[...].astype(BF16)

    def body(nlo, carry):
        xin = src_ref[pl.ds(nlo, nh, stride=n2), :]
        s_ref[pl.ds(pl.multiple_of(nlo * kh2, SUB), kh2), :] = _dot(f1, xin)
        return carry

    lax.fori_loop(0, n2, body, 0, unroll=FFT_UNROLL)


def _fft_mid_matrix(k1, a3_ref, b3_ref, twr_ref, twi_ref):
    tr = twr_ref[pl.ds(k1, 1), :]
    ti = twi_ref[pl.ds(k1, 1), :]
    return (a3_ref[...] * tr + b3_ref[...] * ti).astype(BF16)


def _fft_load_k1(s_ref, k1, kh2):
    n2 = FFT_N2
    re = s_ref[pl.ds(k1, n2, stride=kh2), :]
    im = s_ref[pl.ds(kh2 // 2 + k1, n2, stride=kh2), :]
    return jnp.concatenate([re, im], axis=0)


def _hyena_kernel(x0_ref, x1_ref, v_ref, bias_ref, kf_ref,
                  f1_ref, a3_ref, b3_ref, twr_ref, twi_ref, m3i_ref, ca_ref, cb_ref, t2r_ref, t2i_ref,
                  o_ref, z_scr, y_scr, s_scr, q_scr):
    n2 = FFT_N2
    kh2, nh = f1_ref.shape
    kh = kh2 // 2
    z_scr[...] = v_ref[0].astype(F32) * x1_ref[0].astype(F32)
    _fft_stage1(z_scr, f1_ref, s_scr)
    m3i = m3i_ref[...].astype(BF16)

    def mid(k1, carry):
        m = _fft_mid_matrix(k1, a3_ref, b3_ref, twr_ref, twi_ref)
        cc = _dot(m, _fft_load_k1(s_scr, k1, kh2))
        kf = kf_ref[0, k1]
        cr, ci = cc[:n2], cc[n2:]
        kr, ki = kf[:n2], kf[n2:]
        pp = jnp.concatenate([cr * kr - ci * ki, cr * ki + ci * kr], axis=0)
        q = _dot(m3i, pp)
        q_scr[pl.ds(k1, n2, stride=kh2), :] = q[:n2]
        q_scr[pl.ds(kh + k1, n2, stride=kh2), :] = q[n2:]
        return carry

    lax.fori_loop(0, kh, mid, 0, unroll=FFT_UNROLL)

    def last(m2, carry):
        f4 = ca_ref[...] * t2r_ref[pl.ds(m2, 1), :] + cb_ref[...] * t2i_ref[pl.ds(m2, 1), :]
        qm = q_scr[pl.ds(pl.multiple_of(m2 * kh2, SUB), kh2), :]
        y_scr[pl.ds(m2, nh, stride=n2), :] = _dot(f4, qm)
        return carry

    lax.fori_loop(0, n2, last, 0, unroll=FFT_UNROLL)
    z = z_scr[...]
    o_ref[0] = (y_scr[...] + z * bias_ref[...]) * x0_ref[0].astype(F32)


def _hyena_conv(p, hy_bias, kf, tabs, *, hy_col0, hy_dim, ct):
    bsz, L, _ = p.shape
    n_tiles = hy_dim // ct
    blk0 = hy_col0 // ct
    sec = hy_dim // ct
    xspec = lambda s: pl.BlockSpec((1, L, ct), lambda t, b: (b, 0, blk0 + s * sec + t))
    tab_list = [tabs[k] for k in ("f1", "a3", "b3", "twr", "twi", "m3i", "ca", "cb", "t2r", "t2i")]
    full = lambda a: pl.BlockSpec(a.shape, lambda t, b: (0,) * a.ndim)
    kh2 = tabs["f1"].shape[0]
    return pl.pallas_call(
        _hyena_kernel,
        grid=(n_tiles, bsz),
        in_specs=[xspec(0), xspec(1), xspec(2),
                  pl.BlockSpec((1, ct), lambda t, b: (0, t)),
                  pl.BlockSpec((1,) + kf.shape[1:], lambda t, b: (t, 0, 0, 0), pipeline_mode=pl.Buffered(1))]
                 + [full(a) for a in tab_list],
        out_specs=pl.BlockSpec((1, L, ct), lambda t, b: (b, 0, t)),
        out_shape=jax.ShapeDtypeStruct((bsz, L, hy_dim), F32),
        scratch_shapes=[pltpu.VMEM((L, ct), F32), pltpu.VMEM((L, ct), F32),
                        pltpu.VMEM((FFT_N2 * kh2, ct), F32), pltpu.VMEM((FFT_N2 * kh2, ct), F32)],
        compiler_params=_cparams(("arbitrary", "arbitrary")),
        name="hyena_conv",
    )(p, p, p, hy_bias.reshape(1, -1), kf, *tab_list)


def _filter_kernel(z_ref, t_ref, w1_ref, b1_ref, w2_ref, b2_ref, w3f_ref, w3b_ref, freq_ref, delta_ref,
                   f1_ref, a3_ref, b3_ref, twr_ref, twi_ref, kf_ref, h_scr, hf_scr, hb_scr, sf_scr, sb_scr):
    n2 = FFT_N2
    kh2 = f1_ref.shape[0]

    @pl.when(pl.program_id(0) == 0)
    def _():
        freq = freq_ref[...]
        h1 = jnp.sin(freq * (_dot3(z_ref[...], w1_ref[...]) + b1_ref[...]))
        h_scr[...] = jnp.sin(freq * (_dot3(h1, w2_ref[...]) + b2_ref[...]))

    h = h_scr[...]
    window = jnp.exp(-t_ref[...] * delta_ref[...])
    hf_scr[...] = _dot3(h, w3f_ref[...]) * window
    hb = _dot3(h, w3b_ref[...]) * window
    rows = lax.broadcasted_iota(jnp.int32, hb.shape, 0)
    hb_scr[...] = jnp.where(rows == 0, 0.0, hb)
    _fft_stage1(hf_scr, f1_ref, sf_scr)
    _fft_stage1(hb_scr, f1_ref, sb_scr)

    def mid(k1, carry):
        m = _fft_mid_matrix(k1, a3_ref, b3_ref, twr_ref, twi_ref)
        cf = _dot(m, _fft_load_k1(sf_scr, k1, kh2))
        cb = _dot(m, _fft_load_k1(sb_scr, k1, kh2))
        kf_ref[0, k1] = jnp.concatenate([cf[:n2] + cb[:n2], cf[n2:] - cb[n2:]], axis=0)
        return carry

    lax.fori_loop(0, kh2 // 2, mid, 0, unroll=FFT_UNROLL)


def _filter_spectrum(zpos, tcol, w1, b1, w2, b2, w3, freq, deltas, tabs, *, ct):
    L = zpos.shape[0]
    hy_dim = deltas.shape[1]
    n_tiles = hy_dim // ct
    kh2 = tabs["f1"].shape[0]
    kh = kh2 // 2
    tab_list = [tabs[k] for k in ("f1", "a3", "b3", "twr", "twi")]
    full = lambda a: pl.BlockSpec(a.shape, lambda t: (0,) * a.ndim)
    row = lambda a: a.reshape(1, -1)
    small = [zpos, tcol, w1, row(b1), w2, row(b2)]
    return pl.pallas_call(
        _filter_kernel,
        grid=(n_tiles,),
        in_specs=[full(a) for a in small]
                 + [pl.BlockSpec((w3.shape[0], ct), lambda t: (0, t)),
                    pl.BlockSpec((w3.shape[0], ct), lambda t: (0, n_tiles + t)),
                    full(row(freq)),
                    pl.BlockSpec((1, ct), lambda t: (0, t))]
                 + [full(a) for a in tab_list],
        out_specs=pl.BlockSpec((1, kh, 2 * FFT_N2, ct), lambda t: (t, 0, 0, 0)),
        out_shape=jax.ShapeDtypeStruct((n_tiles, kh, 2 * FFT_N2, ct), F32),
        scratch_shapes=[pltpu.VMEM((L, w2.shape[1]), F32), pltpu.VMEM((L, ct), F32), pltpu.VMEM((L, ct), F32),
                        pltpu.VMEM((FFT_N2 * kh2, ct), F32), pltpu.VMEM((FFT_N2 * kh2, ct), F32)],
        compiler_params=_cparams(("arbitrary",)),
        name="hyena_filter",
    )(*small, w3, w3, row(freq), deltas, *tab_list)


def _outproj_kernel(of_ref, ob_ref, gate_ref, ohy_ref, x_ref, er_ref, ec_ref, g1_ref, w_ref, lng_ref, lnb_ref,
                    o_ref):
    rw = of_ref.shape[2]
    a_rw = ((of_ref[0] + ob_ref[0]) * gate_ref[0].astype(F32)).astype(BF16)
    mix = jnp.dot(a_rw, w_ref[0:rw, :], preferred_element_type=F32)
    mix = mix + jnp.dot(ohy_ref[0].astype(BF16), w_ref[rw:, :], preferred_element_type=F32)
    gmix = g1_ref[0] * mix
    tm = x_ref.shape[1]
    for q in range(tm // GRID_W):
        rows = slice(q * GRID_W, (q + 1) * GRID_W)
        xv = _add_pos(x_ref[0, rows, :], er_ref[q:q + 1, :], ec_ref[...])
        o_ref[0, rows, :] = _layer_norm(ALPHA * xv + gmix[rows], lng_ref[...], lnb_ref[...]).astype(o_ref.dtype)


def _outproj(o_f, o_b, gate, o_hy, x, er, ec, mod3, w_out, ln_g, ln_b, *, tm):
    bsz, L, d = x.shape
    rw = gate.shape[2]
    hy = o_hy.shape[2]
    gr = tm // GRID_W
    return pl.pallas_call(
        _outproj_kernel,
        grid=(bsz, L // tm),
        in_specs=[pl.BlockSpec((1, tm, rw), lambda b, i: (b, i, 0)),
                  pl.BlockSpec((1, tm, rw), lambda b, i: (b, i, 0)),
                  pl.BlockSpec((1, tm, rw), lambda b, i: (b, i, 0)),
                  pl.BlockSpec((1, tm, hy), lambda b, i: (b, i, 0)),
                  pl.BlockSpec((1, tm, d), lambda b, i: (b, i, 0)),
                  pl.BlockSpec((gr, d // 2), lambda b, i: (i, 0)),
                  pl.BlockSpec((GRID_W, d // 2), lambda b, i: (0, 0)),
                  pl.BlockSpec((1, 1, d), lambda b, i: (b, 0, 2)),
                  pl.BlockSpec((d, d), lambda b, i: (0, 0)),
                  pl.BlockSpec((1, d), lambda b, i: (0, 0)),
                  pl.BlockSpec((1, d), lambda b, i: (0, 0))],
        out_specs=pl.BlockSpec((1, tm, d), lambda b, i: (b, i, 0)),
        out_shape=jax.ShapeDtypeStruct((bsz, L, d), BF16),
        compiler_params=_cparams(("arbitrary", "arbitrary")),
        name="outproj_ln",
    )(o_f, o_b, gate, o_hy, x, er, ec, mod3, w_out, ln_g.reshape(1, d), ln_b.reshape(1, d))


def _ffn_kernel(x_ref, sh_ref, sc_ref, g2_ref, w1_ref, w3_ref, w2_ref, lng_ref, lnb_ref, o_ref, a_scr):
    f = pl.program_id(2)
    tm = x_ref.shape[1]

    @pl.when(f == 0)
    def _():
        a_scr[...] = (x_ref[0].astype(F32) * (1.0 + sc_ref[0]) + sh_ref[0]).astype(BF16)
        o_ref[0] = jnp.zeros((tm, o_ref.shape[2]), F32)

    a = a_scr[...]
    h1 = jnp.dot(a, w1_ref[...], preferred_element_type=F32)
    h3 = jnp.dot(a, w3_ref[...], preferred_element_type=F32)
    h = (h1 * _sigmoid(h1) * h3).astype(BF16)
    o_ref[0] += jnp.dot(h, w2_ref[...], preferred_element_type=F32)

    @pl.when(f == pl.num_programs(2) - 1)
    def _():
        for r0 in range(0, tm, FFN_LN_ROWS):
            rows = slice(r0, r0 + FFN_LN_ROWS)
            res = ALPHA * x_ref[0, rows, :].astype(F32) + g2_ref[0] * o_ref[0, rows, :]
            o_ref[0, rows, :] = _layer_norm(res, lng_ref[...], lnb_ref[...])


def _ffn(x1, mod3, w1, w3, w2, ln_g, ln_b, *, tm, tf):
    bsz, L, d = x1.shape
    dff = w1.shape[1]
    return pl.pallas_call(
        _ffn_kernel,
        grid=(bsz, L // tm, dff // tf),
        in_specs=[pl.BlockSpec((1, tm, d), lambda b, i, f: (b, i, 0)),
                  pl.BlockSpec((1, 1, d), lambda b, i, f: (b, 0, 3)),
                  pl.BlockSpec((1, 1, d), lambda b, i, f: (b, 0, 4)),
                  pl.BlockSpec((1, 1, d), lambda b, i, f: (b, 0, 5)),
                  pl.BlockSpec((d, tf), lambda b, i, f: (0, f)),
                  pl.BlockSpec((d, tf), lambda b, i, f: (0, f)),
                  pl.BlockSpec((tf, d), lambda b, i, f: (f, 0)),
                  pl.BlockSpec((1, d), lambda b, i, f: (0, 0)),
                  pl.BlockSpec((1, d), lambda b, i, f: (0, 0))],
        out_specs=pl.BlockSpec((1, tm, d), lambda b, i, f: (b, i, 0)),
        out_shape=jax.ShapeDtypeStruct((bsz, L, d), F32),
        scratch_shapes=[pltpu.VMEM((tm, d), BF16)],
        compiler_params=_cparams(("arbitrary", "arbitrary", "arbitrary")),
        name="ffn_ln",
    )(x1, mod3, mod3, mod3, w1, w3, w2, ln_g.reshape(1, d), ln_b.reshape(1, d))


def _pos_tables(rows, d):
    quarter = d // 4
    omega = 1.0 / (10000.0 ** (jnp.arange(quarter, dtype=F32) / quarter))
    er = jnp.arange(rows, dtype=F32)[:, None] * omega
    ec = jnp.arange(GRID_W, dtype=F32)[:, None] * omega
    er = jnp.concatenate([jnp.sin(er), jnp.cos(er)], -1)
    ec = jnp.concatenate([jnp.sin(ec), jnp.cos(ec)], -1)
    return er, ec


def _filter_positions(L):
    pos = jnp.arange(L, dtype=F32)[:, None]
    t = jnp.linspace(0.0, 1.0, L, dtype=F32)[:, None]
    bands = jnp.linspace(1e-4, N_BANDS - 1, N_BANDS, dtype=F32)[None, :]
    ang = 2.0 * math.pi * bands * pos / L
    z = jnp.concatenate([t, jnp.cos(ang), -jnp.sin(ang)], -1)
    return z, t


def _pad_cols(a, width):
    return jnp.pad(a, ((0, 0), (0, width - a.shape[1])))


def _pad_rows(a, height):
    return jnp.pad(a, ((0, height - a.shape[0]), (0, 0)))


def kernel(x, c, ctx, c_ctx, w_ada, b_ada, w_in, conv_rw, conv_hy, conv_hy_b, w0_f, w_up_f, a0_f, a_up_f, w0_b, w_up_b, a0_b, a_up_b, k_k, k_a, r_k, g_up, gn_g, gn_b, filt_w1, filt_b1, filt_w2, filt_b2, filt_w3, sin_freq, hy_bias, w_out, ln1_g, ln1_b, ffn_w1, ffn_w3, ffn_w2, ln2_g, ln2_b):
    bsz, L, d = x.shape
    rw = k_k.shape[1]
    hy = hy_bias.shape[1]
    dl, il, gl = w_up_f.shape[1], a_up_f.shape[1], g_up.shape[1]
    lo_w = 4 * LANE
    rkv_cols = 3 * rw

    wi = w_in[0]
    lo0 = rkv_cols

    def lora_layout(a):
        return jnp.concatenate([_pad_cols(a[:, lo0:lo0 + dl], LANE),
                                _pad_cols(a[:, lo0 + dl:lo0 + dl + il], LANE),
                                _pad_cols(a[:, lo0 + dl + il:lo0 + dl + il + gl], 2 * LANE)], axis=1)

    w_lat = _weight_layout(jnp.transpose(wi), rkv_cols=rkv_cols, lora_widths=(dl, il, gl), hy_cols=3 * hy)
    cw_lat = jnp.concatenate([conv_rw[0][:, :rkv_cols], conv_hy[0], lora_layout(conv_rw[0])], axis=1)
    cb_lat = jnp.concatenate([jnp.zeros((rkv_cols,), F32), conv_hy_b[0], jnp.zeros((lo_w,), F32)])[None, :]
    wup = _pad_rows(jnp.concatenate([w_up_f[0], w_up_b[0]], axis=1), LANE)
    aup = _pad_rows(jnp.concatenate([a_up_f[0], a_up_b[0]], axis=1), LANE)
    gup = _pad_rows(g_up[0], 2 * LANE)
    w0 = jnp.concatenate([w0_f[0], w0_b[0]])
    a0 = jnp.concatenate([a0_f[0], a0_b[0]])

    cc = _pad_rows(jnp.concatenate([c, c_ctx[None, :]], axis=0), SUB)
    mod = _adaln(cc, w_ada[0], b_ada[0])
    mod3 = mod.reshape(SUB, 1, 6 * d)

    er, ec = _pos_tables(L // GRID_W, d)

    tn = lo_w
    p_lat = _inproj(x, er, ec, mod3, w_lat, cw_lat, cb_lat, tm=1024, tn=tn, add_pos=True, ctx_row=None)
    ctx_tiles = list(range(rkv_cols // tn)) + [(rkv_cols + 3 * hy) // tn]
    p_ctx = _inproj(ctx, er, ec, mod3, w_lat, cw_lat, cb_lat, tm=ctx.shape[1], tn=tn, add_pos=False,
                    ctx_row=bsz, col_tiles=ctx_tiles)

    lo_cols = (rkv_cols + 3 * hy, rkv_cols + 3 * hy + lo_w)
    r, v, kk, gate, lw, kh, bvec = _rwkv_prep(p_lat, p_ctx, wup, aup, gup, w0, a0,
                                              k_k[0], k_a[0], rw=rw, rkv_cols=rkv_cols, lo_cols=lo_cols)
    o_f, o_b = _rwkv_scan(r, v, kk, lw, kh, bvec, r_k[0], gn_g[0], gn_b[0], n_lat_tokens=L)

    tabs = _fft_tables(L)
    zpos, tcol = _filter_positions(L)
    max_decay = math.log(DECAY_TARGET) / FAST_DECAY_PCT
    min_decay = math.log(DECAY_TARGET) / SLOW_DECAY_PCT
    deltas = jnp.abs(jnp.linspace(min_decay, max_decay, hy, dtype=F32))[None, :]
    kf = _filter_spectrum(_pad_cols(zpos, LANE), tcol, _pad_rows(filt_w1[0], LANE), filt_b1[0], filt_w2[0],
                          filt_b2[0], filt_w3[0], sin_freq[0], deltas, tabs, ct=LANE)
    o_hy = _hyena_conv(p_lat, hy_bias[0], kf, tabs, hy_col0=rkv_cols, hy_dim=hy, ct=LANE)

    x1 = _outproj(o_f, o_b, gate, o_hy, x, er, ec, mod3, w_out[0].astype(BF16), ln1_g[0], ln1_b[0], tm=512)
    return _ffn(x1, mod3, ffn_w1[0].astype(BF16), ffn_w3[0].astype(BF16), ffn_w2[0].astype(BF16),
                ln2_g[0], ln2_b[0], tm=1024, tf=512)
```

```python
import functools
import math

import numpy as np
import jax
import jax.numpy as jnp
from jax import lax
from jax.experimental import pallas as pl
from jax.experimental.pallas import tpu as pltpu

F32 = jnp.float32
BF16 = jnp.bfloat16

GRID_W = 64
HEAD_DIM = 64
N_BANDS = 16
DECAY_TARGET = 1e-2
FAST_DECAY_PCT = 0.3
SLOW_DECAY_PCT = 1.5
LN_EPS = 1e-5
GN_EPS = 64e-5
ALPHA = 2.0 ** 0.25

LANE = 128
SUB = 8
VMEM_LIMIT = 56 * 1024 * 1024

SCAN_CHUNK = 64
SCAN_GROUP = 2
SCAN_BLOCK = 128
FFN_LN_ROWS = 256
FFT_N2 = 64
FFT_UNROLL = 8


def _cparams(sem):
    return pltpu.CompilerParams(dimension_semantics=sem, vmem_limit_bytes=VMEM_LIMIT)


def _dot(a, b, dims=(((1,), (0,)), ((), ()))):
    return lax.dot_general(a.astype(BF16), b.astype(BF16), dims, preferred_element_type=F32)


def _dot_nt(a, b):
    return _dot(a, b, (((1,), (1,)), ((), ())))


def _dot_tn(a, b):
    return _dot(a, b, (((0,), (0,)), ((), ())))


def _split2(a):
    hi = a.astype(BF16)
    lo = (a - hi.astype(F32)).astype(BF16)
    return hi, lo


def _dot3(a, b, b_split=None):
    ah, al = _split2(a)
    bh, bl = _split2(b) if b_split is None else b_split
    return _dot(ah, bh) + (_dot(al, bh) + _dot(ah, bl))


def _dot_exact_lhs(a_exact, b):
    b1, b2 = _split2(b)
    return _dot(a_exact, b1) + _dot(a_exact, b2)


def _sigmoid(x):
    return 0.5 * jnp.tanh(0.5 * x) + 0.5


def _layer_norm(h, g, b):
    mu = jnp.mean(h, -1, keepdims=True)
    hc = h - mu
    var = jnp.mean(hc * hc, -1, keepdims=True)
    return hc * lax.rsqrt(var + LN_EPS) * g + b


def _adaln_kernel(c_ref, w_ref, b_ref, o_ref):
    cv = c_ref[...]
    a = cv * _sigmoid(cv)
    o_ref[...] = _dot3(a, w_ref[...]) + b_ref[...]


def _adaln(cc, w, b, tn=1024):
    m, d = cc.shape
    n = w.shape[1]
    return pl.pallas_call(
        _adaln_kernel,
        grid=(n // tn,),
        in_specs=[pl.BlockSpec((m, d), lambda j: (0, 0)),
                  pl.BlockSpec((d, tn), lambda j: (0, j)),
                  pl.BlockSpec((1, tn), lambda j: (0, j))],
        out_specs=pl.BlockSpec((m, tn), lambda j: (0, j)),
        out_shape=jax.ShapeDtypeStruct((m, n), F32),
        compiler_params=_cparams(("arbitrary",)),
        name="adaln",
    )(cc, w, b.reshape(1, n))


def _wlayout_kernel(wt_ref, o_ref, *, rkv_cols, lora_widths, hy_cols):
    tk = wt_ref.shape[1]

    def put(dst, rows):
        o_ref[:, dst:dst + rows.shape[0]] = jnp.transpose(rows).astype(o_ref.dtype)

    put(0, wt_ref[0:rkv_cols, :])
    src = rkv_cols
    pieces = []
    for width in lora_widths:
        padded = -(-width // LANE) * LANE
        pieces.append(wt_ref[src:src + width, :])
        if padded > width:
            pieces.append(jnp.zeros((padded - width, tk), F32))
        src += width
    put(rkv_cols, wt_ref[src:src + hy_cols, :])
    put(rkv_cols + hy_cols, jnp.concatenate(pieces, axis=0))


def _weight_layout(wt, *, rkv_cols, lora_widths, hy_cols, tk=256):
    n_src, k = wt.shape
    n_dst = rkv_cols + hy_cols + sum(-(-wd // LANE) * LANE for wd in lora_widths)
    return pl.pallas_call(
        functools.partial(_wlayout_kernel, rkv_cols=rkv_cols, lora_widths=lora_widths, hy_cols=hy_cols),
        grid=(k // tk,),
        in_specs=[pl.BlockSpec((n_src, tk), lambda i: (0, i))],
        out_specs=pl.BlockSpec((tk, n_dst), lambda i: (i, 0)),
        out_shape=jax.ShapeDtypeStruct((k, n_dst), BF16),
        compiler_params=_cparams(("arbitrary",)),
        name="weight_layout",
    )(wt)


HALO = 16


def _add_pos(xv, er_row, ec_rows):
    half = ec_rows.shape[1]
    return jnp.concatenate([xv[:, :half] + er_row, xv[:, half:] + ec_rows], axis=1)


def _inproj_kernel(x_ref, xp_ref, xn_ref, er_ref, ec_ref, sh_ref, sc_ref, w_ref, cw_ref, cb_ref, o_ref, a_scr,
                   *, add_pos):
    i = pl.program_id(1)
    n_i = pl.num_programs(1)
    tm = x_ref.shape[1]
    gr = tm // GRID_W

    @pl.when(pl.program_id(2) == 0)
    def _():
        scale = 1.0 + sc_ref[0]
        shift = sh_ref[0]

        def put(rows, xv, keep=None):
            a = xv * scale + shift
            if keep is not None:
                a = jnp.where(keep, a, 0.0)
            a_scr[rows, :] = a.astype(BF16)

        has_prev = i > 0
        has_next = i < n_i - 1
        lo_rows, hi_rows = slice(0, HALO), slice(HALO + tm, 2 * HALO + tm)
        if add_pos:
            n_er = er_ref.shape[0]
            for q in range(gr):
                rows = slice(q * GRID_W, (q + 1) * GRID_W)
                put(slice(HALO + q * GRID_W, HALO + (q + 1) * GRID_W),
                    _add_pos(x_ref[0, rows, :], er_ref[pl.ds(i * gr + q, 1), :], ec_ref[...]))
            put(lo_rows, _add_pos(xp_ref[0], er_ref[pl.ds(jnp.maximum(i * gr - 1, 0), 1), :],
                                  ec_ref[GRID_W - HALO:, :]), has_prev)
            put(hi_rows, _add_pos(xn_ref[0], er_ref[pl.ds(jnp.minimum((i + 1) * gr, n_er - 1), 1), :],
                                  ec_ref[:HALO, :]), has_next)
        else:
            put(lo_rows, xp_ref[0], has_prev)
            put(slice(HALO, HALO + tm), x_ref[0])
            put(hi_rows, xn_ref[0], has_next)

    n_rows = tm + 2 * HALO
    half = w_ref.shape[1] // 2
    for h in range(2):
        cols = slice(h * half, (h + 1) * half)
        res = jnp.dot(a_scr[...], w_ref[:, cols], preferred_element_type=F32)
        out = (pltpu.roll(res, 1, 0) * cw_ref[0:1, cols] + res * cw_ref[1:2, cols]
               + pltpu.roll(res, n_rows - 1, 0) * cw_ref[2:3, cols] + cb_ref[:, cols])
        o_ref[0, :, cols] = out[HALO:HALO + tm].astype(o_ref.dtype)


def _inproj(x, er, ec, mod3, w, cw, cb, *, tm, tn, add_pos, ctx_row, col_tiles=None):
    bsz, lx, d = x.shape
    if col_tiles is None:
        n = w.shape[1]
        col = lambda j: j
    else:
        n = len(col_tiles) * tn
        run = col_tiles[:-1]
        assert run == list(range(len(run)))
        col = lambda j: jnp.where(j < len(run), j, col_tiles[-1])
    hb = tm // HALO
    n_hb = lx // HALO
    row = (lambda b: ctx_row) if ctx_row is not None else (lambda b: b)
    return pl.pallas_call(
        functools.partial(_inproj_kernel, add_pos=add_pos),
        grid=(bsz, lx // tm, n // tn),
        in_specs=[pl.BlockSpec((1, tm, d), lambda b, i, j: (b, i, 0)),
                  pl.BlockSpec((1, HALO, d), lambda b, i, j: (b, jnp.maximum(i * hb - 1, 0), 0)),
                  pl.BlockSpec((1, HALO, d), lambda b, i, j: (b, jnp.minimum((i + 1) * hb, n_hb - 1), 0)),
                  pl.BlockSpec(er.shape, lambda b, i, j: (0, 0)),
                  pl.BlockSpec(ec.shape, lambda b, i, j: (0, 0)),
                  pl.BlockSpec((1, 1, d), lambda b, i, j: (row(b), 0, 0)),
                  pl.BlockSpec((1, 1, d), lambda b, i, j: (row(b), 0, 1)),
                  pl.BlockSpec((d, tn), lambda b, i, j: (0, col(j))),
                  pl.BlockSpec((3, tn), lambda b, i, j: (0, col(j))),
                  pl.BlockSpec((1, tn), lambda b, i, j: (0, col(j)))],
        out_specs=pl.BlockSpec((1, tm, tn), lambda b, i, j: (b, i, j)),
        out_shape=jax.ShapeDtypeStruct((bsz, lx, n), BF16),
        scratch_shapes=[pltpu.VMEM((tm + 2 * HALO, d), BF16)],
        compiler_params=_cparams(("arbitrary", "arbitrary", "arbitrary")),
        name="inproj_pos" if add_pos else "inproj_ctx",
    )(x, x, x, er, ec, mod3, mod3, w, cw, cb)


def _head_sum(x, ones_bd):
    rows = x.shape[0]
    outs = []
    for j in range(x.shape[1] // LANE):
        hi, lo = _split2(x[:, j * LANE:(j + 1) * LANE])
        s = _dot(jnp.concatenate([hi, lo], axis=0), ones_bd)
        outs.append(s[:rows] + s[rows:])
    return jnp.concatenate(outs, axis=1)


def _prep_kernel(p_rkv, p_lo, pc_rkv, pc_lo, wup, aup, gup, w0, a0, kk_w, ka_w, ones_ref,
                 r_out, v_out, kk_out, gate_out, lw_out, kh_out, b_out, *, n_lat):
    is_ctx = pl.program_id(1) == n_lat
    rw = kk_out.shape[2]
    ones_bd = ones_ref[...]
    wup_split = _split2(wup[...])
    aup_split = _split2(aup[...])
    gup_bf = gup[...].astype(BF16)

    def row_block(rows):
        def rkv(s):
            cols = slice(s * rw, (s + 1) * rw)
            return jnp.where(is_ctx, pc_rkv[0, rows, cols], p_rkv[0, rows, cols])

        r_out[0, rows, :] = rkv(0)
        v_out[0, rows, :] = rkv(2)
        lo = jnp.where(is_ctx, pc_lo[0, rows, :], p_lo[0, rows, :]).astype(F32)
        tw_split = _split2(jnp.tanh(lo[:, 0:LANE]))
        al_split = _split2(lo[:, LANE:2 * LANE])
        gl = _sigmoid(lo[:, 2 * LANE:]).astype(BF16)

        def up(a_split, b_split, cols):
            (ah, al_), (bh, bl) = a_split, b_split
            return _dot(ah, bh[:, cols]) + (_dot(al_, bh[:, cols]) + _dot(ah, bl[:, cols]))

        for c in range(rw // LANE):
            cs = slice(c * LANE, (c + 1) * LANE)
            src = slice(rw + c * LANE, rw + (c + 1) * LANE)
            k = jnp.where(is_ctx, pc_rkv[0, rows, src], p_rkv[0, rows, src]).astype(F32)
            kkr = k * kk_w[:, cs]
            nrm = jnp.sqrt(_head_sum(kkr * kkr, ones_bd))
            kk = kkr / jnp.maximum(nrm, 1e-12)
            kk_out[0, rows, cs] = kk.astype(kk_out.dtype)
            gate_out[0, rows, cs] = _dot(gl, gup_bf[:, cs]).astype(gate_out.dtype)
            for d in range(2):
                ws = slice(d * rw + c * LANE, d * rw + (c + 1) * LANE)
                wx = up(tw_split, wup_split, ws) + w0[:, ws]
                ax = up(al_split, aup_split, ws) + a0[:, ws]
                lw_out[d, 0, rows, cs] = -math.exp(-0.5) * _sigmoid(wx)
                ia = _sigmoid(ax)
                kh_out[d, 0, rows, cs] = (k * (1.0 + (ia - 1.0) * ka_w[:, cs])).astype(kh_out.dtype)
                b_out[d, 0, rows, cs] = (kk * ia).astype(b_out.dtype)

    row_block(slice(0, kk_out.shape[1]))


def _rwkv_prep(p, pc, wup, aup, gup, w0, a0, k_k, k_a, *, rw, rkv_cols, lo_cols):
    bsz, lq, _ = p.shape
    ctx = pc.shape[1]
    tr = ctx
    n_lat = lq // tr
    t_all = lq + ctx
    lo_w = lo_cols[1] - lo_cols[0]
    lo_blk = lo_cols[0] // lo_w
    lo_blk_c = rkv_cols // lo_w
    lat = lambda i: jnp.minimum(i, n_lat - 1)
    ones_bd = (np.arange(LANE)[:, None] // HEAD_DIM == np.arange(LANE)[None, :] // HEAD_DIM)
    ones_bd = jnp.asarray(ones_bd, BF16)
    full = lambda a: pl.BlockSpec(a.shape, lambda b, i: (0,) * a.ndim)
    row = lambda a: a.reshape(1, -1)
    weights = [wup, aup, gup, row(w0), row(a0), row(k_k), row(k_a), ones_bd]
    shared = jax.ShapeDtypeStruct((bsz, t_all, rw), BF16)
    perdir = jax.ShapeDtypeStruct((2, bsz, t_all, rw), BF16)
    perdir_f32 = jax.ShapeDtypeStruct((2, bsz, t_all, rw), F32)
    o_shared = pl.BlockSpec((1, tr, rw), lambda b, i: (b, i, 0))
    o_dir = pl.BlockSpec((2, 1, tr, rw), lambda b, i: (0, b, i, 0))
    return pl.pallas_call(
        functools.partial(_prep_kernel, n_lat=n_lat),
        grid=(bsz, n_lat + 1),
        in_specs=[pl.BlockSpec((1, tr, rkv_cols), lambda b, i: (b, lat(i), 0)),
                  pl.BlockSpec((1, tr, lo_w), lambda b, i: (b, lat(i), lo_blk)),
                  pl.BlockSpec((1, tr, rkv_cols), lambda b, i: (b, 0, 0)),
                  pl.BlockSpec((1, tr, lo_w), lambda b, i: (b, 0, lo_blk_c))]
                 + [full(a) for a in weights],
        out_specs=[o_shared, o_shared, o_shared, o_shared, o_dir, o_dir, o_dir],
        out_shape=[shared, shared, shared, shared, perdir_f32, perdir, perdir],
        compiler_params=_cparams(("arbitrary", "arbitrary")),
        name="rwkv_prep",
    )(p, p, pc, pc, *weights)


def _scan_kernel(rf_ref, vf_ref, kkf_ref, lwf_ref, khf_ref, bf_ref, rb_ref, vb_ref, kkb_ref, lwb_ref, khb_ref,
                 bb_ref, rk_ref, gng_ref, gnb_ref, ones_ref, of_ref, ob_ref, s_scr):
    C = SCAN_CHUNK
    n_sub = rf_ref.shape[1] // C
    G = SCAN_GROUP
    W = G * HEAD_DIM
    n = G * C
    n_groups = rf_ref.shape[2] // W

    @pl.when(pl.program_id(1) == 0)
    def _():
        s_scr[...] = jnp.zeros_like(s_scr)

    bi = lax.broadcasted_iota(jnp.int32, (n, n), 0)
    bj = lax.broadcasted_iota(jnp.int32, (n, n), 1)
    same = (bi // C) == (bj // C)
    li = bi % C
    lj = bj % C
    eye = jnp.where(bi == bj, 1.0, 0.0)
    hrow = lax.broadcasted_iota(jnp.int32, (n, W), 0) // C
    hlane = lax.broadcasted_iota(jnp.int32, (n, W), 1) // HEAD_DIM
    hmask = hrow == hlane
    ti = lax.broadcasted_iota(jnp.int32, (C, C), 0)
    tj = lax.broadcasted_iota(jnp.int32, (C, C), 1)

    def rep(x):
        return jnp.where(hmask, jnp.concatenate([x] * G, axis=0), 0.0).astype(BF16)

    chains = []
    dirs = ((rf_ref, vf_ref, kkf_ref, lwf_ref, khf_ref, bf_ref, False),
            (rb_ref, vb_ref, kkb_ref, lwb_ref, khb_ref, bb_ref, True))
    per_chunk = {}
    for di, (r_ref, v_ref, kk_ref, lw_ref, kh_ref, b_ref, rev) in enumerate(dirs):
        before = (lj > li) if rev else (lj < li)
        strict = jnp.logical_and(same, before)
        incl = jnp.logical_and(same, jnp.logical_or(before, li == lj))
        levels = []
        s = 1
        while s < C:
            later, earlier = (lj, li) if rev else (li, lj)
            off = jnp.logical_and(jnp.logical_and((later // s) % 2 == 1, (earlier // s) % 2 == 0),
                                  (li // (2 * s)) == (lj // (2 * s)))
            levels.append(jnp.logical_and(same, off))
            s *= 2
        tri = jnp.where((tj >= ti) if rev else (tj <= ti), 1.0, 0.0).astype(BF16)
        for t in range(n_sub):
            sub = n_sub - 1 - t if rev else t
            rows = slice(sub * C, (sub + 1) * C)
            r = r_ref[0, rows, :].astype(F32)
            v = v_ref[0, rows, :].astype(F32)
            kk = kk_ref[0, rows, :].astype(F32)
            lw = lw_ref[0, 0, rows, :]
            kh = kh_ref[0, 0, rows, :].astype(F32)
            bv = b_ref[0, 0, rows, :].astype(F32)
            cum = _dot_exact_lhs(tri, lw)
            tot = cum[0:1, :] if rev else cum[C - 1:C, :]
            e_neg = jnp.exp(-cum)
            e_end = jnp.exp(tot - cum)
            rt = r * jnp.exp(cum)
            at = -kk * jnp.exp(cum - lw)
            bt = bv * e_neg
            kt = kh * e_neg
            bp = bv * e_end
            kp = kh * e_end
            per_chunk[(di, t)] = (r, v, kh, jnp.exp(tot), rows)
            for g in range(n_groups):
                cols = slice(g * W, (g + 1) * W)
                chains.append(dict(
                    di=di, t=t, g=g, cols=cols, strict=strict, incl=incl, levels=levels,
                    v2=rep(v[:, cols]),
                    ar=jnp.concatenate([rep(at[:, cols]), rep(rt[:, cols])], axis=0),
                    bk=jnp.concatenate([rep(bt[:, cols]), rep(kt[:, cols])], axis=0),
                    bkp=jnp.concatenate([rep(bp[:, cols]), rep(kp[:, cols])], axis=0)))

    for ch in chains:
        ch["tt"] = _dot_nt(ch["ar"], ch["bk"])
    for ch in chains:
        ch["p"] = jnp.where(ch["strict"], ch["tt"][:n, :n], 0.0)
        ch["tinv"] = eye + jnp.where(ch["levels"][0], ch["p"], 0.0)
    for lvl in range(1, len(chains[0]["levels"])):
        for ch in chains:
            ch["e"] = _dot(jnp.where(ch["levels"][lvl], ch["p"], 0.0), ch["tinv"])
        for ch in chains:
            ch["tinv"] = ch["tinv"] + _dot(ch["tinv"], ch["e"])
    state = {(di, g): s_scr[di, g] for di in range(2) for g in range(n_groups)}
    for t in range(n_sub):
        cur = [ch for ch in chains if ch["t"] == t]
        for ch in cur:
            ch["s0"] = state[(ch["di"], ch["g"])]
            ch["xs"] = _dot_nt(ch["ar"], ch["s0"])
        for ch in cur:
            a_ak = jnp.where(ch["strict"], ch["tt"][:n, n:], 0.0)
            ch["rhs"] = ch["xs"][:n] + _dot(a_ak, ch["v2"])
        for ch in cur:
            u = _dot(ch["tinv"], ch["rhs"])
            ch["uv"] = jnp.concatenate([u.astype(BF16), ch["v2"]], axis=0)
        for ch in cur:
            e_tot = per_chunk[(ch["di"], t)][3]
            state[(ch["di"], ch["g"])] = ch["s0"] * e_tot[:, ch["cols"]] + _dot_tn(ch["uv"], ch["bkp"])
        for ch in cur:
            a_r = jnp.concatenate([jnp.where(ch["incl"], ch["tt"][n:, :n], 0.0),
                                   jnp.where(ch["incl"], ch["tt"][n:, n:], 0.0)], axis=1)
            ch["y2"] = ch["xs"][n:] + _dot(a_r, ch["uv"])
    for (di, g), s_new in state.items():
        s_scr[di, g] = s_new

    ones_bd = ones_ref[...]
    inv = 1.0 / HEAD_DIM
    for di, o_ref in enumerate((of_ref, ob_ref)):
        for t in range(n_sub):
            r, v, kh, _, rows = per_chunk[(di, t)]
            ys = []
            for ch in chains:
                if ch["di"] == di and ch["t"] == t:
                    y = ch["y2"][0:C]
                    for h in range(1, G):
                        y = y + ch["y2"][h * C:(h + 1) * C]
                    ys.append(y)
            y = jnp.concatenate(ys, axis=1)
            mu = _head_sum(y, ones_bd) * inv
            yc = y - mu
            var = _head_sum(yc * yc, ones_bd) * inv
            gn = yc * lax.rsqrt(var + GN_EPS) * gng_ref[...] + gnb_ref[...]
            bonus = _head_sum(r * kh * rk_ref[...], ones_bd) * v
            o_ref[0, rows, :] = gn + bonus


def _rwkv_scan(r, v, kk, lw, kh, bvec, r_k, gn_g, gn_b, *, n_lat_tokens):
    bsz, t_all, rw = r.shape
    C = SCAN_BLOCK
    n_chunks = t_all // C
    n_ctx = (t_all - n_lat_tokens) // C
    n_lat = n_lat_tokens // C
    W = SCAN_GROUP * HEAD_DIM

    blk_f = lambda c: jnp.where(c < n_ctx, n_lat + c, c - n_ctx)
    blk_b = lambda c: n_chunks - 1 - c
    ones_bd = (np.arange(LANE)[:, None] // HEAD_DIM == np.arange(LANE)[None, :] // HEAD_DIM)
    ones_bd = jnp.asarray(ones_bd, BF16)
    shared = lambda blk: pl.BlockSpec((1, C, rw), lambda b, c: (b, blk(c), 0))
    perdir = lambda d, blk: pl.BlockSpec((1, 1, C, rw), lambda b, c: (d, b, blk(c), 0))
    vec = pl.BlockSpec((1, rw), lambda b, c: (0, 0))
    out = jax.ShapeDtypeStruct((bsz, t_all, rw), F32)
    return pl.pallas_call(
        _scan_kernel,
        grid=(bsz, n_chunks),
        in_specs=[shared(blk_f), shared(blk_f), shared(blk_f), perdir(0, blk_f), perdir(0, blk_f), perdir(0, blk_f),
                  shared(blk_b), shared(blk_b), shared(blk_b), perdir(1, blk_b), perdir(1, blk_b), perdir(1, blk_b),
                  vec, vec, vec, pl.BlockSpec((LANE, LANE), lambda b, c: (0, 0))],
        out_specs=[shared(blk_f), shared(blk_b)],
        out_shape=[out, out],
        scratch_shapes=[pltpu.VMEM((2, rw // W, W, W), F32)],
        compiler_params=_cparams(("arbitrary", "arbitrary")),
        name="rwkv_scan",
    )(r, v, kk, lw, kh, bvec, r, v, kk, lw, kh, bvec,
      r_k.reshape(1, rw), gn_g.reshape(1, rw), gn_b.reshape(1, rw), ones_bd)


def _fft_tables(L):
    n2 = FFT_N2
    n_all = 2 * L
    n1 = n_all // n2
    nh = n1 // 2
    two_pi = 2.0 * np.pi
    kh = -(-(nh + 1) // FFT_UNROLL) * FFT_UNROLL
    k1 = np.arange(kh)
    pair_w = np.where(k1 > nh, 0.0, np.where((k1 == 0) | (k1 == nh), 1.0, 2.0))
    th1 = two_pi * np.outer(k1, np.arange(nh)) / n1
    f1 = np.concatenate([np.cos(th1), -np.sin(th1)], 0)
    th2 = two_pi * np.outer(np.arange(n2), np.arange(n2)) / n2
    fr, fi = np.cos(th2), -np.sin(th2)
    a3 = np.block([[fr, -fi], [fi, fr]])
    b3 = np.block([[-fi, -fr], [fr, -fi]])
    tht = two_pi * np.outer(k1, np.arange(n2)) / n_all
    twr = np.tile(np.cos(tht), (1, 2))
    twi = np.tile(-np.sin(tht), (1, 2))
    gr, gi = np.cos(th2), np.sin(th2)
    m3i = np.block([[gr, -gi], [gi, gr]]) / n_all
    th4 = two_pi * np.outer(np.arange(nh), k1) / n1
    cos4, sin4 = np.cos(th4) * pair_w, np.sin(th4) * pair_w
    ca = np.concatenate([cos4, -sin4], 1)
    cb = np.concatenate([-sin4, -cos4], 1)
    th5 = two_pi * np.outer(np.arange(n2), k1) / n_all
    t2r, t2i = np.tile(np.cos(th5), (1, 2)), np.tile(np.sin(th5), (1, 2))
    f = lambda a: jnp.asarray(a, F32)
    return dict(f1=f(f1), a3=f(a3), b3=f(b3), twr=f(twr), twi=f(twi),
                m3i=f(m3i), ca=f(ca), cb=f(cb), t2r=f(t2r), t2i=f(t2i))


def _fft_stage1(src_ref, f1_ref, s_ref):
    kh2, nh = f1_ref.shape
    n2 = FFT_N2
    f1 = f1_ref[...].astype(BF16)

    def body(nlo, carry):
        xin = src_ref[pl.ds(nlo, nh, stride=n2), :]
        s_ref[pl.ds(pl.multiple_of(nlo * kh2, SUB), kh2), :] = _dot(f1, xin)
        return carry

    lax.fori_loop(0, n2, body, 0, unroll=FFT_UNROLL)


def _fft_mid_matrix(k1, a3_ref, b3_ref, twr_ref, twi_ref):
    tr = twr_ref[pl.ds(k1, 1), :]
    ti = twi_ref[pl.ds(k1, 1), :]
    return (a3_ref[...] * tr + b3_ref[...] * ti).astype(BF16)


def _fft_load_k1(s_ref, k1, kh2):
    n2 = FFT_N2
    re = s_ref[pl.ds(k1, n2, stride=kh2), :]
    im = s_ref[pl.ds(kh2 // 2 + k1, n2, stride=kh2), :]
    return jnp.concatenate([re, im], axis=0)


def _hyena_kernel(x0_ref, x1_ref, v_ref, bias_ref, kf_ref,
                  f1_ref, a3_ref, b3_ref, twr_ref, twi_ref, m3i_ref, ca_ref, cb_ref, t2r_ref, t2i_ref,
                  o_ref, z_scr, y_scr, s_scr, q_scr):
    n2 = FFT_N2
    kh2, nh = f1_ref.shape
    kh = kh2 // 2
    z_scr[...] = v_ref[0].astype(F32) * x1_ref[0].astype(F32)
    _fft_stage1(z_scr, f1_ref, s_scr)
    m3i = m3i_ref[...].astype(BF16)

    def mid(k1, carry):
        m = _fft_mid_matrix(k1, a3_ref, b3_ref, twr_ref, twi_ref)
        cc = _dot(m, _fft_load_k1(s_scr, k1, kh2))
        kf = kf_ref[0, k1]
        cr, ci = cc[:n2], cc[n2:]
        kr, ki = kf[:n2], kf[n2:]
        pp = jnp.concatenate([cr * kr - ci * ki, cr * ki + ci * kr], axis=0)
        q = _dot(m3i, pp)
        q_scr[pl.ds(k1, n2, stride=kh2), :] = q[:n2]
        q_scr[pl.ds(kh + k1, n2, stride=kh2), :] = q[n2:]
        return carry

    lax.fori_loop(0, kh, mid, 0, unroll=FFT_UNROLL)

    def last(m2, carry):
        f4 = ca_ref[...] * t2r_ref[pl.ds(m2, 1), :] + cb_ref[...] * t2i_ref[pl.ds(m2, 1), :]
        qm = q_scr[pl.ds(pl.multiple_of(m2 * kh2, SUB), kh2), :]
        y_scr[pl.ds(m2, nh, stride=n2), :] = _dot(f4, qm)
        return carry

    lax.fori_loop(0, n2, last, 0, unroll=FFT_UNROLL)
    z = z_scr[...]
    o_ref[0] = (y_scr[...] + z * bias_ref[...]) * x0_ref[0].astype(F32)


def _hyena_conv(p, hy_bias, kf, tabs, *, hy_col0, hy_dim, ct):
    bsz, L, _ = p.shape
    n_tiles = hy_dim // ct
    blk0 = hy_col0 // ct
    sec = hy_dim // ct
    xspec = lambda s: pl.BlockSpec((1, L, ct), lambda t, b: (b, 0, blk0 + s * sec + t))
    tab_list = [tabs[k] for k in ("f1", "a3", "b3", "twr", "twi", "m3i", "ca", "cb", "t2r", "t2i")]
    full = lambda a: pl.BlockSpec(a.shape, lambda t, b: (0,) * a.ndim)
    kh2 = tabs["f1"].shape[0]
    return pl.pallas_call(
        _hyena_kernel,
        grid=(n_tiles, bsz),
        in_specs=[xspec(0), xspec(1), xspec(2),
                  pl.BlockSpec((1, ct), lambda t, b: (0, t)),
                  pl.BlockSpec((1,) + kf.shape[1:], lambda t, b: (t, 0, 0, 0), pipeline_mode=pl.Buffered(1))]
                 + [full(a) for a in tab_list],
        out_specs=pl.BlockSpec((1, L, ct), lambda t, b: (b, 0, t)),
        out_shape=jax.ShapeDtypeStruct((bsz, L, hy_dim), F32),
        scratch_shapes=[pltpu.VMEM((L, ct), F32), pltpu.VMEM((L, ct), F32),
                        pltpu.VMEM((FFT_N2 * kh2, ct), F32), pltpu.VMEM((FFT_N2 * kh2, ct), F32)],
        compiler_params=_cparams(("arbitrary", "arbitrary")),
        name="hyena_conv",
    )(p, p, p, hy_bias.reshape(1, -1), kf, *tab_list)


def _filter_kernel(z_ref, t_ref, w1_ref, b1_ref, w2_ref, b2_ref, w3f_ref, w3b_ref, freq_ref, delta_ref,
                   f1_ref, a3_ref, b3_ref, twr_ref, twi_ref, kf_ref, h_scr, hf_scr, hb_scr, sf_scr, sb_scr):
    n2 = FFT_N2
    kh2 = f1_ref.shape[0]

    @pl.when(pl.program_id(0) == 0)
    def _():
        freq = freq_ref[...]
        h1 = jnp.sin(freq * (_dot3(z_ref[...], w1_ref[...]) + b1_ref[...]))
        h_scr[...] = jnp.sin(freq * (_dot3(h1, w2_ref[...]) + b2_ref[...]))

    h = h_scr[...]
    window = jnp.exp(-t_ref[...] * delta_ref[...])
    hf_scr[...] = _dot3(h, w3f_ref[...]) * window
    hb = _dot3(h, w3b_ref[...]) * window
    rows = lax.broadcasted_iota(jnp.int32, hb.shape, 0)
    hb_scr[...] = jnp.where(rows == 0, 0.0, hb)
    _fft_stage1(hf_scr, f1_ref, sf_scr)
    _fft_stage1(hb_scr, f1_ref, sb_scr)

    def mid(k1, carry):
        m = _fft_mid_matrix(k1, a3_ref, b3_ref, twr_ref, twi_ref)
        cf = _dot(m, _fft_load_k1(sf_scr, k1, kh2))
        cb = _dot(m, _fft_load_k1(sb_scr, k1, kh2))
        kf_ref[0, k1] = jnp.concatenate([cf[:n2] + cb[:n2], cf[n2:] - cb[n2:]], axis=0)
        return carry

    lax.fori_loop(0, kh2 // 2, mid, 0, unroll=FFT_UNROLL)


def _filter_spectrum(zpos, tcol, w1, b1, w2, b2, w3, freq, deltas, tabs, *, ct):
    L = zpos.shape[0]
    hy_dim = deltas.shape[1]
    n_tiles = hy_dim // ct
    kh2 = tabs["f1"].shape[0]
    kh = kh2 // 2
    tab_list = [tabs[k] for k in ("f1", "a3", "b3", "twr", "twi")]
    full = lambda a: pl.BlockSpec(a.shape, lambda t: (0,) * a.ndim)
    row = lambda a: a.reshape(1, -1)
    small = [zpos, tcol, w1, row(b1), w2, row(b2)]
    return pl.pallas_call(
        _filter_kernel,
        grid=(n_tiles,),
        in_specs=[full(a) for a in small]
                 + [pl.BlockSpec((w3.shape[0], ct), lambda t: (0, t)),
                    pl.BlockSpec((w3.shape[0], ct), lambda t: (0, n_tiles + t)),
                    full(row(freq)),
                    pl.BlockSpec((1, ct), lambda t: (0, t))]
                 + [full(a) for a in tab_list],
        out_specs=pl.BlockSpec((1, kh, 2 * FFT_N2, ct), lambda t: (t, 0, 0, 0)),
        out_shape=jax.ShapeDtypeStruct((n_tiles, kh, 2 * FFT_N2, ct), F32),
        scratch_shapes=[pltpu.VMEM((L, w2.shape[1]), F32), pltpu.VMEM((L, ct), F32), pltpu.VMEM((L, ct), F32),
                        pltpu.VMEM((FFT_N2 * kh2, ct), F32), pltpu.VMEM((FFT_N2 * kh2, ct), F32)],
        compiler_params=_cparams(("arbitrary",)),
        name="hyena_filter",
    )(*small, w3, w3, row(freq), deltas, *tab_list)


def _outproj_kernel(of_ref, ob_ref, gate_ref, ohy_ref, x_ref, er_ref, ec_ref, g1_ref, w_ref, lng_ref, lnb_ref,
                    o_ref):
    rw = of_ref.shape[2]
    a_rw = ((of_ref[0] + ob_ref[0]) * gate_ref[0].astype(F32)).astype(BF16)
    mix = jnp.dot(a_rw, w_ref[0:rw, :], preferred_element_type=F32)
    mix = mix + jnp.dot(ohy_ref[0].astype(BF16), w_ref[rw:, :], preferred_element_type=F32)
    gmix = g1_ref[0] * mix
    tm = x_ref.shape[1]
    for q in range(tm // GRID_W):
        rows = slice(q * GRID_W, (q + 1) * GRID_W)
        xv = _add_pos(x_ref[0, rows, :], er_ref[q:q + 1, :], ec_ref[...])
        o_ref[0, rows, :] = _layer_norm(ALPHA * xv + gmix[rows], lng_ref[...], lnb_ref[...]).astype(o_ref.dtype)


def _outproj(o_f, o_b, gate, o_hy, x, er, ec, mod3, w_out, ln_g, ln_b, *, tm):
    bsz, L, d = x.shape
    rw = gate.shape[2]
    hy = o_hy.shape[2]
    gr = tm // GRID_W
    return pl.pallas_call(
        _outproj_kernel,
        grid=(bsz, L // tm),
        in_specs=[pl.BlockSpec((1, tm, rw), lambda b, i: (b, i, 0)),
                  pl.BlockSpec((1, tm, rw), lambda b, i: (b, i, 0)),
                  pl.BlockSpec((1, tm, rw), lambda b, i: (b, i, 0)),
                  pl.BlockSpec((1, tm, hy), lambda b, i: (b, i, 0)),
                  pl.BlockSpec((1, tm, d), lambda b, i: (b, i, 0)),
                  pl.BlockSpec((gr, d // 2), lambda b, i: (i, 0)),
                  pl.BlockSpec((GRID_W, d // 2), lambda b, i: (0, 0)),
                  pl.BlockSpec((1, 1, d), lambda b, i: (b, 0, 2)),
                  pl.BlockSpec((d, d), lambda b, i: (0, 0)),
                  pl.BlockSpec((1, d), lambda b, i: (0, 0)),
                  pl.BlockSpec((1, d), lambda b, i: (0, 0))],
        out_specs=pl.BlockSpec((1, tm, d), lambda b, i: (b, i, 0)),
        out_shape=jax.ShapeDtypeStruct((bsz, L, d), BF16),
        compiler_params=_cparams(("arbitrary", "arbitrary")),
        name="outproj_ln",
    )(o_f, o_b, gate, o_hy, x, er, ec, mod3, w_out, ln_g.reshape(1, d), ln_b.reshape(1, d))


def _ffn_kernel(x_ref, sh_ref, sc_ref, g2_ref, w1_ref, w3_ref, w2_ref, lng_ref, lnb_ref, o_ref, a_scr):
    f = pl.program_id(2)
    tm = x_ref.shape[1]

    @pl.when(f == 0)
    def _():
        a_scr[...] = (x_ref[0].astype(F32) * (1.0 + sc_ref[0]) + sh_ref[0]).astype(BF16)
        o_ref[0] = jnp.zeros((tm, o_ref.shape[2]), F32)

    a = a_scr[...]
    h1 = jnp.dot(a, w1_ref[...], preferred_element_type=F32)
    h3 = jnp.dot(a, w3_ref[...], preferred_element_type=F32)
    h = (h1 * _sigmoid(h1) * h3).astype(BF16)
    o_ref[0] += jnp.dot(h, w2_ref[...], preferred_element_type=F32)

    @pl.when(f == pl.num_programs(2) - 1)
    def _():
        for r0 in range(0, tm, FFN_LN_ROWS):
            rows = slice(r0, r0 + FFN_LN_ROWS)
            res = ALPHA * x_ref[0, rows, :].astype(F32) + g2_ref[0] * o_ref[0, rows, :]
            o_ref[0, rows, :] = _layer_norm(res, lng_ref[...], lnb_ref[...])


def _ffn(x1, mod3, w1, w3, w2, ln_g, ln_b, *, tm, tf):
    bsz, L, d = x1.shape
    dff = w1.shape[1]
    return pl.pallas_call(
        _ffn_kernel,
        grid=(bsz, L // tm, dff // tf),
        in_specs=[pl.BlockSpec((1, tm, d), lambda b, i, f: (b, i, 0)),
                  pl.BlockSpec((1, 1, d), lambda b, i, f: (b, 0, 3)),
                  pl.BlockSpec((1, 1, d), lambda b, i, f: (b, 0, 4)),
                  pl.BlockSpec((1, 1, d), lambda b, i, f: (b, 0, 5)),
                  pl.BlockSpec((d, tf), lambda b, i, f: (0, f)),
                  pl.BlockSpec((d, tf), lambda b, i, f: (0, f)),
                  pl.BlockSpec((tf, d), lambda b, i, f: (f, 0)),
                  pl.BlockSpec((1, d), lambda b, i, f: (0, 0)),
                  pl.BlockSpec((1, d), lambda b, i, f: (0, 0))],
        out_specs=pl.BlockSpec((1, tm, d), lambda b, i, f: (b, i, 0)),
        out_shape=jax.ShapeDtypeStruct((bsz, L, d), F32),
        scratch_shapes=[pltpu.VMEM((tm, d), BF16)],
        compiler_params=_cparams(("arbitrary", "arbitrary", "arbitrary")),
        name="ffn_ln",
    )(x1, mod3, mod3, mod3, w1, w3, w2, ln_g.reshape(1, d), ln_b.reshape(1, d))


def _pos_tables(rows, d):
    quarter = d // 4
    omega = 1.0 / (10000.0 ** (jnp.arange(quarter, dtype=F32) / quarter))
    er = jnp.arange(rows, dtype=F32)[:, None] * omega
    ec = jnp.arange(GRID_W, dtype=F32)[:, None] * omega
    er = jnp.concatenate([jnp.sin(er), jnp.cos(er)], -1)
    ec = jnp.concatenate([jnp.sin(ec), jnp.cos(ec)], -1)
    return er, ec


def _filter_positions(L):
    pos = jnp.arange(L, dtype=F32)[:, None]
    t = jnp.linspace(0.0, 1.0, L, dtype=F32)[:, None]
    bands = jnp.linspace(1e-4, N_BANDS - 1, N_BANDS, dtype=F32)[None, :]
    ang = 2.0 * math.pi * bands * pos / L
    z = jnp.concatenate([t, jnp.cos(ang), -jnp.sin(ang)], -1)
    return z, t


def _pad_cols(a, width):
    return jnp.pad(a, ((0, 0), (0, width - a.shape[1])))


def _pad_rows(a, height):
    return jnp.pad(a, ((0, height - a.shape[0]), (0, 0)))


def kernel(x, c, ctx, c_ctx, w_ada, b_ada, w_in, conv_rw, conv_hy, conv_hy_b, w0_f, w_up_f, a0_f, a_up_f, w0_b, w_up_b, a0_b, a_up_b, k_k, k_a, r_k, g_up, gn_g, gn_b, filt_w1, filt_b1, filt_w2, filt_b2, filt_w3, sin_freq, hy_bias, w_out, ln1_g, ln1_b, ffn_w1, ffn_w3, ffn_w2, ln2_g, ln2_b):
    bsz, L, d = x.shape
    rw = k_k.shape[1]
    hy = hy_bias.shape[1]
    dl, il, gl = w_up_f.shape[1], a_up_f.shape[1], g_up.shape[1]
    lo_w = 4 * LANE
    rkv_cols = 3 * rw

    wi = w_in[0]
    lo0 = rkv_cols

    def lora_layout(a):
        return jnp.concatenate([_pad_cols(a[:, lo0:lo0 + dl], LANE),
                                _pad_cols(a[:, lo0 + dl:lo0 + dl + il], LANE),
                                _pad_cols(a[:, lo0 + dl + il:lo0 + dl + il + gl], 2 * LANE)], axis=1)

    w_lat = _weight_layout(jnp.transpose(wi), rkv_cols=rkv_cols, lora_widths=(dl, il, gl), hy_cols=3 * hy)
    cw_lat = jnp.concatenate([conv_rw[0][:, :rkv_cols], conv_hy[0], lora_layout(conv_rw[0])], axis=1)
    cb_lat = jnp.concatenate([jnp.zeros((rkv_cols,), F32), conv_hy_b[0], jnp.zeros((lo_w,), F32)])[None, :]
    wup = _pad_rows(jnp.concatenate([w_up_f[0], w_up_b[0]], axis=1), LANE)
    aup = _pad_rows(jnp.concatenate([a_up_f[0], a_up_b[0]], axis=1), LANE)
    gup = _pad_rows(g_up[0], 2 * LANE)
    w0 = jnp.concatenate([w0_f[0], w0_b[0]])
    a0 = jnp.concatenate([a0_f[0], a0_b[0]])

    cc = _pad_rows(jnp.concatenate([c, c_ctx[None, :]], axis=0), SUB)
    mod = _adaln(cc, w_ada[0], b_ada[0])
    mod3 = mod.reshape(SUB, 1, 6 * d)

    er, ec = _pos_tables(L // GRID_W, d)

    tn = lo_w
    p_lat = _inproj(x, er, ec, mod3, w_lat, cw_lat, cb_lat, tm=1024, tn=tn, add_pos=True, ctx_row=None)
    ctx_tiles = list(range(rkv_cols // tn)) + [(rkv_cols + 3 * hy) // tn]
    p_ctx = _inproj(ctx, er, ec, mod3, w_lat, cw_lat, cb_lat, tm=ctx.shape[1], tn=tn, add_pos=False,
                    ctx_row=bsz, col_tiles=ctx_tiles)

    lo_cols = (rkv_cols + 3 * hy, rkv_cols + 3 * hy + lo_w)
    r, v, kk, gate, lw, kh, bvec = _rwkv_prep(p_lat, p_ctx, wup, aup, gup, w0, a0,
                                              k_k[0], k_a[0], rw=rw, rkv_cols=rkv_cols, lo_cols=lo_cols)
    o_f, o_b = _rwkv_scan(r, v, kk, lw, kh, bvec, r_k[0], gn_g[0], gn_b[0], n_lat_tokens=L)

    tabs = _fft_tables(L)
    zpos, tcol = _filter_positions(L)
    max_decay = math.log(DECAY_TARGET) / FAST_DECAY_PCT
    min_decay = math.log(DECAY_TARGET) / SLOW_DECAY_PCT
    deltas = jnp.abs(jnp.linspace(min_decay, max_decay, hy, dtype=F32))[None, :]
    kf = _filter_spectrum(_pad_cols(zpos, LANE), tcol, _pad_rows(filt_w1[0], LANE), filt_b1[0], filt_w2[0],
                          filt_b2[0], filt_w3[0], sin_freq[0], deltas, tabs, ct=LANE)
    o_hy = _hyena_conv(p_lat, hy_bias[0], kf, tabs, hy_col0=rkv_cols, hy_dim=hy, ct=LANE)

    x1 = _outproj(o_f, o_b, gate, o_hy, x, er, ec, mod3, w_out[0].astype(BF16), ln1_g[0], ln1_b[0], tm=512)
    return _ffn(x1, mod3, ffn_w1[0].astype(BF16), ffn_w3[0].astype(BF16), ffn_w2[0].astype(BF16),
                ln2_g[0], ln2_b[0], tm=1024, tf=512)
```
